```python
import jax, jax.numpy as jnp
from jax import lax
import numpy as np

D_MODEL = 1024
BATCH = 8
SEQ = 8192
DEPTH = 2

PLE_DIM = 256
BLOCK = 128
EPS = 1e-6
ROPE_THETA = 10000.0
RET_HEADS = 4
RET_DK = 128
RET_DV = 128
RET_QKW = RET_HEADS * RET_DK
RET_VW = RET_HEADS * RET_DV
SGU_GROUPS = 4
SGU_GROUP_CH = 128
SGU_WIDTH = SGU_GROUPS * SGU_GROUP_CH
ATT_HEADS = 8
ATT_KV_HEADS = 2
ATT_DH = 64
ATT_GROUP = ATT_HEADS // ATT_KV_HEADS
ATT_QW = ATT_HEADS * ATT_DH
ATT_KVW = ATT_KV_HEADS * ATT_DH
WINDOW = 128
NEG_INF = -1e30
PEER_HEADS = 8
PEER_KEYS = 128
PEER_EXPERTS = PEER_KEYS * PEER_KEYS
PEER_QDIM = 256
PEER_TOPK = 16
IN_WIDTHS = (RET_QKW, RET_QKW, RET_VW, RET_VW,
             SGU_WIDTH, SGU_WIDTH,
             ATT_QW, ATT_KVW, ATT_KVW,
             D_MODEL, D_MODEL, D_MODEL)
D_IN = sum(IN_WIDTHS)

kernel_name = 'hybrid_retention_sgu_swa_peer_encoder'


def rms_norm(x, g):
    xf = x.astype(jnp.float32)
    y = xf * lax.rsqrt(jnp.mean(xf * xf, axis=-1, keepdims=True) + EPS)
    return (y * g.astype(jnp.float32)).astype(x.dtype)


def rope_tables(seq, dim):
    inv = 1.0 / (ROPE_THETA ** (jnp.arange(0, dim, 2, dtype=jnp.float32) / dim))
    ang = jnp.arange(seq, dtype=jnp.float32)[:, None] * inv[None, :]
    return jnp.cos(ang), jnp.sin(ang)


def apply_rope(x, cos, sin):
    x1, x2 = jnp.split(x.astype(jnp.float32), 2, axis=-1)
    c = cos[None, :, None, :]
    s = sin[None, :, None, :]
    return jnp.concatenate([x1 * c - x2 * s, x2 * c + x1 * s], axis=-1).astype(x.dtype)


def split_columns(z):
    offs = [int(o) for o in np.cumsum(IN_WIDTHS)[:-1]]
    return jnp.split(z, offs, axis=-1)


def retention_chunkwise(q, k, v, log_gamma, strict):
    B, H, S, dk = q.shape
    dv = v.shape[-1]
    C = BLOCK
    N = S // C
    qc = q.reshape(B, H, N, C, dk)
    kc = k.reshape(B, H, N, C, dk)
    vc = v.reshape(B, H, N, C, dv)
    idx = jnp.arange(C, dtype=jnp.float32)
    diff = idx[:, None] - idx[None, :]
    mask = (diff > 0) if strict else (diff >= 0)
    lg = log_gamma[:, None, None]
    decay = jnp.where(mask[None], jnp.exp(lg * jnp.maximum(diff, 0.0)[None]), 0.0)
    scores = jnp.einsum('bhncd,bhnjd->bhncj', qc, kc) * decay[None, :, None]
    out_inner = jnp.einsum('bhncj,bhnje->bhnce', scores, vc)
    k_w = jnp.exp(log_gamma[:, None] * (C - 1 - idx)[None, :])
    kv = jnp.einsum('bhncd,hc,bhnce->bhnde', kc, k_w, vc)
    chunk_decay = jnp.exp(log_gamma * C)[None, :, None, None]

    def step(state, kv_n):
        return state * chunk_decay + kv_n, state

    _, prev = lax.scan(step, jnp.zeros((B, H, dk, dv), jnp.float32), jnp.moveaxis(kv, 2, 0))
    prev = jnp.moveaxis(prev, 0, 2)
    q_w = jnp.exp(log_gamma[:, None] * (idx + 1.0)[None, :])
    out_cross = jnp.einsum('bhncd,hc,bhnde->bhnce', qc, q_w, prev)
    return (out_inner + out_cross).reshape(B, H, S, dv)


def retention_branch(q, k, v, g, decay_logit, gn_gain, cos, sin):
    B, S, _ = q.shape
    q = apply_rope(q.reshape(B, S, RET_HEADS, RET_DK), cos, sin)
    k = apply_rope(k.reshape(B, S, RET_HEADS, RET_DK), cos, sin)
    v = v.reshape(B, S, RET_HEADS, RET_DV)
    qh = jnp.transpose(q, (0, 2, 1, 3)).astype(jnp.float32)
    kh = jnp.transpose(k, (0, 2, 1, 3)).astype(jnp.float32) * (RET_DK ** -0.5)
    vh = jnp.transpose(v, (0, 2, 1, 3)).astype(jnp.float32)
    log_g = jax.nn.log_sigmoid(decay_logit.astype(jnp.float32))
    fwd = retention_chunkwise(qh, kh, vh, log_g[0], strict=False)
    bwd = retention_chunkwise(qh[:, :, ::-1], kh[:, :, ::-1], vh[:, :, ::-1], log_g[1], strict=True)[:, :, ::-1]
    y = jnp.transpose(fwd + bwd, (0, 2, 1, 3))
    y = y * lax.rsqrt(jnp.mean(y * y, axis=-1, keepdims=True) + EPS)
    y = (y * gn_gain.astype(jnp.float32).reshape(RET_HEADS, RET_DV)).reshape(B, S, RET_VW)
    return (jax.nn.silu(g.astype(jnp.float32)) * y).astype(q.dtype)


def sgu_branch(u, v, ln_g, ln_b, w_s, b_s):
    B, S, _ = u.shape
    N = S // BLOCK
    uf = jax.nn.gelu(u.astype(jnp.float32), approximate=False)
    vf = jax.nn.gelu(v.astype(jnp.float32), approximate=False)
    mu = jnp.mean(vf, axis=-1, keepdims=True)
    var = jnp.mean(jnp.square(vf - mu), axis=-1, keepdims=True)
    vf = (vf - mu) * lax.rsqrt(var + EPS) * ln_g.astype(jnp.float32) + ln_b.astype(jnp.float32)
    vf = vf.reshape(B, N, BLOCK, SGU_GROUPS, SGU_GROUP_CH)
    mixed = jnp.einsum('gij,bnjgc->bnigc', w_s.astype(jnp.float32), vf)
    mixed = mixed + jnp.transpose(b_s.astype(jnp.float32))[None, None, :, :, None]
    return (uf * mixed.reshape(B, S, SGU_WIDTH)).astype(u.dtype)


def window_attention(q, k, v, q_gain, k_gain, sink, cos, sin):
    B, S, _ = q.shape
    N = S // BLOCK
    q = apply_rope(rms_norm(q.reshape(B, S, ATT_HEADS, ATT_DH), q_gain), cos, sin)
    k = apply_rope(rms_norm(k.reshape(B, S, ATT_KV_HEADS, ATT_DH), k_gain), cos, sin)
    v = v.reshape(B, S, ATT_KV_HEADS, ATT_DH)
    pad = ((0, 0), (BLOCK, BLOCK), (0, 0), (0, 0))
    kp = jnp.pad(k, pad).reshape(B, N + 2, BLOCK, ATT_KV_HEADS, ATT_DH)
    vp = jnp.pad(v, pad).reshape(B, N + 2, BLOCK, ATT_KV_HEADS, ATT_DH)
    kb = jnp.concatenate([kp[:, :-2], kp[:, 1:-1], kp[:, 2:]], axis=2)
    vb = jnp.concatenate([vp[:, :-2], vp[:, 1:-1], vp[:, 2:]], axis=2)
    qb = q.reshape(B, N, BLOCK, ATT_KV_HEADS, ATT_GROUP, ATT_DH)
    s = jnp.einsum('bnikgd,bnjkd->bnkgij', qb.astype(jnp.float32), kb.astype(jnp.float32)) * (ATT_DH ** -0.5)
    qpos = jnp.arange(BLOCK)[:, None] + BLOCK
    kpos = jnp.arange(3 * BLOCK)[None, :]
    abs_k = jnp.arange(N)[:, None, None] * BLOCK - BLOCK + kpos[None]
    valid = (jnp.abs(qpos - kpos)[None] <= WINDOW) & (abs_k >= 0) & (abs_k < S)
    s = jnp.where(valid[None, :, None, None], s, NEG_INF)
    sk = sink.astype(jnp.float32).reshape(ATT_KV_HEADS, ATT_GROUP)[None, None, :, :, None, None]
    m = jnp.maximum(jnp.max(s, axis=-1, keepdims=True), sk)
    e = jnp.exp(s - m)
    prob = e / (jnp.sum(e, axis=-1, keepdims=True) + jnp.exp(sk - m))
    o = jnp.einsum('bnkgij,bnjkd->bnikgd', prob, vb.astype(jnp.float32))
    return o.reshape(B, S, ATT_QW).astype(q.dtype)


def peer_ffn(xn, wq, sub_keys, u_tab, v_tab):
    B, S, D = xn.shape
    xb = xn.reshape(-1, BLOCK, D)
    kf = sub_keys.astype(jnp.float32)

    def block(xt):
        T = xt.shape[0]
        q = (xt @ wq).astype(jnp.float32).reshape(T, PEER_HEADS, 2, PEER_QDIM // 2)
        s = jnp.einsum('thpc,hpkc->thpk', q, kf)
        s1, i1 = lax.top_k(s[:, :, 0], PEER_TOPK)
        s2, i2 = lax.top_k(s[:, :, 1], PEER_TOPK)
        cand_s = (s1[..., :, None] + s2[..., None, :]).reshape(T, PEER_HEADS, PEER_TOPK * PEER_TOPK)
        cand_e = (i1[..., :, None] * PEER_KEYS + i2[..., None, :]).reshape(T, PEER_HEADS, PEER_TOPK * PEER_TOPK)
        top_s, sel = lax.top_k(cand_s, PEER_TOPK)
        eid = jnp.take_along_axis(cand_e, sel, axis=-1)
        gate = jax.nn.softmax(top_s, axis=-1)
        act = jax.nn.gelu(jnp.einsum('thkd,td->thk', u_tab[eid], xt).astype(jnp.float32), approximate=False)
        w = (gate * act).astype(xt.dtype)
        return jnp.einsum('thk,thkd->td', w, v_tab[eid])

    return lax.map(block, xb).reshape(B, S, D)


def setup_inputs(seed: int = 0) -> dict:
    key = jax.random.key(seed)
    ks = jax.random.split(key, 26)
    nrm = lambda k, shp, sc: jax.random.normal(k, shp, jnp.float32) * sc
    gain = lambda k, shp: 1.0 + 0.05 * jax.random.normal(k, shp, jnp.float32)
    base_decay = jnp.log(2.0 ** (5.0 + jnp.arange(RET_HEADS, dtype=jnp.float32)) - 1.0)
    return {
        'x': nrm(ks[0], (BATCH, SEQ, D_MODEL), 1.0),
        'p': nrm(ks[1], (DEPTH, BATCH, SEQ, PLE_DIM), 1.0),
        'norm_mix': gain(ks[2], (DEPTH, D_MODEL)),
        'w_in': nrm(ks[3], (DEPTH, D_MODEL, D_IN), D_MODEL ** -0.5),
        'ret_decay': base_decay[None, None, :] + nrm(ks[4], (DEPTH, 2, RET_HEADS), 0.1),
        'ret_norm': gain(ks[5], (DEPTH, RET_VW)),
        'sgu_ln_g': gain(ks[6], (DEPTH, SGU_WIDTH)),
        'sgu_ln_b': nrm(ks[7], (DEPTH, SGU_WIDTH), 0.05),
        'sgu_w': nrm(ks[8], (DEPTH, SGU_GROUPS, BLOCK, BLOCK), BLOCK ** -0.5),
        'sgu_b': 1.0 + nrm(ks[9], (DEPTH, SGU_GROUPS, BLOCK), 0.1),
        'att_q_norm': gain(ks[10], (DEPTH, ATT_DH)),
        'att_k_norm': gain(ks[11], (DEPTH, ATT_DH)),
        'att_sink': nrm(ks[12], (DEPTH, ATT_HEADS), 0.5),
        'w_proj_ret': nrm(ks[13], (DEPTH, RET_VW, D_MODEL), RET_VW ** -0.5),
        'w_proj_sgu': nrm(ks[14], (DEPTH, SGU_WIDTH, D_MODEL), SGU_WIDTH ** -0.5),
        'w_proj_att': nrm(ks[15], (DEPTH, ATT_QW, D_MODEL), ATT_QW ** -0.5),
        'w_out': nrm(ks[16], (DEPTH, D_MODEL, D_MODEL), D_MODEL ** -0.5),
        'norm_ffn': gain(ks[17], (DEPTH, D_MODEL)),
        'peer_wq': nrm(ks[18], (DEPTH, D_MODEL, PEER_HEADS * PEER_QDIM), D_MODEL ** -0.5),
        'peer_keys': nrm(ks[19], (DEPTH, PEER_HEADS, 2, PEER_KEYS, PEER_QDIM // 2), (PEER_QDIM // 2) ** -0.5),
        'peer_u': nrm(ks[20], (DEPTH, PEER_EXPERTS, D_MODEL), D_MODEL ** -0.5),
        'peer_v': nrm(ks[21], (DEPTH, PEER_EXPERTS, D_MODEL), PEER_HEADS ** -0.5),
        'norm_ple': gain(ks[22], (DEPTH, D_MODEL)),
        'ple_gate': nrm(ks[23], (DEPTH, D_MODEL, D_MODEL), D_MODEL ** -0.5),
        'ple_proj': nrm(ks[24], (DEPTH, PLE_DIM, D_MODEL), PLE_DIM ** -0.5),
    }


def reference(x, p, norm_mix, w_in, ret_decay, ret_norm, sgu_ln_g, sgu_ln_b, sgu_w, sgu_b,
              att_q_norm, att_k_norm, att_sink, w_proj_ret, w_proj_sgu, w_proj_att, w_out,
              norm_ffn, peer_wq, peer_keys, peer_u, peer_v, norm_ple, ple_gate, ple_proj):
    S = x.shape[1]
    cos_r, sin_r = rope_tables(S, RET_DK)
    cos_a, sin_a = rope_tables(S, ATT_DH)
    h = x
    for i in range(DEPTH):
        xn = rms_norm(h, norm_mix[i])
        (q_r, k_r, v_r, g_r, u_s, v_s, q_a, k_a, v_a,
         gate_r, gate_s, gate_a) = split_columns(xn @ w_in[i])
        y_r = retention_branch(q_r, k_r, v_r, g_r, ret_decay[i], ret_norm[i], cos_r, sin_r)
        y_s = sgu_branch(u_s, v_s, sgu_ln_g[i], sgu_ln_b[i], sgu_w[i], sgu_b[i])
        y_a = window_attention(q_a, k_a, v_a, att_q_norm[i], att_k_norm[i], att_sink[i], cos_a, sin_a)
        merged = (jax.nn.sigmoid(gate_r) * (y_r @ w_proj_ret[i])
                  + jax.nn.sigmoid(gate_s) * (y_s @ w_proj_sgu[i])
                  + jax.nn.sigmoid(gate_a) * (y_a @ w_proj_att[i]))
        h = h + merged @ w_out[i]
        h = h + peer_ffn(rms_norm(h, norm_ffn[i]), peer_wq[i], peer_keys[i], peer_u[i], peer_v[i])
        h = h + jax.nn.sigmoid(rms_norm(h, norm_ple[i]) @ ple_gate[i]) * (p[i] @ ple_proj[i])
    return h
```

```python
import functools
import math

import jax
import jax.numpy as jnp
from jax import lax
from jax.experimental import pallas as pl
from jax.experimental.pallas import tpu as pltpu

F32 = jnp.float32
BF16 = jnp.bfloat16

D_MODEL = 1024
PLE_DIM = 256
CHUNK = 128
EPS = 1e-6
ROPE_THETA = 10000.0
RET_HEADS = 4
RET_DK = 128
SGU_GROUPS = 4
SGU_WIDTH = 512
ATT_HEADS = 8
ATT_KV_HEADS = 2
ATT_DH = 64
ATT_GROUP = ATT_HEADS // ATT_KV_HEADS
NEG_INF = -1e30
PEER_HEADS = 8
PEER_KEYS = 128
PEER_QDIM = 256
PEER_TOPK = 16
PEER_PICKS = PEER_HEADS * PEER_TOPK

Z_GATE_R, Z_GATE_S, Z_GATE_A = 0, 1024, 2048
Z_QR, Z_KR, Z_VR, Z_GR = 3072, 3584, 4096, 4608
Z_US, Z_VS = 5120, 5632
Z_QA, Z_KA, Z_VA = 6144, 6656, 6784
D_IN = 6912

LANE = 128
VMEM_LIMIT = 56 * 1024 * 1024


def _params(*sem):
    return pltpu.CompilerParams(dimension_semantics=sem, vmem_limit_bytes=VMEM_LIMIT)


def _gelu(x):
    return 0.5 * x * (1.0 + lax.erf(x * (1.0 / math.sqrt(2.0))))


def _sigmoid(x):
    return 1.0 / (1.0 + jnp.exp(-x))


def _rms(x, g):
    return x * lax.rsqrt(jnp.mean(x * x, axis=-1, keepdims=True) + EPS) * g


def _inproj_body(x_ref, g_ref, w_ref, o_ref, xn_ref):
    @pl.when(pl.program_id(1) == 0)
    def _():
        xn_ref[...] = _rms(x_ref[...], g_ref[...]).astype(BF16)

    o_ref[...] = jnp.dot(xn_ref[...], w_ref[...], preferred_element_type=F32)


def _norm_matmul(h, gain, w_bf, tm, tn):
    t, d = h.shape
    n = w_bf.shape[1]
    return pl.pallas_call(
        _inproj_body,
        grid=(t // tm, n // tn),
        in_specs=[pl.BlockSpec((tm, d), lambda i, j: (i, 0)),
                  pl.BlockSpec((1, d), lambda i, j: (0, 0)),
                  pl.BlockSpec((d, tn), lambda i, j: (0, j))],
        out_specs=pl.BlockSpec((tm, tn), lambda i, j: (i, j)),
        out_shape=jax.ShapeDtypeStruct((t, n), F32),
        scratch_shapes=[pltpu.VMEM((tm, d), BF16)],
        compiler_params=_params("parallel", "arbitrary"),
        name="norm_matmul",
    )(h, gain.reshape(1, d), w_bf)


def _rope128(x, cos, sin_signed):
    return x * cos + pltpu.roll(x, 64, 1) * sin_signed


def _ret_bwd_body(q_ref, k_ref, v_ref, cos_ref, sin_ref, qw_ref, kw_ref, cd_ref, o_ref, st_ref, *, nchunk):
    @pl.when(pl.program_id(2) == 0)
    def _():
        st_ref[...] = jnp.zeros_like(st_ref)

    qw = qw_ref[0]
    kw = kw_ref[0]
    cd = cd_ref[0, 0:1, :]
    for c in reversed(range(nchunk)):
        rows = pl.ds(c * CHUNK, CHUNK)
        cos = cos_ref[rows, :]
        sin = sin_ref[rows, :]
        q = _rope128(q_ref[rows, :], cos, sin)
        k = _rope128(k_ref[rows, :], cos, sin) * (RET_DK ** -0.5)
        v = v_ref[rows, :]
        st = st_ref[...]
        o_ref[rows, :] = jnp.dot((q * qw).astype(BF16), st.astype(BF16), preferred_element_type=F32)
        kv = jnp.dot((k * kw).T.astype(BF16), v.astype(BF16), preferred_element_type=F32)
        st_ref[...] = st * cd + kv


def _ret_fwd_body(q_ref, k_ref, v_ref, g_ref, yb_ref, cos_ref, sin_ref, dm_ref, qw_ref, kw_ref, cd_ref,
                  gn_ref, o_ref, st_ref, *, nchunk):
    @pl.when(pl.program_id(2) == 0)
    def _():
        st_ref[...] = jnp.zeros_like(st_ref)

    qw = qw_ref[0]
    kw = kw_ref[0]
    cd = cd_ref[0, 0:1, :]
    dm = dm_ref[0]
    gn = gn_ref[0, 0:1, :]
    for c in range(nchunk):
        rows = pl.ds(c * CHUNK, CHUNK)
        cos = cos_ref[rows, :]
        sin = sin_ref[rows, :]
        q = _rope128(q_ref[rows, :], cos, sin)
        k = _rope128(k_ref[rows, :], cos, sin) * (RET_DK ** -0.5)
        v = v_ref[rows, :].astype(BF16)
        st = st_ref[...]
        s = lax.dot_general(q.astype(BF16), k.astype(BF16), (((1,), (1,)), ((), ())),
                            preferred_element_type=F32) * dm
        y = jnp.dot(s.astype(BF16), v, preferred_element_type=F32)
        y += jnp.dot((q * qw).astype(BF16), st.astype(BF16), preferred_element_type=F32)
        y += yb_ref[rows, :]
        kv = jnp.dot((k * kw).T.astype(BF16), v, preferred_element_type=F32)
        st_ref[...] = st * cd + kv
        y = y * lax.rsqrt(jnp.mean(y * y, axis=-1, keepdims=True) + EPS) * gn
        g = g_ref[rows, :]
        o_ref[rows, :] = (g * _sigmoid(g) * y).astype(o_ref.dtype)


def _retention(z, ret_decay, ret_norm, cos, sin, batch, seq, ts):
    t = z.shape[0]
    nchunk = ts // CHUNK
    nstep = seq // ts
    hd = RET_HEADS
    log_g = jax.nn.log_sigmoid(ret_decay.astype(F32))
    idx = jnp.arange(CHUNK, dtype=F32)
    diff = idx[:, None] - idx[None, :]
    lf = log_g[0][:, None, None]
    lb = log_g[1][:, None, None]
    dmat = jnp.where(diff[None] >= 0, jnp.exp(lf * jnp.maximum(diff, 0.0)[None]),
                     jnp.exp(lb * jnp.maximum(-diff, 0.0)[None]))
    bc = lambda a: jnp.broadcast_to(a[:, :, None], (hd, a.shape[1], LANE))
    qw_f = bc(jnp.exp(log_g[0][:, None] * (idx + 1.0)[None, :]))
    kw_f = bc(jnp.exp(log_g[0][:, None] * (CHUNK - 1 - idx)[None, :]))
    qw_b = bc(jnp.exp(log_g[1][:, None] * (CHUNK - idx)[None, :]))
    kw_b = bc(jnp.exp(log_g[1][:, None] * idx[None, :]))
    cd_f = jnp.broadcast_to(jnp.exp(log_g[0] * CHUNK)[:, None, None], (hd, 8, LANE))
    cd_b = jnp.broadcast_to(jnp.exp(log_g[1] * CHUNK)[:, None, None], (hd, 8, LANE))
    gn = jnp.broadcast_to(ret_norm.astype(F32).reshape(hd, 1, LANE), (hd, 8, LANE))

    def zspec(col0, rev):
        cb = col0 // LANE
        if rev:
            return pl.BlockSpec((ts, LANE), lambda b, h, s: (b * nstep + nstep - 1 - s, cb + h))
        return pl.BlockSpec((ts, LANE), lambda b, h, s: (b * nstep + s, cb + h))

    def tspec(rev):
        if rev:
            return pl.BlockSpec((ts, LANE), lambda b, h, s: (nstep - 1 - s, 0))
        return pl.BlockSpec((ts, LANE), lambda b, h, s: (s, 0))

    hspec = lambda r: pl.BlockSpec((1, r, LANE), lambda b, h, s: (h, 0, 0))

    yb = pl.pallas_call(
        functools.partial(_ret_bwd_body, nchunk=nchunk),
        grid=(batch, hd, nstep),
        in_specs=[zspec(Z_QR, True), zspec(Z_KR, True), zspec(Z_VR, True), tspec(True), tspec(True),
                  hspec(CHUNK), hspec(CHUNK), hspec(8)],
        out_specs=pl.BlockSpec((ts, LANE), lambda b, h, s: (b * nstep + nstep - 1 - s, h)),
        out_shape=jax.ShapeDtypeStruct((t, hd * LANE), F32),
        scratch_shapes=[pltpu.VMEM((RET_DK, LANE), F32)],
        compiler_params=_params("parallel", "parallel", "arbitrary"),
        name="retention_bwd",
    )(z, z, z, cos, sin, qw_b, kw_b, cd_b)

    return pl.pallas_call(
        functools.partial(_ret_fwd_body, nchunk=nchunk),
        grid=(batch, hd, nstep),
        in_specs=[zspec(Z_QR, False), zspec(Z_KR, False), zspec(Z_VR, False), zspec(Z_GR, False),
                  pl.BlockSpec((ts, LANE), lambda b, h, s: (b * nstep + s, h)),
                  tspec(False), tspec(False), hspec(CHUNK), hspec(CHUNK), hspec(CHUNK), hspec(8), hspec(8)],
        out_specs=pl.BlockSpec((ts, LANE), lambda b, h, s: (b * nstep + s, h)),
        out_shape=jax.ShapeDtypeStruct((t, hd * LANE), BF16),
        scratch_shapes=[pltpu.VMEM((RET_DK, LANE), F32)],
        compiler_params=_params("parallel", "parallel", "arbitrary"),
        name="retention_fwd",
    )(z, z, z, z, yb, cos, sin, dmat, qw_f, kw_f, cd_f, gn)


def _sgu_body(u_ref, v_ref, lg_ref, lb_ref, w_ref, b_ref, o_ref, *, nchunk):
    lg = lg_ref[...]
    lb = lb_ref[...]
    for c in range(nchunk):
        rows = pl.ds(c * CHUNK, CHUNK)
        vf = _gelu(v_ref[rows, :])
        mu = jnp.mean(vf, axis=-1, keepdims=True)
        vc = vf - mu
        var = jnp.mean(vc * vc, axis=-1, keepdims=True)
        vn = (vc * lax.rsqrt(var + EPS) * lg + lb).astype(BF16)
        for g in range(SGU_GROUPS):
            cols = slice(g * LANE, (g + 1) * LANE)
            mixed = jnp.dot(w_ref[g], vn[:, cols], preferred_element_type=F32) + b_ref[g]
            uf = _gelu(u_ref[rows, cols])
            o_ref[rows, cols] = (uf * mixed).astype(o_ref.dtype)


def _sgu(z, ln_g, ln_b, w_s, b_s, ts):
    t = z.shape[0]
    bias = jnp.broadcast_to(b_s.astype(F32)[:, :, None], (SGU_GROUPS, CHUNK, LANE))
    return pl.pallas_call(
        functools.partial(_sgu_body, nchunk=ts // CHUNK),
        grid=(t // ts,),
        in_specs=[pl.BlockSpec((ts, SGU_WIDTH), lambda i: (i, Z_US // SGU_WIDTH)),
                  pl.BlockSpec((ts, SGU_WIDTH), lambda i: (i, Z_VS // SGU_WIDTH)),
                  pl.BlockSpec((1, SGU_WIDTH), lambda i: (0, 0)),
                  pl.BlockSpec((1, SGU_WIDTH), lambda i: (0, 0)),
                  pl.BlockSpec((SGU_GROUPS, CHUNK, CHUNK), lambda i: (0, 0, 0)),
                  pl.BlockSpec((SGU_GROUPS, CHUNK, LANE), lambda i: (0, 0, 0))],
        out_specs=pl.BlockSpec((ts, SGU_WIDTH), lambda i: (i, 0)),
        out_shape=jax.ShapeDtypeStruct((t, SGU_WIDTH), BF16),
        compiler_params=_params("parallel"),
        name="spatial_gating",
    )(z, z, ln_g.reshape(1, -1).astype(F32), ln_b.reshape(1, -1).astype(F32), w_s.astype(BF16), bias)


def _pair_norm_rope(x, gain, cos, sin_up, sin_dn, low):
    sq = x * x
    lo = jnp.sum(jnp.where(low, sq, 0.0), axis=-1, keepdims=True)
    hi = jnp.sum(sq, axis=-1, keepdims=True) - lo
    ms = jnp.where(low, lo, hi) * (1.0 / ATT_DH)
    xn = x * lax.rsqrt(ms + EPS) * gain
    return xn * cos + pltpu.roll(xn, LANE - 32, 1) * sin_up + pltpu.roll(xn, 32, 1) * sin_dn


def _attn_body(sink_ref, q_ref, kp_ref, k_ref, kn_ref, vp_ref, v_ref, vn_ref,
               cq_ref, suq_ref, sdq_ref, ckp_ref, sukp_ref, sdkp_ref, ckn_ref, sukn_ref, sdkn_ref,
               qg_ref, kg_ref, o_ref, *, nchunk, nstep):
    s_id = pl.program_id(1)
    ts = nchunk * CHUNK
    lane = lax.broadcasted_iota(jnp.int32, (1, LANE), 1)
    low = lane < ATT_DH
    kg = kg_ref[...]
    qg = qg_ref[...]
    k_ext = jnp.concatenate([
        _pair_norm_rope(kp_ref[...], kg, ckp_ref[...], sukp_ref[...], sdkp_ref[...], low),
        _pair_norm_rope(k_ref[...], kg, cq_ref[...], suq_ref[...], sdq_ref[...], low),
        _pair_norm_rope(kn_ref[...], kg, ckn_ref[...], sukn_ref[...], sdkn_ref[...], low)], axis=0)
    v_ext = jnp.concatenate([vp_ref[...], v_ref[...], vn_ref[...]], axis=0).astype(BF16)
    k_lo = jnp.where(low, k_ext, 0.0).astype(BF16)
    k_hi = jnp.where(low, 0.0, k_ext).astype(BF16)
    qi = lax.broadcasted_iota(jnp.int32, (CHUNK, 3 * CHUNK), 0)
    kj = lax.broadcasted_iota(jnp.int32, (CHUNK, 3 * CHUNK), 1)
    band = jnp.abs(qi + CHUNK - kj) <= CHUNK
    for c in range(nchunk):
        rows = pl.ds(c * CHUNK, CHUNK)
        first = jnp.logical_and(s_id == 0, c == 0)
        last = jnp.logical_and(s_id == nstep - 1, c == nchunk - 1)
        valid = band
        if c == 0:
            valid = jnp.logical_and(valid, jnp.logical_or(kj >= CHUNK, jnp.logical_not(first)))
        if c == nchunk - 1:
            valid = jnp.logical_and(valid, jnp.logical_or(kj < 2 * CHUNK, jnp.logical_not(last)))
        kc_lo = k_lo[c * CHUNK:(c + 3) * CHUNK]
        kc_hi = k_hi[c * CHUNK:(c + 3) * CHUNK]
        vc = v_ext[c * CHUNK:(c + 3) * CHUNK]
        cos = cq_ref[rows, :]
        su = suq_ref[rows, :]
        sd = sdq_ref[rows, :]
        for pair in range(ATT_HEADS // 2):
            cols = slice(pair * LANE, (pair + 1) * LANE)
            qp = _pair_norm_rope(q_ref[rows, cols], qg, cos, su, sd, low) * (ATT_DH ** -0.5)
            kv_head = (2 * pair) // ATT_GROUP
            outs = []
            for half in range(2):
                head = 2 * pair + half
                qh = qp if half == kv_head else pltpu.roll(qp, ATT_DH, 1)
                if kv_head == 0:
                    qh = jnp.where(low, qh, 0.0)
                    kc = kc_lo
                else:
                    qh = jnp.where(low, 0.0, qh)
                    kc = kc_hi
                s = lax.dot_general(qh.astype(BF16), kc, (((1,), (1,)), ((), ())), preferred_element_type=F32)
                s = jnp.where(valid, s, NEG_INF)
                sk = sink_ref[head]
                m = jnp.maximum(jnp.max(s, axis=-1, keepdims=True), sk)
                e = jnp.exp(s - m)
                den = jnp.sum(e, axis=-1, keepdims=True) + jnp.exp(sk - m)
                o = jnp.dot(e.astype(BF16), vc, preferred_element_type=F32) / den
                outs.append(o if half == kv_head else pltpu.roll(o, ATT_DH, 1))
            o_ref[rows, cols] = jnp.where(low, outs[0], outs[1]).astype(o_ref.dtype)


def _attention(z, q_gain, k_gain, sink, cos, sin, batch, seq, ts):
    t = z.shape[0]
    nchunk = ts // CHUNK
    nstep = seq // ts
    nblk = seq // CHUNK
    cos2 = jnp.tile(jnp.concatenate([cos, cos], axis=1), (1, 2))
    zero = jnp.zeros_like(sin)
    sin_up = jnp.tile(jnp.concatenate([-sin, zero], axis=1), (1, 2))
    sin_dn = jnp.tile(jnp.concatenate([zero, sin], axis=1), (1, 2))
    qg = jnp.tile(q_gain.astype(F32), 2).reshape(1, LANE)
    kg = jnp.tile(k_gain.astype(F32), 2).reshape(1, LANE)

    kcb, vcb = Z_KA // LANE, Z_VA // LANE
    prev_blk = lambda s: jnp.maximum(s * nchunk - 1, 0)
    next_blk = lambda s: jnp.minimum((s + 1) * nchunk, nblk - 1)
    main = lambda cb: pl.BlockSpec((ts, LANE), lambda b, s: (b * nstep + s, cb))
    prev = lambda cb: pl.BlockSpec((CHUNK, LANE), lambda b, s: (b * nblk + prev_blk(s), cb))
    nxt = lambda cb: pl.BlockSpec((CHUNK, LANE), lambda b, s: (b * nblk + next_blk(s), cb))
    tmain = pl.BlockSpec((ts, LANE), lambda b, s: (s, 0))
    tprev = pl.BlockSpec((CHUNK, LANE), lambda b, s: (prev_blk(s), 0))
    tnext = pl.BlockSpec((CHUNK, LANE), lambda b, s: (next_blk(s), 0))
    one = pl.BlockSpec((1, LANE), lambda b, s: (0, 0))
    return pl.pallas_call(
        functools.partial(_attn_body, nchunk=nchunk, nstep=nstep),
        grid=(batch, nstep),
        in_specs=[pl.BlockSpec(memory_space=pltpu.SMEM),
                  pl.BlockSpec((ts, ATT_HEADS * ATT_DH), lambda b, s: (b * nstep + s, Z_QA // 512)),
                  prev(kcb), main(kcb), nxt(kcb), prev(vcb), main(vcb), nxt(vcb),
                  tmain, tmain, tmain, tprev, tprev, tprev, tnext, tnext, tnext, one, one],
        out_specs=pl.BlockSpec((ts, ATT_HEADS * ATT_DH), lambda b, s: (b * nstep + s, 0)),
        out_shape=jax.ShapeDtypeStruct((t, ATT_HEADS * ATT_DH), BF16),
        compiler_params=_params("parallel", "arbitrary"),
        name="window_attention",
    )(sink.astype(F32), z, z, z, z, z, z, z,
      cos2, sin_up, sin_dn, cos2, sin_up, sin_dn, cos2, sin_up, sin_dn, qg, kg)


def _merge_body(h_ref, gr_ref, gs_ref, ga_ref, yr_ref, ys_ref, ya_ref, wr_ref, ws_ref, wa_ref, wo_ref, o_ref):
    m = _sigmoid(gr_ref[...]) * jnp.dot(yr_ref[...], wr_ref[...], preferred_element_type=F32)
    m += _sigmoid(gs_ref[...]) * jnp.dot(ys_ref[...], ws_ref[...], preferred_element_type=F32)
    m += _sigmoid(ga_ref[...]) * jnp.dot(ya_ref[...], wa_ref[...], preferred_element_type=F32)
    o_ref[...] = h_ref[...] + jnp.dot(m.astype(BF16), wo_ref[...], preferred_element_type=F32)


def _merge(h, z, y_r, y_s, y_a, w_r, w_s, w_a, w_o, tm):
    t, d = h.shape
    row = lambda w: pl.BlockSpec((tm, w), lambda i: (i, 0))
    gate = lambda col0: pl.BlockSpec((tm, d), lambda i: (i, col0 // d))
    full = lambda a: pl.BlockSpec(a.shape, lambda i: (0, 0))
    return pl.pallas_call(
        _merge_body,
        grid=(t // tm,),
        in_specs=[row(d), gate(Z_GATE_R), gate(Z_GATE_S), gate(Z_GATE_A), row(512), row(512), row(512),
                  full(w_r), full(w_s), full(w_a), full(w_o)],
        out_specs=row(d),
        out_shape=jax.ShapeDtypeStruct((t, d), F32),
        compiler_params=_params("parallel"),
        name="branch_merge",
    )(h, z, z, z, y_r, y_s, y_a, w_r, w_s, w_a, w_o)


def _top16(vals, rows):
    n = vals.shape[0]
    out_v, out_i = [], []
    for _ in range(PEER_TOPK):
        m = jnp.max(vals, axis=0, keepdims=True)
        idx = jnp.min(jnp.where(vals == m, rows, n), axis=0, keepdims=True)
        out_v.append(m)
        out_i.append(idx)
        vals = jnp.where(rows == idx, -jnp.inf, vals)
    return out_v, out_i


def _route_body(h_ref, g_ref, wq_ref, keys_ref, hn_ref, eid_ref, gate_ref):
    hn = _rms(h_ref[...], g_ref[...])
    hn_ref[...] = hn
    q = jnp.dot(hn.astype(BF16), wq_ref[...], preferred_element_type=F32).astype(BF16)
    tm = q.shape[0]
    rows = lax.broadcasted_iota(jnp.int32, (PEER_KEYS, tm), 0)
    rows2 = lax.broadcasted_iota(jnp.int32, (PEER_TOPK * PEER_TOPK, tm), 0)
    half = PEER_QDIM // 2
    eids, gates = [], []
    for hd in range(PEER_HEADS):
        sub = []
        for p in range(2):
            qs = q[:, (2 * hd + p) * half:(2 * hd + p + 1) * half]
            s = lax.dot_general(keys_ref[hd, p], qs, (((1,), (1,)), ((), ())), preferred_element_type=F32)
            sub.append(_top16(s, rows))
        (s1, i1), (s2, i2) = sub
        s2c = jnp.concatenate(s2, axis=0)
        i2c = jnp.concatenate(i2, axis=0)
        cand_s = jnp.concatenate([a + s2c for a in s1], axis=0)
        cand_e = jnp.concatenate([a * PEER_KEYS + i2c for a in i1], axis=0)
        top_s, sel = _top16(cand_s, rows2)
        top_e = [jnp.sum(jnp.where(rows2 == i, cand_e, 0), axis=0, keepdims=True) for i in sel]
        ts_ = jnp.concatenate(top_s, axis=0)
        e = jnp.exp(ts_ - top_s[0])
        gates.append(e / jnp.sum(e, axis=0, keepdims=True))
        eids.append(jnp.concatenate(top_e, axis=0))
    eid_ref[...] = jnp.concatenate(eids, axis=0).T
    gate_ref[...] = jnp.concatenate(gates, axis=0).T


def _route(h, gain, wq_bf, keys_bf, tm):
    t, d = h.shape
    return pl.pallas_call(
        _route_body,
        grid=(t // tm,),
        in_specs=[pl.BlockSpec((tm, d), lambda i: (i, 0)),
                  pl.BlockSpec((1, d), lambda i: (0, 0)),
                  pl.BlockSpec(wq_bf.shape, lambda i: (0, 0)),
                  pl.BlockSpec(keys_bf.shape, lambda i: (0, 0, 0, 0))],
        out_specs=[pl.BlockSpec((tm, d), lambda i: (i, 0)),
                   pl.BlockSpec((tm, PEER_PICKS), lambda i: (i, 0)),
                   pl.BlockSpec((tm, PEER_PICKS), lambda i: (i, 0))],
        out_shape=[jax.ShapeDtypeStruct((t, d), F32),
                   jax.ShapeDtypeStruct((t, PEER_PICKS), jnp.int32),
                   jax.ShapeDtypeStruct((t, PEER_PICKS), F32)],
        compiler_params=_params("parallel"),
        name="peer_route",
    )(h, gain.reshape(1, d), wq_bf, keys_bf)


def _expert_body(h_ref, hn_ref, gate_ref, gu_ref, gv_ref, o_ref, *, ntok):
    gate_cols = gate_ref[...].T
    for t in range(ntok):
        rows = pl.ds(t * PEER_PICKS, PEER_PICKS)
        x = hn_ref[t:t + 1, :]
        act = jnp.sum(gu_ref[rows, :] * x, axis=-1, keepdims=True)
        w = gate_cols[:, t:t + 1] * _gelu(act)
        y = jnp.sum(gv_ref[rows, :] * w, axis=0, keepdims=True)
        o_ref[t:t + 1, :] = h_ref[t:t + 1, :] + y


def _experts(h, hn, gate, g_u, g_v, ntok):
    t, d = h.shape
    row = lambda w: pl.BlockSpec((ntok, w), lambda i: (i, 0))
    gat = pl.BlockSpec((ntok * PEER_PICKS, d), lambda i: (i, 0))
    return pl.pallas_call(
        functools.partial(_expert_body, ntok=ntok),
        grid=(t // ntok,),
        in_specs=[row(d), row(d), row(PEER_PICKS), gat, gat],
        out_specs=row(d),
        out_shape=jax.ShapeDtypeStruct((t, d), F32),
        compiler_params=_params("parallel"),
        name="peer_experts",
    )(h, hn, gate, g_u, g_v)


def _ple_body(h_ref, g_ref, wg_ref, p_ref, wp_ref, o_ref):
    h = h_ref[...]
    hn = _rms(h, g_ref[...]).astype(BF16)
    gate = _sigmoid(jnp.dot(hn, wg_ref[...], preferred_element_type=F32))
    emb = jnp.dot(p_ref[...].astype(BF16), wp_ref[...], preferred_element_type=F32)
    o_ref[...] = h + gate * emb


def _ple(h, gain, wg_bf, p, wp_bf, tm):
    t, d = h.shape
    return pl.pallas_call(
        _ple_body,
        grid=(t // tm,),
        in_specs=[pl.BlockSpec((tm, d), lambda i: (i, 0)),
                  pl.BlockSpec((1, d), lambda i: (0, 0)),
                  pl.BlockSpec(wg_bf.shape, lambda i: (0, 0)),
                  pl.BlockSpec((tm, p.shape[1]), lambda i: (i, 0)),
                  pl.BlockSpec(wp_bf.shape, lambda i: (0, 0))],
        out_specs=pl.BlockSpec((tm, d), lambda i: (i, 0)),
        out_shape=jax.ShapeDtypeStruct((t, d), F32),
        compiler_params=_params("parallel"),
        name="layer_embedding",
    )(h, gain.reshape(1, d), wg_bf, p, wp_bf)


def _rope_tables(seq, dim):
    inv = 1.0 / (ROPE_THETA ** (jnp.arange(0, dim, 2, dtype=F32) / dim))
    ang = jnp.arange(seq, dtype=F32)[:, None] * inv[None, :]
    return jnp.cos(ang), jnp.sin(ang)


def _permute_in_columns(w_in):
    return jnp.concatenate([w_in[:, 3840:], w_in[:, :3840]], axis=1)


def kernel(x, p, norm_mix, w_in, ret_decay, ret_norm, sgu_ln_g, sgu_ln_b, sgu_w, sgu_b, att_q_norm, att_k_norm, att_sink, w_proj_ret, w_proj_sgu, w_proj_att, w_out, norm_ffn, peer_wq, peer_keys, peer_u, peer_v, norm_ple, ple_gate, ple_proj):
    batch, seq, d = x.shape
    depth = w_in.shape[0]
    t = batch * seq
    ts = min(512, seq)
    tm = min(512, t)
    cos_r, sin_r = _rope_tables(seq, RET_DK)
    cos_r2 = jnp.concatenate([cos_r, cos_r], axis=1)
    sin_r2 = jnp.concatenate([-sin_r, sin_r], axis=1)
    cos_a, sin_a = _rope_tables(seq, ATT_DH)
    h = x.reshape(t, d)
    for i in range(depth):
        z = _norm_matmul(h, norm_mix[i], _permute_in_columns(w_in[i]).astype(BF16), tm, 1152)
        y_r = _retention(z, ret_decay[i], ret_norm[i], cos_r2, sin_r2, batch, seq, ts)
        y_s = _sgu(z, sgu_ln_g[i], sgu_ln_b[i], sgu_w[i], sgu_b[i], ts)
        y_a = _attention(z, att_q_norm[i], att_k_norm[i], att_sink[i], cos_a, sin_a, batch, seq, ts)
        h = _merge(h, z, y_r, y_s, y_a, w_proj_ret[i].astype(BF16), w_proj_sgu[i].astype(BF16),
                   w_proj_att[i].astype(BF16), w_out[i].astype(BF16), tm)
        hn, eid, gate = _route(h, norm_ffn[i], peer_wq[i].astype(BF16), peer_keys[i].astype(BF16), min(256, t))
        parts = []
        for b in range(batch):
            sl = slice(b * seq, (b + 1) * seq)
            g_u = jnp.take(peer_u[i], eid[sl].reshape(-1), axis=0)
            g_v = jnp.take(peer_v[i], eid[sl].reshape(-1), axis=0)
            parts.append(_experts(h[sl], hn[sl], gate[sl], g_u, g_v, 8))
        h = jnp.concatenate(parts, axis=0)
        h = _ple(h, norm_ple[i], ple_gate[i].astype(BF16), p[i].reshape(t, -1), ple_proj[i].astype(BF16), tm)
    return h.reshape(batch, seq, d)
```

```python
import functools
import math

import jax
import jax.numpy as jnp
from jax import lax
from jax.experimental import pallas as pl
from jax.experimental.pallas import tpu as pltpu
from jax.experimental.pallas import tpu_sc as plsc

F32 = jnp.float32
BF16 = jnp.bfloat16

D_MODEL = 1024
PLE_DIM = 256
CHUNK = 128
EPS = 1e-6
ROPE_THETA = 10000.0
RET_HEADS = 4
RET_DK = 128
SGU_GROUPS = 4
SGU_WIDTH = 512
ATT_HEADS = 8
ATT_KV_HEADS = 2
ATT_DH = 64
ATT_GROUP = ATT_HEADS // ATT_KV_HEADS
NEG_INF = -1e30
PEER_HEADS = 8
PEER_KEYS = 128
PEER_QDIM = 256
PEER_TOPK = 16
PEER_PICKS = PEER_HEADS * PEER_TOPK

Z_GATE_R, Z_GATE_S, Z_GATE_A = 0, 1024, 2048
Z_QR, Z_KR, Z_VR, Z_GR = 3072, 3584, 4096, 4608
Z_US, Z_VS = 5120, 5632
Z_QA, Z_KA, Z_VA = 6144, 6656, 6784
D_IN = 6912

LANE = 128
SC_CORES = 2
SC_SUBCORES = 16
SC_LANES = 16
SC_WORKERS = SC_CORES * SC_SUBCORES
SC_TOKENS = 16
SC_ROWS = 32
SC_COLS = 256
VMEM_LIMIT = 56 * 1024 * 1024


def _params(*sem):
    return pltpu.CompilerParams(dimension_semantics=sem, vmem_limit_bytes=VMEM_LIMIT)


def _gelu(x):
    return 0.5 * x * (1.0 + lax.erf(x * (1.0 / math.sqrt(2.0))))


def _sigmoid(x):
    return 1.0 / (1.0 + jnp.exp(-x))


def _rms(x, g):
    return x * lax.rsqrt(jnp.mean(x * x, axis=-1, keepdims=True) + EPS) * g


def _inproj_body(x_ref, g_ref, w_ref, o_ref, xn_ref):
    @pl.when(pl.program_id(1) == 0)
    def _():
        xn_ref[...] = _rms(x_ref[...], g_ref[...]).astype(BF16)

    o_ref[...] = jnp.dot(xn_ref[...], w_ref[...], preferred_element_type=F32)


def _norm_matmul(h, gain, w_bf, tm, tn):
    t, d = h.shape
    n = w_bf.shape[1]
    return pl.pallas_call(
        _inproj_body,
        grid=(t // tm, n // tn),
        in_specs=[pl.BlockSpec((tm, d), lambda i, j: (i, 0)),
                  pl.BlockSpec((1, d), lambda i, j: (0, 0)),
                  pl.BlockSpec((d, tn), lambda i, j: (0, j))],
        out_specs=pl.BlockSpec((tm, tn), lambda i, j: (i, j)),
        out_shape=jax.ShapeDtypeStruct((t, n), F32),
        scratch_shapes=[pltpu.VMEM((tm, d), BF16)],
        compiler_params=_params("parallel", "arbitrary"),
        name="norm_matmul",
    )(h, gain.reshape(1, d), w_bf)


def _rope128(x, cos, sin_signed):
    return x * cos + pltpu.roll(x, 64, 1) * sin_signed


def _ret_bwd_body(q_ref, k_ref, v_ref, cos_ref, sin_ref, qw_ref, kw_ref, cd_ref, o_ref, st_ref, *, nchunk):
    @pl.when(pl.program_id(2) == 0)
    def _():
        st_ref[...] = jnp.zeros_like(st_ref)

    qw = qw_ref[0]
    kw = kw_ref[0]
    cd = cd_ref[0, 0:1, :]
    for c in reversed(range(nchunk)):
        rows = pl.ds(c * CHUNK, CHUNK)
        cos = cos_ref[rows, :]
        sin = sin_ref[rows, :]
        q = _rope128(q_ref[rows, :], cos, sin)
        k = _rope128(k_ref[rows, :], cos, sin) * (RET_DK ** -0.5)
        v = v_ref[rows, :]
        st = st_ref[...]
        o_ref[rows, :] = jnp.dot((q * qw).astype(BF16), st.astype(BF16), preferred_element_type=F32)
        kv = jnp.dot((k * kw).T.astype(BF16), v.astype(BF16), preferred_element_type=F32)
        st_ref[...] = st * cd + kv


def _ret_fwd_body(q_ref, k_ref, v_ref, g_ref, yb_ref, cos_ref, sin_ref, dm_ref, qw_ref, kw_ref, cd_ref,
                  gn_ref, o_ref, st_ref, *, nchunk):
    @pl.when(pl.program_id(2) == 0)
    def _():
        st_ref[...] = jnp.zeros_like(st_ref)

    qw = qw_ref[0]
    kw = kw_ref[0]
    cd = cd_ref[0, 0:1, :]
    dm = dm_ref[0]
    gn = gn_ref[0, 0:1, :]
    for c in range(nchunk):
        rows = pl.ds(c * CHUNK, CHUNK)
        cos = cos_ref[rows, :]
        sin = sin_ref[rows, :]
        q = _rope128(q_ref[rows, :], cos, sin)
        k = _rope128(k_ref[rows, :], cos, sin) * (RET_DK ** -0.5)
        v = v_ref[rows, :].astype(BF16)
        st = st_ref[...]
        s = lax.dot_general(q.astype(BF16), k.astype(BF16), (((1,), (1,)), ((), ())),
                            preferred_element_type=F32) * dm
        y = jnp.dot(s.astype(BF16), v, preferred_element_type=F32)
        y += jnp.dot((q * qw).astype(BF16), st.astype(BF16), preferred_element_type=F32)
        y += yb_ref[rows, :]
        kv = jnp.dot((k * kw).T.astype(BF16), v, preferred_element_type=F32)
        st_ref[...] = st * cd + kv
        y = y * lax.rsqrt(jnp.mean(y * y, axis=-1, keepdims=True) + EPS) * gn
        g = g_ref[rows, :]
        o_ref[rows, :] = (g * _sigmoid(g) * y).astype(o_ref.dtype)


def _retention(z, ret_decay, ret_norm, cos, sin, batch, seq, ts):
    t = z.shape[0]
    nchunk = ts // CHUNK
    nstep = seq // ts
    hd = RET_HEADS
    log_g = jax.nn.log_sigmoid(ret_decay.astype(F32))
    idx = jnp.arange(CHUNK, dtype=F32)
    diff = idx[:, None] - idx[None, :]
    lf = log_g[0][:, None, None]
    lb = log_g[1][:, None, None]
    dmat = jnp.where(diff[None] >= 0, jnp.exp(lf * jnp.maximum(diff, 0.0)[None]),
                     jnp.exp(lb * jnp.maximum(-diff, 0.0)[None]))
    bc = lambda a: jnp.broadcast_to(a[:, :, None], (hd, a.shape[1], LANE))
    qw_f = bc(jnp.exp(log_g[0][:, None] * (idx + 1.0)[None, :]))
    kw_f = bc(jnp.exp(log_g[0][:, None] * (CHUNK - 1 - idx)[None, :]))
    qw_b = bc(jnp.exp(log_g[1][:, None] * (CHUNK - idx)[None, :]))
    kw_b = bc(jnp.exp(log_g[1][:, None] * idx[None, :]))
    cd_f = jnp.broadcast_to(jnp.exp(log_g[0] * CHUNK)[:, None, None], (hd, 8, LANE))
    cd_b = jnp.broadcast_to(jnp.exp(log_g[1] * CHUNK)[:, None, None], (hd, 8, LANE))
    gn = jnp.broadcast_to(ret_norm.astype(F32).reshape(hd, 1, LANE), (hd, 8, LANE))

    def zspec(col0, rev):
        cb = col0 // LANE
        if rev:
            return pl.BlockSpec((ts, LANE), lambda b, h, s: (b * nstep + nstep - 1 - s, cb + h))
        return pl.BlockSpec((ts, LANE), lambda b, h, s: (b * nstep + s, cb + h))

    def tspec(rev):
        if rev:
            return pl.BlockSpec((ts, LANE), lambda b, h, s: (nstep - 1 - s, 0))
        return pl.BlockSpec((ts, LANE), lambda b, h, s: (s, 0))

    hspec = lambda r: pl.BlockSpec((1, r, LANE), lambda b, h, s: (h, 0, 0))

    yb = pl.pallas_call(
        functools.partial(_ret_bwd_body, nchunk=nchunk),
        grid=(batch, hd, nstep),
        in_specs=[zspec(Z_QR, True), zspec(Z_KR, True), zspec(Z_VR, True), tspec(True), tspec(True),
                  hspec(CHUNK), hspec(CHUNK), hspec(8)],
        out_specs=pl.BlockSpec((ts, LANE), lambda b, h, s: (b * nstep + nstep - 1 - s, h)),
        out_shape=jax.ShapeDtypeStruct((t, hd * LANE), F32),
        scratch_shapes=[pltpu.VMEM((RET_DK, LANE), F32)],
        compiler_params=_params("parallel", "parallel", "arbitrary"),
        name="retention_bwd",
    )(z, z, z, cos, sin, qw_b, kw_b, cd_b)

    return pl.pallas_call(
        functools.partial(_ret_fwd_body, nchunk=nchunk),
        grid=(batch, hd, nstep),
        in_specs=[zspec(Z_QR, False), zspec(Z_KR, False), zspec(Z_VR, False), zspec(Z_GR, False),
                  pl.BlockSpec((ts, LANE), lambda b, h, s: (b * nstep + s, h)),
                  tspec(False), tspec(False), hspec(CHUNK), hspec(CHUNK), hspec(CHUNK), hspec(8), hspec(8)],
        out_specs=pl.BlockSpec((ts, LANE), lambda b, h, s: (b * nstep + s, h)),
        out_shape=jax.ShapeDtypeStruct((t, hd * LANE), BF16),
        scratch_shapes=[pltpu.VMEM((RET_DK, LANE), F32)],
        compiler_params=_params("parallel", "parallel", "arbitrary"),
        name="retention_fwd",
    )(z, z, z, z, yb, cos, sin, dmat, qw_f, kw_f, cd_f, gn)


def _sgu_body(u_ref, v_ref, lg_ref, lb_ref, w_ref, b_ref, o_ref, *, nchunk):
    lg = lg_ref[...]
    lb = lb_ref[...]
    for c in range(nchunk):
        rows = pl.ds(c * CHUNK, CHUNK)
        vf = _gelu(v_ref[rows, :])
        mu = jnp.mean(vf, axis=-1, keepdims=True)
        vc = vf - mu
        var = jnp.mean(vc * vc, axis=-1, keepdims=True)
        vn = (vc * lax.rsqrt(var + EPS) * lg + lb).astype(BF16)
        for g in range(SGU_GROUPS):
            cols = slice(g * LANE, (g + 1) * LANE)
            mixed = jnp.dot(w_ref[g], vn[:, cols], preferred_element_type=F32) + b_ref[g]
            uf = _gelu(u_ref[rows, cols])
            o_ref[rows, cols] = (uf * mixed).astype(o_ref.dtype)


def _sgu(z, ln_g, ln_b, w_s, b_s, ts):
    t = z.shape[0]
    bias = jnp.broadcast_to(b_s.astype(F32)[:, :, None], (SGU_GROUPS, CHUNK, LANE))
    return pl.pallas_call(
        functools.partial(_sgu_body, nchunk=ts // CHUNK),
        grid=(t // ts,),
        in_specs=[pl.BlockSpec((ts, SGU_WIDTH), lambda i: (i, Z_US // SGU_WIDTH)),
                  pl.BlockSpec((ts, SGU_WIDTH), lambda i: (i, Z_VS // SGU_WIDTH)),
                  pl.BlockSpec((1, SGU_WIDTH), lambda i: (0, 0)),
                  pl.BlockSpec((1, SGU_WIDTH), lambda i: (0, 0)),
                  pl.BlockSpec((SGU_GROUPS, CHUNK, CHUNK), lambda i: (0, 0, 0)),
                  pl.BlockSpec((SGU_GROUPS, CHUNK, LANE), lambda i: (0, 0, 0))],
        out_specs=pl.BlockSpec((ts, SGU_WIDTH), lambda i: (i, 0)),
        out_shape=jax.ShapeDtypeStruct((t, SGU_WIDTH), BF16),
        compiler_params=_params("parallel"),
        name="spatial_gating",
    )(z, z, ln_g.reshape(1, -1).astype(F32), ln_b.reshape(1, -1).astype(F32), w_s.astype(BF16), bias)


def _pair_norm_rope(x, gain, cos, sin_up, sin_dn, low):
    sq = x * x
    lo = jnp.sum(jnp.where(low, sq, 0.0), axis=-1, keepdims=True)
    hi = jnp.sum(sq, axis=-1, keepdims=True) - lo
    ms = jnp.where(low, lo, hi) * (1.0 / ATT_DH)
    xn = x * lax.rsqrt(ms + EPS) * gain
    return xn * cos + pltpu.roll(xn, LANE - 32, 1) * sin_up + pltpu.roll(xn, 32, 1) * sin_dn


def _attn_body(sink_ref, q_ref, kp_ref, k_ref, kn_ref, vp_ref, v_ref, vn_ref,
               cq_ref, suq_ref, sdq_ref, ckp_ref, sukp_ref, sdkp_ref, ckn_ref, sukn_ref, sdkn_ref,
               qg_ref, kg_ref, o_ref, *, nchunk, nstep):
    s_id = pl.program_id(1)
    ts = nchunk * CHUNK
    lane = lax.broadcasted_iota(jnp.int32, (1, LANE), 1)
    low = lane < ATT_DH
    kg = kg_ref[...]
    qg = qg_ref[...]
    k_ext = jnp.concatenate([
        _pair_norm_rope(kp_ref[...], kg, ckp_ref[...], sukp_ref[...], sdkp_ref[...], low),
        _pair_norm_rope(k_ref[...], kg, cq_ref[...], suq_ref[...], sdq_ref[...], low),
        _pair_norm_rope(kn_ref[...], kg, ckn_ref[...], sukn_ref[...], sdkn_ref[...], low)], axis=0)
    v_ext = jnp.concatenate([vp_ref[...], v_ref[...], vn_ref[...]], axis=0).astype(BF16)
    k_lo = jnp.where(low, k_ext, 0.0).astype(BF16)
    k_hi = jnp.where(low, 0.0, k_ext).astype(BF16)
    qi = lax.broadcasted_iota(jnp.int32, (CHUNK, 3 * CHUNK), 0)
    kj = lax.broadcasted_iota(jnp.int32, (CHUNK, 3 * CHUNK), 1)
    band = jnp.abs(qi + CHUNK - kj) <= CHUNK
    for c in range(nchunk):
        rows = pl.ds(c * CHUNK, CHUNK)
        first = jnp.logical_and(s_id == 0, c == 0)
        last = jnp.logical_and(s_id == nstep - 1, c == nchunk - 1)
        valid = band
        if c == 0:
            valid = jnp.logical_and(valid, jnp.logical_or(kj >= CHUNK, jnp.logical_not(first)))
        if c == nchunk - 1:
            valid = jnp.logical_and(valid, jnp.logical_or(kj < 2 * CHUNK, jnp.logical_not(last)))
        kc_lo = k_lo[c * CHUNK:(c + 3) * CHUNK]
        kc_hi = k_hi[c * CHUNK:(c + 3) * CHUNK]
        vc = v_ext[c * CHUNK:(c + 3) * CHUNK]
        cos = cq_ref[rows, :]
        su = suq_ref[rows, :]
        sd = sdq_ref[rows, :]
        for pair in range(ATT_HEADS // 2):
            cols = slice(pair * LANE, (pair + 1) * LANE)
            qp = _pair_norm_rope(q_ref[rows, cols], qg, cos, su, sd, low) * (ATT_DH ** -0.5)
            kv_head = (2 * pair) // ATT_GROUP
            outs = []
            for half in range(2):
                head = 2 * pair + half
                qh = qp if half == kv_head else pltpu.roll(qp, ATT_DH, 1)
                if kv_head == 0:
                    qh = jnp.where(low, qh, 0.0)
                    kc = kc_lo
                else:
                    qh = jnp.where(low, 0.0, qh)
                    kc = kc_hi
                s = lax.dot_general(qh.astype(BF16), kc, (((1,), (1,)), ((), ())), preferred_element_type=F32)
                s = jnp.where(valid, s, NEG_INF)
                sk = sink_ref[head]
                m = jnp.maximum(jnp.max(s, axis=-1, keepdims=True), sk)
                e = jnp.exp(s - m)
                den = jnp.sum(e, axis=-1, keepdims=True) + jnp.exp(sk - m)
                o = jnp.dot(e.astype(BF16), vc, preferred_element_type=F32) / den
                outs.append(o if half == kv_head else pltpu.roll(o, ATT_DH, 1))
            o_ref[rows, cols] = jnp.where(low, outs[0], outs[1]).astype(o_ref.dtype)


def _attention(z, q_gain, k_gain, sink, cos, sin, batch, seq, ts):
    t = z.shape[0]
    nchunk = ts // CHUNK
    nstep = seq // ts
    nblk = seq // CHUNK
    cos2 = jnp.tile(jnp.concatenate([cos, cos], axis=1), (1, 2))
    zero = jnp.zeros_like(sin)
    sin_up = jnp.tile(jnp.concatenate([-sin, zero], axis=1), (1, 2))
    sin_dn = jnp.tile(jnp.concatenate([zero, sin], axis=1), (1, 2))
    qg = jnp.tile(q_gain.astype(F32), 2).reshape(1, LANE)
    kg = jnp.tile(k_gain.astype(F32), 2).reshape(1, LANE)

    kcb, vcb = Z_KA // LANE, Z_VA // LANE
    prev_blk = lambda s: jnp.maximum(s * nchunk - 1, 0)
    next_blk = lambda s: jnp.minimum((s + 1) * nchunk, nblk - 1)
    main = lambda cb: pl.BlockSpec((ts, LANE), lambda b, s: (b * nstep + s, cb))
    prev = lambda cb: pl.BlockSpec((CHUNK, LANE), lambda b, s: (b * nblk + prev_blk(s), cb))
    nxt = lambda cb: pl.BlockSpec((CHUNK, LANE), lambda b, s: (b * nblk + next_blk(s), cb))
    tmain = pl.BlockSpec((ts, LANE), lambda b, s: (s, 0))
    tprev = pl.BlockSpec((CHUNK, LANE), lambda b, s: (prev_blk(s), 0))
    tnext = pl.BlockSpec((CHUNK, LANE), lambda b, s: (next_blk(s), 0))
    one = pl.BlockSpec((1, LANE), lambda b, s: (0, 0))
    return pl.pallas_call(
        functools.partial(_attn_body, nchunk=nchunk, nstep=nstep),
        grid=(batch, nstep),
        in_specs=[pl.BlockSpec(memory_space=pltpu.SMEM),
                  pl.BlockSpec((ts, ATT_HEADS * ATT_DH), lambda b, s: (b * nstep + s, Z_QA // 512)),
                  prev(kcb), main(kcb), nxt(kcb), prev(vcb), main(vcb), nxt(vcb),
                  tmain, tmain, tmain, tprev, tprev, tprev, tnext, tnext, tnext, one, one],
        out_specs=pl.BlockSpec((ts, ATT_HEADS * ATT_DH), lambda b, s: (b * nstep + s, 0)),
        out_shape=jax.ShapeDtypeStruct((t, ATT_HEADS * ATT_DH), BF16),
        compiler_params=_params("parallel", "arbitrary"),
        name="window_attention",
    )(sink.astype(F32), z, z, z, z, z, z, z,
      cos2, sin_up, sin_dn, cos2, sin_up, sin_dn, cos2, sin_up, sin_dn, qg, kg)


def _merge_body(h_ref, gr_ref, gs_ref, ga_ref, yr_ref, ys_ref, ya_ref, wr_ref, ws_ref, wa_ref, wo_ref, o_ref):
    m = _sigmoid(gr_ref[...]) * jnp.dot(yr_ref[...], wr_ref[...], preferred_element_type=F32)
    m += _sigmoid(gs_ref[...]) * jnp.dot(ys_ref[...], ws_ref[...], preferred_element_type=F32)
    m += _sigmoid(ga_ref[...]) * jnp.dot(ya_ref[...], wa_ref[...], preferred_element_type=F32)
    o_ref[...] = h_ref[...] + jnp.dot(m.astype(BF16), wo_ref[...], preferred_element_type=F32)


def _merge(h, z, y_r, y_s, y_a, w_r, w_s, w_a, w_o, tm):
    t, d = h.shape
    row = lambda w: pl.BlockSpec((tm, w), lambda i: (i, 0))
    gate = lambda col0: pl.BlockSpec((tm, d), lambda i: (i, col0 // d))
    full = lambda a: pl.BlockSpec(a.shape, lambda i: (0, 0))
    return pl.pallas_call(
        _merge_body,
        grid=(t // tm,),
        in_specs=[row(d), gate(Z_GATE_R), gate(Z_GATE_S), gate(Z_GATE_A), row(512), row(512), row(512),
                  full(w_r), full(w_s), full(w_a), full(w_o)],
        out_specs=row(d),
        out_shape=jax.ShapeDtypeStruct((t, d), F32),
        compiler_params=_params("parallel"),
        name="branch_merge",
    )(h, z, z, z, y_r, y_s, y_a, w_r, w_s, w_a, w_o)


def _top16(vals, rows):
    n = vals.shape[0]
    out_v, out_i = [], []
    for _ in range(PEER_TOPK):
        m = jnp.max(vals, axis=0, keepdims=True)
        idx = jnp.min(jnp.where(vals == m, rows, n), axis=0, keepdims=True)
        out_v.append(m)
        out_i.append(idx)
        vals = jnp.where(rows == idx, -jnp.inf, vals)
    return out_v, out_i


def _route_body(h_ref, g_ref, wq_ref, keys_ref, hn_ref, eid_ref, gate_ref):
    hn = _rms(h_ref[...], g_ref[...])
    hn_ref[...] = hn
    q = jnp.dot(hn.astype(BF16), wq_ref[...], preferred_element_type=F32).astype(BF16)
    tm = q.shape[0]
    rows = lax.broadcasted_iota(jnp.int32, (PEER_KEYS, tm), 0)
    rows2 = lax.broadcasted_iota(jnp.int32, (PEER_TOPK * PEER_TOPK, tm), 0)
    half = PEER_QDIM // 2
    eids, gates = [], []
    for hd in range(PEER_HEADS):
        sub = []
        for p in range(2):
            qs = q[:, (2 * hd + p) * half:(2 * hd + p + 1) * half]
            s = lax.dot_general(keys_ref[hd, p], qs, (((1,), (1,)), ((), ())), preferred_element_type=F32)
            sub.append(_top16(s, rows))
        (s1, i1), (s2, i2) = sub
        s2c = jnp.concatenate(s2, axis=0)
        i2c = jnp.concatenate(i2, axis=0)
        cand_s = jnp.concatenate([a + s2c for a in s1], axis=0)
        cand_e = jnp.concatenate([a * PEER_KEYS + i2c for a in i1], axis=0)
        top_s, sel = _top16(cand_s, rows2)
        top_e = [jnp.sum(jnp.where(rows2 == i, cand_e, 0), axis=0, keepdims=True) for i in sel]
        ts_ = jnp.concatenate(top_s, axis=0)
        e = jnp.exp(ts_ - top_s[0])
        gates.append(e / jnp.sum(e, axis=0, keepdims=True))
        eids.append(jnp.concatenate(top_e, axis=0))
    eid_ref[...] = jnp.concatenate(eids, axis=0).T
    gate_ref[...] = jnp.concatenate(gates, axis=0).T


def _route(h, gain, wq_bf, keys_bf, tm):
    t, d = h.shape
    return pl.pallas_call(
        _route_body,
        grid=(t // tm,),
        in_specs=[pl.BlockSpec((tm, d), lambda i: (i, 0)),
                  pl.BlockSpec((1, d), lambda i: (0, 0)),
                  pl.BlockSpec(wq_bf.shape, lambda i: (0, 0)),
                  pl.BlockSpec(keys_bf.shape, lambda i: (0, 0, 0, 0))],
        out_specs=[pl.BlockSpec((tm, d), lambda i: (i, 0)),
                   pl.BlockSpec((tm, PEER_PICKS), lambda i: (i, 0)),
                   pl.BlockSpec((tm, PEER_PICKS), lambda i: (i, 0))],
        out_shape=[jax.ShapeDtypeStruct((t, d), F32),
                   jax.ShapeDtypeStruct((t, PEER_PICKS), jnp.int32),
                   jax.ShapeDtypeStruct((t, PEER_PICKS), F32)],
        compiler_params=_params("parallel"),
        name="peer_route",
    )(h, gain.reshape(1, d), wq_bf, keys_bf)


def _sc_mesh():
    return plsc.VectorSubcoreMesh(core_axis_name="core", subcore_axis_name="subcore")


def _sc_worker():
    return lax.axis_index("subcore") * SC_CORES + lax.axis_index("core")


def _sc_row_pipeline(tab_hbm, idx_v, buf, sems, compute):
    nsub = PEER_PICKS // SC_ROWS
    ng = SC_TOKENS * nsub

    def gather(g, slot):
        rows = idx_v.at[g // nsub, pl.ds((g % nsub) * SC_ROWS, SC_ROWS)]
        return pltpu.make_async_copy(tab_hbm.at[rows], buf.at[slot], sems.at[slot])

    gather(0, 0).start()

    @pl.loop(0, ng, step=2)
    def _(g):
        gather(g + 1, 1).start()
        gather(g, 0).wait()
        compute(g, 0)

        @pl.when(g + 2 < ng)
        def _():
            gather(g + 2, 0).start()

        gather(g + 1, 1).wait()
        compute(g + 1, 1)


def _expert_scores(table, eid, hn):
    t, d = hn.shape
    per_worker = t // SC_WORKERS
    nsub = PEER_PICKS // SC_ROWS
    ln = SC_LANES

    @functools.partial(
        pl.kernel, mesh=_sc_mesh(),
        out_type=jax.ShapeDtypeStruct((t, PEER_PICKS), F32),
        scratch_types=[pltpu.VMEM((SC_TOKENS, PEER_PICKS), jnp.int32),
                       pltpu.VMEM((SC_TOKENS, d), F32),
                       pltpu.VMEM((SC_TOKENS, PEER_PICKS), F32),
                       pltpu.VMEM((2, SC_ROWS, d), F32),
                       pltpu.SemaphoreType.DMA((2,))],
        compiler_params=pltpu.CompilerParams(needs_layout_passes=False),
        name="peer_expert_scores")
    def run(tab_hbm, eid_hbm, hn_hbm, out_hbm, idx_v, x_v, a_v, buf, sems):
        lane = lax.iota(jnp.int32, ln)

        def compute(g, slot):
            tok = g // nsub
            sub = g % nsub
            for grp in range(SC_ROWS // ln):
                def body(c, accs):
                    xc = x_v[tok, pl.ds(c * ln, ln)]
                    return tuple(accs[r] + buf[slot, grp * ln + r, pl.ds(c * ln, ln)] * xc for r in range(ln))

                accs = lax.fori_loop(0, d // ln, body, tuple(jnp.zeros((ln,), F32) for _ in range(ln)))
                res = jnp.zeros((ln,), F32)
                for r in range(ln):
                    res = jnp.where(lane == r, jnp.sum(accs[r]), res)
                a_v[tok, pl.ds(sub * SC_ROWS + grp * ln, ln)] = res

        @pl.loop(0, per_worker // SC_TOKENS)
        def _(blk):
            tok0 = _sc_worker() * per_worker + blk * SC_TOKENS
            pltpu.sync_copy(eid_hbm.at[pl.ds(tok0, SC_TOKENS)], idx_v)
            pltpu.sync_copy(hn_hbm.at[pl.ds(tok0, SC_TOKENS)], x_v)
            _sc_row_pipeline(tab_hbm, idx_v, buf, sems, compute)
            pltpu.sync_copy(a_v, out_hbm.at[pl.ds(tok0, SC_TOKENS)])

    return run(table, eid, hn)


def _expert_mix(table, eid, w):
    t = eid.shape[0]
    d = table.shape[1]
    per_worker = t // SC_WORKERS
    nsub = PEER_PICKS // SC_ROWS
    ln = SC_LANES
    nacc = SC_COLS // ln

    @functools.partial(
        pl.kernel, mesh=_sc_mesh(),
        out_type=jax.ShapeDtypeStruct((t, d), F32),
        scratch_types=[pltpu.VMEM((SC_TOKENS, PEER_PICKS), jnp.int32),
                       pltpu.VMEM((SC_TOKENS, PEER_PICKS), F32),
                       pltpu.VMEM((SC_TOKENS, d), F32),
                       pltpu.VMEM((2, SC_ROWS, d), F32),
                       pltpu.SemaphoreType.DMA((2,))],
        compiler_params=pltpu.CompilerParams(needs_layout_passes=False),
        name="peer_expert_mix")
    def run(tab_hbm, eid_hbm, w_hbm, out_hbm, idx_v, w_v, y_v, buf, sems):
        zero = jnp.zeros((ln,), F32)

        def compute(g, slot):
            tok = g // nsub
            sub = g % nsub
            tokv = jnp.full((ln,), tok, jnp.int32)
            for cc in range(d // SC_COLS):
                def body(r, accs):
                    wr = plsc.load_gather(w_v, [tokv, jnp.full((ln,), sub * SC_ROWS + r, jnp.int32)])
                    return tuple(accs[k] + wr * buf[slot, r, pl.ds(cc * SC_COLS + k * ln, ln)] for k in range(nacc))

                init = tuple(y_v[tok, pl.ds(cc * SC_COLS + k * ln, ln)] for k in range(nacc))
                accs = lax.fori_loop(0, SC_ROWS, body, init)
                for k in range(nacc):
                    y_v[tok, pl.ds(cc * SC_COLS + k * ln, ln)] = accs[k]

        @pl.loop(0, per_worker // SC_TOKENS)
        def _(blk):
            tok0 = _sc_worker() * per_worker + blk * SC_TOKENS
            pltpu.sync_copy(eid_hbm.at[pl.ds(tok0, SC_TOKENS)], idx_v)
            pltpu.sync_copy(w_hbm.at[pl.ds(tok0, SC_TOKENS)], w_v)

            @pl.loop(0, SC_TOKENS)
            def _(tok):
                @pl.loop(0, d // ln)
                def _(c):
                    y_v[tok, pl.ds(c * ln, ln)] = zero

            _sc_row_pipeline(tab_hbm, idx_v, buf, sems, compute)
            pltpu.sync_copy(y_v, out_hbm.at[pl.ds(tok0, SC_TOKENS)])

    return run(table, eid, w)


def _pick_weights_body(a_ref, g_ref, o_ref):
    o_ref[...] = g_ref[...] * _gelu(a_ref[...])


def _pick_weights(a, gate, tm):
    t = a.shape[0]
    spec = pl.BlockSpec((tm, PEER_PICKS), lambda i: (i, 0))
    return pl.pallas_call(
        _pick_weights_body,
        grid=(t // tm,),
        in_specs=[spec, spec],
        out_specs=spec,
        out_shape=jax.ShapeDtypeStruct(a.shape, F32),
        compiler_params=_params("parallel"),
        name="peer_pick_weights",
    )(a, gate)


def _ple_body(h_ref, y_ref, g_ref, wg_ref, p_ref, wp_ref, o_ref):
    h = h_ref[...] + y_ref[...]
    hn = _rms(h, g_ref[...]).astype(BF16)
    gate = _sigmoid(jnp.dot(hn, wg_ref[...], preferred_element_type=F32))
    emb = jnp.dot(p_ref[...].astype(BF16), wp_ref[...], preferred_element_type=F32)
    o_ref[...] = h + gate * emb


def _ple(h, y, gain, wg_bf, p, wp_bf, tm):
    t, d = h.shape
    return pl.pallas_call(
        _ple_body,
        grid=(t // tm,),
        in_specs=[pl.BlockSpec((tm, d), lambda i: (i, 0)),
                  pl.BlockSpec((tm, d), lambda i: (i, 0)),
                  pl.BlockSpec((1, d), lambda i: (0, 0)),
                  pl.BlockSpec(wg_bf.shape, lambda i: (0, 0)),
                  pl.BlockSpec((tm, p.shape[1]), lambda i: (i, 0)),
                  pl.BlockSpec(wp_bf.shape, lambda i: (0, 0))],
        out_specs=pl.BlockSpec((tm, d), lambda i: (i, 0)),
        out_shape=jax.ShapeDtypeStruct((t, d), F32),
        compiler_params=_params("parallel"),
        name="layer_embedding",
    )(h, y, gain.reshape(1, d), wg_bf, p, wp_bf)


def _rope_tables(seq, dim):
    inv = 1.0 / (ROPE_THETA ** (jnp.arange(0, dim, 2, dtype=F32) / dim))
    ang = jnp.arange(seq, dtype=F32)[:, None] * inv[None, :]
    return jnp.cos(ang), jnp.sin(ang)


def _permute_in_columns(w_in):
    return jnp.concatenate([w_in[:, 3840:], w_in[:, :3840]], axis=1)


def kernel(x, p, norm_mix, w_in, ret_decay, ret_norm, sgu_ln_g, sgu_ln_b, sgu_w, sgu_b, att_q_norm, att_k_norm, att_sink, w_proj_ret, w_proj_sgu, w_proj_att, w_out, norm_ffn, peer_wq, peer_keys, peer_u, peer_v, norm_ple, ple_gate, ple_proj):
    batch, seq, d = x.shape
    depth = w_in.shape[0]
    t = batch * seq
    ts = min(512, seq)
    tm = min(512, t)
    cos_r, sin_r = _rope_tables(seq, RET_DK)
    cos_r2 = jnp.concatenate([cos_r, cos_r], axis=1)
    sin_r2 = jnp.concatenate([-sin_r, sin_r], axis=1)
    cos_a, sin_a = _rope_tables(seq, ATT_DH)
    h = x.reshape(t, d)
    for i in range(depth):
        z = _norm_matmul(h, norm_mix[i], _permute_in_columns(w_in[i]).astype(BF16), tm, 1152)
        y_r = _retention(z, ret_decay[i], ret_norm[i], cos_r2, sin_r2, batch, seq, ts)
        y_s = _sgu(z, sgu_ln_g[i], sgu_ln_b[i], sgu_w[i], sgu_b[i], ts)
        y_a = _attention(z, att_q_norm[i], att_k_norm[i], att_sink[i], cos_a, sin_a, batch, seq, ts)
        h = _merge(h, z, y_r, y_s, y_a, w_proj_ret[i].astype(BF16), w_proj_sgu[i].astype(BF16),
                   w_proj_att[i].astype(BF16), w_out[i].astype(BF16), tm)
        hn, eid, gate = _route(h, norm_ffn[i], peer_wq[i].astype(BF16), peer_keys[i].astype(BF16), min(256, t))
        a = _expert_scores(peer_u[i], eid, hn)
        w = _pick_weights(a, gate, tm)
        y = _expert_mix(peer_v[i], eid, w)
        h = _ple(h, y, norm_ple[i], ple_gate[i].astype(BF16), p[i].reshape(t, -1), ple_proj[i].astype(BF16), tm)
    return h.reshape(batch, seq, d)
```

```python
import functools
import math

import jax
import jax.numpy as jnp
from jax import lax
from jax.experimental import pallas as pl
from jax.experimental.pallas import tpu as pltpu
from jax.experimental.pallas import tpu_sc as plsc

F32 = jnp.float32
BF16 = jnp.bfloat16

D_MODEL = 1024
PLE_DIM = 256
CHUNK = 128
EPS = 1e-6
ROPE_THETA = 10000.0
RET_HEADS = 4
RET_DK = 128
SGU_GROUPS = 4
SGU_WIDTH = 512
ATT_HEADS = 8
ATT_KV_HEADS = 2
ATT_DH = 64
ATT_GROUP = ATT_HEADS // ATT_KV_HEADS
NEG_INF = -1e30
PEER_HEADS = 8
PEER_KEYS = 128
PEER_QDIM = 256
PEER_TOPK = 16
PEER_PICKS = PEER_HEADS * PEER_TOPK

Z_GATE_R, Z_GATE_S, Z_GATE_A = 0, 1024, 2048
Z_QR, Z_KR, Z_VR, Z_GR = 3072, 3584, 4096, 4608
Z_US, Z_VS = 5120, 5632
Z_QA, Z_KA, Z_VA = 6144, 6656, 6784
D_IN = 6912

LANE = 128
SC_CORES = 2
SC_SUBCORES = 16
SC_LANES = 16
SC_WORKERS = SC_CORES * SC_SUBCORES
SC_TOKENS = 16
SC_ROWS = 32
SC_COLS = 256
VMEM_LIMIT = 56 * 1024 * 1024
PIPELINE_GROUPS = 4


def _params(*sem):
    return pltpu.CompilerParams(dimension_semantics=sem, vmem_limit_bytes=VMEM_LIMIT)


def _gelu(x):
    return 0.5 * x * (1.0 + lax.erf(x * (1.0 / math.sqrt(2.0))))


def _sigmoid(x):
    return 1.0 / (1.0 + jnp.exp(-x))


def _rms(x, g):
    return x * lax.rsqrt(jnp.mean(x * x, axis=-1, keepdims=True) + EPS) * g


def _inproj_body(x_ref, g_ref, w_ref, o_ref, xn_ref):
    @pl.when(pl.program_id(1) == 0)
    def _():
        xn_ref[...] = _rms(x_ref[...], g_ref[...]).astype(BF16)

    o_ref[...] = jnp.dot(xn_ref[...], w_ref[...], preferred_element_type=F32)


def _norm_matmul(h, gain, w_bf, tm, tn):
    t, d = h.shape
    n = w_bf.shape[1]
    return pl.pallas_call(
        _inproj_body,
        grid=(t // tm, n // tn),
        in_specs=[pl.BlockSpec((tm, d), lambda i, j: (i, 0)),
                  pl.BlockSpec((1, d), lambda i, j: (0, 0)),
                  pl.BlockSpec((d, tn), lambda i, j: (0, j))],
        out_specs=pl.BlockSpec((tm, tn), lambda i, j: (i, j)),
        out_shape=jax.ShapeDtypeStruct((t, n), F32),
        scratch_shapes=[pltpu.VMEM((tm, d), BF16)],
        compiler_params=_params("parallel", "arbitrary"),
        name="norm_matmul",
    )(h, gain.reshape(1, d), w_bf)


def _rope128(x, cos, sin_signed):
    return x * cos + pltpu.roll(x, 64, 1) * sin_signed


def _ret_bwd_body(q_ref, k_ref, v_ref, cos_ref, sin_ref, qw_ref, kw_ref, cd_ref, o_ref, st_ref, *, nchunk):
    @pl.when(pl.program_id(2) == 0)
    def _():
        st_ref[...] = jnp.zeros_like(st_ref)

    qw = qw_ref[0]
    kw = kw_ref[0]
    cd = cd_ref[0, 0:1, :]
    for c in reversed(range(nchunk)):
        rows = pl.ds(c * CHUNK, CHUNK)
        cos = cos_ref[rows, :]
        sin = sin_ref[rows, :]
        q = _rope128(q_ref[rows, :], cos, sin)
        k = _rope128(k_ref[rows, :], cos, sin) * (RET_DK ** -0.5)
        v = v_ref[rows, :]
        st = st_ref[...]
        o_ref[rows, :] = jnp.dot((q * qw).astype(BF16), st.astype(BF16), preferred_element_type=F32)
        kv = jnp.dot((k * kw).T.astype(BF16), v.astype(BF16), preferred_element_type=F32)
        st_ref[...] = st * cd + kv


def _ret_fwd_body(q_ref, k_ref, v_ref, g_ref, yb_ref, cos_ref, sin_ref, dm_ref, qw_ref, kw_ref, cd_ref,
                  gn_ref, o_ref, st_ref, *, nchunk):
    @pl.when(pl.program_id(2) == 0)
    def _():
        st_ref[...] = jnp.zeros_like(st_ref)

    qw = qw_ref[0]
    kw = kw_ref[0]
    cd = cd_ref[0, 0:1, :]
    dm = dm_ref[0]
    gn = gn_ref[0, 0:1, :]
    for c in range(nchunk):
        rows = pl.ds(c * CHUNK, CHUNK)
        cos = cos_ref[rows, :]
        sin = sin_ref[rows, :]
        q = _rope128(q_ref[rows, :], cos, sin)
        k = _rope128(k_ref[rows, :], cos, sin) * (RET_DK ** -0.5)
        v = v_ref[rows, :].astype(BF16)
        st = st_ref[...]
        s = lax.dot_general(q.astype(BF16), k.astype(BF16), (((1,), (1,)), ((), ())),
                            preferred_element_type=F32) * dm
        y = jnp.dot(s.astype(BF16), v, preferred_element_type=F32)
        y += jnp.dot((q * qw).astype(BF16), st.astype(BF16), preferred_element_type=F32)
        y += yb_ref[rows, :]
        kv = jnp.dot((k * kw).T.astype(BF16), v, preferred_element_type=F32)
        st_ref[...] = st * cd + kv
        y = y * lax.rsqrt(jnp.mean(y * y, axis=-1, keepdims=True) + EPS) * gn
        g = g_ref[rows, :]
        o_ref[rows, :] = (g * _sigmoid(g) * y).astype(o_ref.dtype)


def _retention(z, ret_decay, ret_norm, cos, sin, batch, seq, ts):
    t = z.shape[0]
    nchunk = ts // CHUNK
    nstep = seq // ts
    hd = RET_HEADS
    log_g = jax.nn.log_sigmoid(ret_decay.astype(F32))
    idx = jnp.arange(CHUNK, dtype=F32)
    diff = idx[:, None] - idx[None, :]
    lf = log_g[0][:, None, None]
    lb = log_g[1][:, None, None]
    dmat = jnp.where(diff[None] >= 0, jnp.exp(lf * jnp.maximum(diff, 0.0)[None]),
                     jnp.exp(lb * jnp.maximum(-diff, 0.0)[None]))
    bc = lambda a: jnp.broadcast_to(a[:, :, None], (hd, a.shape[1], LANE))
    qw_f = bc(jnp.exp(log_g[0][:, None] * (idx + 1.0)[None, :]))
    kw_f = bc(jnp.exp(log_g[0][:, None] * (CHUNK - 1 - idx)[None, :]))
    qw_b = bc(jnp.exp(log_g[1][:, None] * (CHUNK - idx)[None, :]))
    kw_b = bc(jnp.exp(log_g[1][:, None] * idx[None, :]))
    cd_f = jnp.broadcast_to(jnp.exp(log_g[0] * CHUNK)[:, None, None], (hd, 8, LANE))
    cd_b = jnp.broadcast_to(jnp.exp(log_g[1] * CHUNK)[:, None, None], (hd, 8, LANE))
    gn = jnp.broadcast_to(ret_norm.astype(F32).reshape(hd, 1, LANE), (hd, 8, LANE))

    def zspec(col0, rev):
        cb = col0 // LANE
        if rev:
            return pl.BlockSpec((ts, LANE), lambda b, h, s: (b * nstep + nstep - 1 - s, cb + h))
        return pl.BlockSpec((ts, LANE), lambda b, h, s: (b * nstep + s, cb + h))

    def tspec(rev):
        if rev:
            return pl.BlockSpec((ts, LANE), lambda b, h, s: (nstep - 1 - s, 0))
        return pl.BlockSpec((ts, LANE), lambda b, h, s: (s, 0))

    hspec = lambda r: pl.BlockSpec((1, r, LANE), lambda b, h, s: (h, 0, 0))

    yb = pl.pallas_call(
        functools.partial(_ret_bwd_body, nchunk=nchunk),
        grid=(batch, hd, nstep),
        in_specs=[zspec(Z_QR, True), zspec(Z_KR, True), zspec(Z_VR, True), tspec(True), tspec(True),
                  hspec(CHUNK), hspec(CHUNK), hspec(8)],
        out_specs=pl.BlockSpec((ts, LANE), lambda b, h, s: (b * nstep + nstep - 1 - s, h)),
        out_shape=jax.ShapeDtypeStruct((t, hd * LANE), F32),
        scratch_shapes=[pltpu.VMEM((RET_DK, LANE), F32)],
        compiler_params=_params("parallel", "parallel", "arbitrary"),
        name="retention_bwd",
    )(z, z, z, cos, sin, qw_b, kw_b, cd_b)

    return pl.pallas_call(
        functools.partial(_ret_fwd_body, nchunk=nchunk),
        grid=(batch, hd, nstep),
        in_specs=[zspec(Z_QR, False), zspec(Z_KR, False), zspec(Z_VR, False), zspec(Z_GR, False),
                  pl.BlockSpec((ts, LANE), lambda b, h, s: (b * nstep + s, h)),
                  tspec(False), tspec(False), hspec(CHUNK), hspec(CHUNK), hspec(CHUNK), hspec(8), hspec(8)],
        out_specs=pl.BlockSpec((ts, LANE), lambda b, h, s: (b * nstep + s, h)),
        out_shape=jax.ShapeDtypeStruct((t, hd * LANE), BF16),
        scratch_shapes=[pltpu.VMEM((RET_DK, LANE), F32)],
        compiler_params=_params("parallel", "parallel", "arbitrary"),
        name="retention_fwd",
    )(z, z, z, z, yb, cos, sin, dmat, qw_f, kw_f, cd_f, gn)


def _sgu_body(u_ref, v_ref, lg_ref, lb_ref, w_ref, b_ref, o_ref, *, nchunk):
    lg = lg_ref[...]
    lb = lb_ref[...]
    for c in range(nchunk):
        rows = pl.ds(c * CHUNK, CHUNK)
        vf = _gelu(v_ref[rows, :])
        mu = jnp.mean(vf, axis=-1, keepdims=True)
        vc = vf - mu
        var = jnp.mean(vc * vc, axis=-1, keepdims=True)
        vn = (vc * lax.rsqrt(var + EPS) * lg + lb).astype(BF16)
        for g in range(SGU_GROUPS):
            cols = slice(g * LANE, (g + 1) * LANE)
            mixed = jnp.dot(w_ref[g], vn[:, cols], preferred_element_type=F32) + b_ref[g]
            uf = _gelu(u_ref[rows, cols])
            o_ref[rows, cols] = (uf * mixed).astype(o_ref.dtype)


def _sgu(z, ln_g, ln_b, w_s, b_s, ts):
    t = z.shape[0]
    bias = jnp.broadcast_to(b_s.astype(F32)[:, :, None], (SGU_GROUPS, CHUNK, LANE))
    return pl.pallas_call(
        functools.partial(_sgu_body, nchunk=ts // CHUNK),
        grid=(t // ts,),
        in_specs=[pl.BlockSpec((ts, SGU_WIDTH), lambda i: (i, Z_US // SGU_WIDTH)),
                  pl.BlockSpec((ts, SGU_WIDTH), lambda i: (i, Z_VS // SGU_WIDTH)),
                  pl.BlockSpec((1, SGU_WIDTH), lambda i: (0, 0)),
                  pl.BlockSpec((1, SGU_WIDTH), lambda i: (0, 0)),
                  pl.BlockSpec((SGU_GROUPS, CHUNK, CHUNK), lambda i: (0, 0, 0)),
                  pl.BlockSpec((SGU_GROUPS, CHUNK, LANE), lambda i: (0, 0, 0))],
        out_specs=pl.BlockSpec((ts, SGU_WIDTH), lambda i: (i, 0)),
        out_shape=jax.ShapeDtypeStruct((t, SGU_WIDTH), BF16),
        compiler_params=_params("parallel"),
        name="spatial_gating",
    )(z, z, ln_g.reshape(1, -1).astype(F32), ln_b.reshape(1, -1).astype(F32), w_s.astype(BF16), bias)


def _pair_norm_rope(x, gain, cos, sin_up, sin_dn, low):
    sq = x * x
    lo = jnp.sum(jnp.where(low, sq, 0.0), axis=-1, keepdims=True)
    hi = jnp.sum(sq, axis=-1, keepdims=True) - lo
    ms = jnp.where(low, lo, hi) * (1.0 / ATT_DH)
    xn = x * lax.rsqrt(ms + EPS) * gain
    return xn * cos + pltpu.roll(xn, LANE - 32, 1) * sin_up + pltpu.roll(xn, 32, 1) * sin_dn


def _attn_body(sink_ref, q_ref, kp_ref, k_ref, kn_ref, vp_ref, v_ref, vn_ref,
               cq_ref, suq_ref, sdq_ref, ckp_ref, sukp_ref, sdkp_ref, ckn_ref, sukn_ref, sdkn_ref,
               qg_ref, kg_ref, o_ref, *, nchunk, nstep):
    s_id = pl.program_id(1)
    ts = nchunk * CHUNK
    lane = lax.broadcasted_iota(jnp.int32, (1, LANE), 1)
    low = lane < ATT_DH
    kg = kg_ref[...]
    qg = qg_ref[...]
    k_ext = jnp.concatenate([
        _pair_norm_rope(kp_ref[...], kg, ckp_ref[...], sukp_ref[...], sdkp_ref[...], low),
        _pair_norm_rope(k_ref[...], kg, cq_ref[...], suq_ref[...], sdq_ref[...], low),
        _pair_norm_rope(kn_ref[...], kg, ckn_ref[...], sukn_ref[...], sdkn_ref[...], low)], axis=0)
    v_ext = jnp.concatenate([vp_ref[...], v_ref[...], vn_ref[...]], axis=0).astype(BF16)
    k_lo = jnp.where(low, k_ext, 0.0).astype(BF16)
    k_hi = jnp.where(low, 0.0, k_ext).astype(BF16)
    qi = lax.broadcasted_iota(jnp.int32, (CHUNK, 3 * CHUNK), 0)
    kj = lax.broadcasted_iota(jnp.int32, (CHUNK, 3 * CHUNK), 1)
    band = jnp.abs(qi + CHUNK - kj) <= CHUNK
    for c in range(nchunk):
        rows = pl.ds(c * CHUNK, CHUNK)
        first = jnp.logical_and(s_id == 0, c == 0)
        last = jnp.logical_and(s_id == nstep - 1, c == nchunk - 1)
        valid = band
        if c == 0:
            valid = jnp.logical_and(valid, jnp.logical_or(kj >= CHUNK, jnp.logical_not(first)))
        if c == nchunk - 1:
            valid = jnp.logical_and(valid, jnp.logical_or(kj < 2 * CHUNK, jnp.logical_not(last)))
        kc_lo = k_lo[c * CHUNK:(c + 3) * CHUNK]
        kc_hi = k_hi[c * CHUNK:(c + 3) * CHUNK]
        vc = v_ext[c * CHUNK:(c + 3) * CHUNK]
        cos = cq_ref[rows, :]
        su = suq_ref[rows, :]
        sd = sdq_ref[rows, :]
        for pair in range(ATT_HEADS // 2):
            cols = slice(pair * LANE, (pair + 1) * LANE)
            qp = _pair_norm_rope(q_ref[rows, cols], qg, cos, su, sd, low) * (ATT_DH ** -0.5)
            kv_head = (2 * pair) // ATT_GROUP
            outs = []
            for half in range(2):
                head = 2 * pair + half
                qh = qp if half == kv_head else pltpu.roll(qp, ATT_DH, 1)
                if kv_head == 0:
                    qh = jnp.where(low, qh, 0.0)
                    kc = kc_lo
                else:
                    qh = jnp.where(low, 0.0, qh)
                    kc = kc_hi
                s = lax.dot_general(qh.astype(BF16), kc, (((1,), (1,)), ((), ())), preferred_element_type=F32)
                s = jnp.where(valid, s, NEG_INF)
                sk = sink_ref[head]
                m = jnp.maximum(jnp.max(s, axis=-1, keepdims=True), sk)
                e = jnp.exp(s - m)
                den = jnp.sum(e, axis=-1, keepdims=True) + jnp.exp(sk - m)
                o = jnp.dot(e.astype(BF16), vc, preferred_element_type=F32) / den
                outs.append(o if half == kv_head else pltpu.roll(o, ATT_DH, 1))
            o_ref[rows, cols] = jnp.where(low, outs[0], outs[1]).astype(o_ref.dtype)


def _attention(z, q_gain, k_gain, sink, cos, sin, batch, seq, ts):
    t = z.shape[0]
    nchunk = ts // CHUNK
    nstep = seq // ts
    nblk = seq // CHUNK
    cos2 = jnp.tile(jnp.concatenate([cos, cos], axis=1), (1, 2))
    zero = jnp.zeros_like(sin)
    sin_up = jnp.tile(jnp.concatenate([-sin, zero], axis=1), (1, 2))
    sin_dn = jnp.tile(jnp.concatenate([zero, sin], axis=1), (1, 2))
    qg = jnp.tile(q_gain.astype(F32), 2).reshape(1, LANE)
    kg = jnp.tile(k_gain.astype(F32), 2).reshape(1, LANE)

    kcb, vcb = Z_KA // LANE, Z_VA // LANE
    prev_blk = lambda s: jnp.maximum(s * nchunk - 1, 0)
    next_blk = lambda s: jnp.minimum((s + 1) * nchunk, nblk - 1)
    main = lambda cb: pl.BlockSpec((ts, LANE), lambda b, s: (b * nstep + s, cb))
    prev = lambda cb: pl.BlockSpec((CHUNK, LANE), lambda b, s: (b * nblk + prev_blk(s), cb))
    nxt = lambda cb: pl.BlockSpec((CHUNK, LANE), lambda b, s: (b * nblk + next_blk(s), cb))
    tmain = pl.BlockSpec((ts, LANE), lambda b, s: (s, 0))
    tprev = pl.BlockSpec((CHUNK, LANE), lambda b, s: (prev_blk(s), 0))
    tnext = pl.BlockSpec((CHUNK, LANE), lambda b, s: (next_blk(s), 0))
    one = pl.BlockSpec((1, LANE), lambda b, s: (0, 0))
    return pl.pallas_call(
        functools.partial(_attn_body, nchunk=nchunk, nstep=nstep),
        grid=(batch, nstep),
        in_specs=[pl.BlockSpec(memory_space=pltpu.SMEM),
                  pl.BlockSpec((ts, ATT_HEADS * ATT_DH), lambda b, s: (b * nstep + s, Z_QA // 512)),
                  prev(kcb), main(kcb), nxt(kcb), prev(vcb), main(vcb), nxt(vcb),
                  tmain, tmain, tmain, tprev, tprev, tprev, tnext, tnext, tnext, one, one],
        out_specs=pl.BlockSpec((ts, ATT_HEADS * ATT_DH), lambda b, s: (b * nstep + s, 0)),
        out_shape=jax.ShapeDtypeStruct((t, ATT_HEADS * ATT_DH), BF16),
        compiler_params=_params("parallel", "arbitrary"),
        name="window_attention",
    )(sink.astype(F32), z, z, z, z, z, z, z,
      cos2, sin_up, sin_dn, cos2, sin_up, sin_dn, cos2, sin_up, sin_dn, qg, kg)


def _merge_body(h_ref, gr_ref, gs_ref, ga_ref, yr_ref, ys_ref, ya_ref, wr_ref, ws_ref, wa_ref, wo_ref, o_ref):
    m = _sigmoid(gr_ref[...]) * jnp.dot(yr_ref[...], wr_ref[...], preferred_element_type=F32)
    m += _sigmoid(gs_ref[...]) * jnp.dot(ys_ref[...], ws_ref[...], preferred_element_type=F32)
    m += _sigmoid(ga_ref[...]) * jnp.dot(ya_ref[...], wa_ref[...], preferred_element_type=F32)
    o_ref[...] = h_ref[...] + jnp.dot(m.astype(BF16), wo_ref[...], preferred_element_type=F32)


def _merge(h, z, y_r, y_s, y_a, w_r, w_s, w_a, w_o, tm):
    t, d = h.shape
    row = lambda w: pl.BlockSpec((tm, w), lambda i: (i, 0))
    gate = lambda col0: pl.BlockSpec((tm, d), lambda i: (i, col0 // d))
    full = lambda a: pl.BlockSpec(a.shape, lambda i: (0, 0))
    return pl.pallas_call(
        _merge_body,
        grid=(t // tm,),
        in_specs=[row(d), gate(Z_GATE_R), gate(Z_GATE_S), gate(Z_GATE_A), row(512), row(512), row(512),
                  full(w_r), full(w_s), full(w_a), full(w_o)],
        out_specs=row(d),
        out_shape=jax.ShapeDtypeStruct((t, d), F32),
        compiler_params=_params("parallel"),
        name="branch_merge",
    )(h, z, z, z, y_r, y_s, y_a, w_r, w_s, w_a, w_o)


def _top16(vals, rows):
    n = vals.shape[0]
    out_v, out_i = [], []
    for _ in range(PEER_TOPK):
        m = jnp.max(vals, axis=0, keepdims=True)
        idx = jnp.min(jnp.where(vals == m, rows, n), axis=0, keepdims=True)
        out_v.append(m)
        out_i.append(idx)
        vals = jnp.where(rows == idx, -jnp.inf, vals)
    return out_v, out_i


def _route_body(h_ref, g_ref, wq_ref, keys_ref, hn_ref, eid_ref, gate_ref):
    hn = _rms(h_ref[...], g_ref[...])
    hn_ref[...] = hn
    q = jnp.dot(hn.astype(BF16), wq_ref[...], preferred_element_type=F32).astype(BF16)
    tm = q.shape[0]
    rows = lax.broadcasted_iota(jnp.int32, (PEER_KEYS, tm), 0)
    rows2 = lax.broadcasted_iota(jnp.int32, (PEER_TOPK * PEER_TOPK, tm), 0)
    half = PEER_QDIM // 2
    eids, gates = [], []
    for hd in range(PEER_HEADS):
        sub = []
        for p in range(2):
            qs = q[:, (2 * hd + p) * half:(2 * hd + p + 1) * half]
            s = lax.dot_general(keys_ref[hd, p], qs, (((1,), (1,)), ((), ())), preferred_element_type=F32)
            sub.append(_top16(s, rows))
        (s1, i1), (s2, i2) = sub
        s2c = jnp.concatenate(s2, axis=0)
        i2c = jnp.concatenate(i2, axis=0)
        cand_s = jnp.concatenate([a + s2c for a in s1], axis=0)
        cand_e = jnp.concatenate([a * PEER_KEYS + i2c for a in i1], axis=0)
        top_s, sel = _top16(cand_s, rows2)
        top_e = [jnp.sum(jnp.where(rows2 == i, cand_e, 0), axis=0, keepdims=True) for i in sel]
        ts_ = jnp.concatenate(top_s, axis=0)
        e = jnp.exp(ts_ - top_s[0])
        gates.append(e / jnp.sum(e, axis=0, keepdims=True))
        eids.append(jnp.concatenate(top_e, axis=0))
    eid_ref[...] = jnp.concatenate(eids, axis=0).T
    gate_ref[...] = jnp.concatenate(gates, axis=0).T


def _route(h, gain, wq_bf, keys_bf, tm):
    t, d = h.shape
    return pl.pallas_call(
        _route_body,
        grid=(t // tm,),
        in_specs=[pl.BlockSpec((tm, d), lambda i: (i, 0)),
                  pl.BlockSpec((1, d), lambda i: (0, 0)),
                  pl.BlockSpec(wq_bf.shape, lambda i: (0, 0)),
                  pl.BlockSpec(keys_bf.shape, lambda i: (0, 0, 0, 0))],
        out_specs=[pl.BlockSpec((tm, d), lambda i: (i, 0)),
                   pl.BlockSpec((tm, PEER_PICKS), lambda i: (i, 0)),
                   pl.BlockSpec((tm, PEER_PICKS), lambda i: (i, 0))],
        out_shape=[jax.ShapeDtypeStruct((t, d), F32),
                   jax.ShapeDtypeStruct((t, PEER_PICKS), jnp.int32),
                   jax.ShapeDtypeStruct((t, PEER_PICKS), F32)],
        compiler_params=_params("parallel"),
        name="peer_route",
    )(h, gain.reshape(1, d), wq_bf, keys_bf)


def _sc_mesh():
    return plsc.VectorSubcoreMesh(core_axis_name="core", subcore_axis_name="subcore")


def _sc_worker():
    return lax.axis_index("subcore") * SC_CORES + lax.axis_index("core")


def _sc_row_pipeline(tab_hbm, idx_v, buf, sems, compute):
    nsub = PEER_PICKS // SC_ROWS
    ng = SC_TOKENS * nsub

    def gather(g, slot):
        rows = idx_v.at[g // nsub, pl.ds((g % nsub) * SC_ROWS, SC_ROWS)]
        return pltpu.make_async_copy(tab_hbm.at[rows], buf.at[slot], sems.at[slot])

    gather(0, 0).start()

    @pl.loop(0, ng, step=2)
    def _(g):
        gather(g + 1, 1).start()
        gather(g, 0).wait()
        compute(g, 0)

        @pl.when(g + 2 < ng)
        def _():
            gather(g + 2, 0).start()

        gather(g + 1, 1).wait()
        compute(g + 1, 1)


def _expert_scores(table, eid, hn):
    t, d = hn.shape
    per_worker = t // SC_WORKERS
    nsub = PEER_PICKS // SC_ROWS
    ln = SC_LANES

    @functools.partial(
        pl.kernel, mesh=_sc_mesh(),
        out_type=jax.ShapeDtypeStruct((t, PEER_PICKS), F32),
        scratch_types=[pltpu.VMEM((SC_TOKENS, PEER_PICKS), jnp.int32),
                       pltpu.VMEM((SC_TOKENS, d), F32),
                       pltpu.VMEM((SC_TOKENS, PEER_PICKS), F32),
                       pltpu.VMEM((2, SC_ROWS, d), F32),
                       pltpu.SemaphoreType.DMA((2,))],
        compiler_params=pltpu.CompilerParams(needs_layout_passes=False),
        name="peer_expert_scores")
    def run(tab_hbm, eid_hbm, hn_hbm, out_hbm, idx_v, x_v, a_v, buf, sems):
        lane = lax.iota(jnp.int32, ln)

        def compute(g, slot):
            tok = g // nsub
            sub = g % nsub
            for grp in range(SC_ROWS // ln):
                def body(c, accs):
                    xc = x_v[tok, pl.ds(c * ln, ln)]
                    return tuple(accs[r] + buf[slot, grp * ln + r, pl.ds(c * ln, ln)] * xc for r in range(ln))

                accs = lax.fori_loop(0, d // ln, body, tuple(jnp.zeros((ln,), F32) for _ in range(ln)))
                res = jnp.zeros((ln,), F32)
                for r in range(ln):
                    res = jnp.where(lane == r, jnp.sum(accs[r]), res)
                a_v[tok, pl.ds(sub * SC_ROWS + grp * ln, ln)] = res

        @pl.loop(0, per_worker // SC_TOKENS)
        def _(blk):
            tok0 = _sc_worker() * per_worker + blk * SC_TOKENS
            pltpu.sync_copy(eid_hbm.at[pl.ds(tok0, SC_TOKENS)], idx_v)
            pltpu.sync_copy(hn_hbm.at[pl.ds(tok0, SC_TOKENS)], x_v)
            _sc_row_pipeline(tab_hbm, idx_v, buf, sems, compute)
            pltpu.sync_copy(a_v, out_hbm.at[pl.ds(tok0, SC_TOKENS)])

    return run(table, eid, hn)


def _expert_mix(table, eid, w):
    t = eid.shape[0]
    d = table.shape[1]
    per_worker = t // SC_WORKERS
    nsub = PEER_PICKS // SC_ROWS
    ln = SC_LANES
    nacc = SC_COLS // ln

    @functools.partial(
        pl.kernel, mesh=_sc_mesh(),
        out_type=jax.ShapeDtypeStruct((t, d), F32),
        scratch_types=[pltpu.VMEM((SC_TOKENS, PEER_PICKS), jnp.int32),
                       pltpu.VMEM((SC_TOKENS, PEER_PICKS), F32),
                       pltpu.VMEM((SC_TOKENS, d), F32),
                       pltpu.VMEM((2, SC_ROWS, d), F32),
                       pltpu.SemaphoreType.DMA((2,))],
        compiler_params=pltpu.CompilerParams(needs_layout_passes=False),
        name="peer_expert_mix")
    def run(tab_hbm, eid_hbm, w_hbm, out_hbm, idx_v, w_v, y_v, buf, sems):
        zero = jnp.zeros((ln,), F32)

        def compute(g, slot):
            tok = g // nsub
            sub = g % nsub
            tokv = jnp.full((ln,), tok, jnp.int32)
            for cc in range(d // SC_COLS):
                def body(r, accs):
                    wr = plsc.load_gather(w_v, [tokv, jnp.full((ln,), sub * SC_ROWS + r, jnp.int32)])
                    return tuple(accs[k] + wr * buf[slot, r, pl.ds(cc * SC_COLS + k * ln, ln)] for k in range(nacc))

                init = tuple(y_v[tok, pl.ds(cc * SC_COLS + k * ln, ln)] for k in range(nacc))
                accs = lax.fori_loop(0, SC_ROWS, body, init)
                for k in range(nacc):
                    y_v[tok, pl.ds(cc * SC_COLS + k * ln, ln)] = accs[k]

        @pl.loop(0, per_worker // SC_TOKENS)
        def _(blk):
            tok0 = _sc_worker() * per_worker + blk * SC_TOKENS
            pltpu.sync_copy(eid_hbm.at[pl.ds(tok0, SC_TOKENS)], idx_v)
            pltpu.sync_copy(w_hbm.at[pl.ds(tok0, SC_TOKENS)], w_v)

            @pl.loop(0, SC_TOKENS)
            def _(tok):
                @pl.loop(0, d // ln)
                def _(c):
                    y_v[tok, pl.ds(c * ln, ln)] = zero

            _sc_row_pipeline(tab_hbm, idx_v, buf, sems, compute)
            pltpu.sync_copy(y_v, out_hbm.at[pl.ds(tok0, SC_TOKENS)])

    return run(table, eid, w)


def _pick_weights_body(a_ref, g_ref, o_ref):
    o_ref[...] = g_ref[...] * _gelu(a_ref[...])


def _pick_weights(a, gate, tm):
    t = a.shape[0]
    spec = pl.BlockSpec((tm, PEER_PICKS), lambda i: (i, 0))
    return pl.pallas_call(
        _pick_weights_body,
        grid=(t // tm,),
        in_specs=[spec, spec],
        out_specs=spec,
        out_shape=jax.ShapeDtypeStruct(a.shape, F32),
        compiler_params=_params("parallel"),
        name="peer_pick_weights",
    )(a, gate)


def _ple_body(h_ref, y_ref, g_ref, wg_ref, p_ref, wp_ref, o_ref):
    h = h_ref[...] + y_ref[...]
    hn = _rms(h, g_ref[...]).astype(BF16)
    gate = _sigmoid(jnp.dot(hn, wg_ref[...], preferred_element_type=F32))
    emb = jnp.dot(p_ref[...].astype(BF16), wp_ref[...], preferred_element_type=F32)
    o_ref[...] = h + gate * emb


def _ple(h, y, gain, wg_bf, p, wp_bf, tm):
    t, d = h.shape
    return pl.pallas_call(
        _ple_body,
        grid=(t // tm,),
        in_specs=[pl.BlockSpec((tm, d), lambda i: (i, 0)),
                  pl.BlockSpec((tm, d), lambda i: (i, 0)),
                  pl.BlockSpec((1, d), lambda i: (0, 0)),
                  pl.BlockSpec(wg_bf.shape, lambda i: (0, 0)),
                  pl.BlockSpec((tm, p.shape[1]), lambda i: (i, 0)),
                  pl.BlockSpec(wp_bf.shape, lambda i: (0, 0))],
        out_specs=pl.BlockSpec((tm, d), lambda i: (i, 0)),
        out_shape=jax.ShapeDtypeStruct((t, d), F32),
        compiler_params=_params("parallel"),
        name="layer_embedding",
    )(h, y, gain.reshape(1, d), wg_bf, p, wp_bf)


def _rope_tables(seq, dim):
    inv = 1.0 / (ROPE_THETA ** (jnp.arange(0, dim, 2, dtype=F32) / dim))
    ang = jnp.arange(seq, dtype=F32)[:, None] * inv[None, :]
    return jnp.cos(ang), jnp.sin(ang)


def _permute_in_columns(w_in):
    return jnp.concatenate([w_in[:, 3840:], w_in[:, :3840]], axis=1)


def kernel(x, p, norm_mix, w_in, ret_decay, ret_norm, sgu_ln_g, sgu_ln_b, sgu_w, sgu_b, att_q_norm, att_k_norm, att_sink, w_proj_ret, w_proj_sgu, w_proj_att, w_out, norm_ffn, peer_wq, peer_keys, peer_u, peer_v, norm_ple, ple_gate, ple_proj):
    batch, seq, d = x.shape
    depth = w_in.shape[0]
    groups = PIPELINE_GROUPS if batch % PIPELINE_GROUPS == 0 else 1
    gb = batch // groups
    t = gb * seq
    ts = min(512, seq)
    tm = min(512, t)
    cos_r, sin_r = _rope_tables(seq, RET_DK)
    cos_r2 = jnp.concatenate([cos_r, cos_r], axis=1)
    sin_r2 = jnp.concatenate([-sin_r, sin_r], axis=1)
    cos_a, sin_a = _rope_tables(seq, ATT_DH)
    hs = [x[g * gb:(g + 1) * gb].reshape(t, d) for g in range(groups)]
    for i in range(depth):
        w_in_bf = _permute_in_columns(w_in[i]).astype(BF16)
        w_r, w_s, w_a = w_proj_ret[i].astype(BF16), w_proj_sgu[i].astype(BF16), w_proj_att[i].astype(BF16)
        w_o, w_q, keys = w_out[i].astype(BF16), peer_wq[i].astype(BF16), peer_keys[i].astype(BF16)
        w_g, w_p = ple_gate[i].astype(BF16), ple_proj[i].astype(BF16)
        for g in range(groups):
            h = hs[g]
            z = _norm_matmul(h, norm_mix[i], w_in_bf, tm, 1152)
            y_r = _retention(z, ret_decay[i], ret_norm[i], cos_r2, sin_r2, gb, seq, ts)
            y_s = _sgu(z, sgu_ln_g[i], sgu_ln_b[i], sgu_w[i], sgu_b[i], ts)
            y_a = _attention(z, att_q_norm[i], att_k_norm[i], att_sink[i], cos_a, sin_a, gb, seq, ts)
            h = _merge(h, z, y_r, y_s, y_a, w_r, w_s, w_a, w_o, tm)
            hn, eid, gate = _route(h, norm_ffn[i], w_q, keys, min(256, t))
            a = _expert_scores(peer_u[i], eid, hn)
            w = _pick_weights(a, gate, tm)
            y = _expert_mix(peer_v[i], eid, w)
            p_g = p[i, g * gb:(g + 1) * gb].reshape(t, -1)
            hs[g] = _ple(h, y, norm_ple[i], w_g, p_g, w_p, tm)
    return jnp.concatenate(hs, axis=0).reshape(batch, seq, d)
```

```python
import functools
import math

import jax
import jax.numpy as jnp
from jax import lax
from jax.experimental import pallas as pl
from jax.experimental.pallas import tpu as pltpu
from jax.experimental.pallas import tpu_sc as plsc

F32 = jnp.float32
BF16 = jnp.bfloat16

D_MODEL = 1024
PLE_DIM = 256
CHUNK = 128
EPS = 1e-6
ROPE_THETA = 10000.0
RET_HEADS = 4
RET_DK = 128
SGU_GROUPS = 4
SGU_WIDTH = 512
ATT_HEADS = 8
ATT_KV_HEADS = 2
ATT_DH = 64
ATT_GROUP = ATT_HEADS // ATT_KV_HEADS
NEG_INF = -1e30
PEER_HEADS = 8
PEER_KEYS = 128
PEER_QDIM = 256
PEER_TOPK = 16
PEER_PICKS = PEER_HEADS * PEER_TOPK

Z_GATE_R, Z_GATE_S, Z_GATE_A = 0, 1024, 2048
Z_QR, Z_KR, Z_VR, Z_GR = 3072, 3584, 4096, 4608
Z_US, Z_VS = 5120, 5632
Z_QA, Z_KA, Z_VA = 6144, 6656, 6784
D_IN = 6912

LANE = 128
SC_CORES = 2
SC_SUBCORES = 16
SC_LANES = 16
SC_WORKERS = SC_CORES * SC_SUBCORES
SC_TOKENS = 32
SC_ROWS = 64
SC_WORDS = 128
HI_MASK = -65536
VMEM_LIMIT = 56 * 1024 * 1024
PIPELINE_GROUPS = 4


def _params(*sem):
    return pltpu.CompilerParams(dimension_semantics=sem, vmem_limit_bytes=VMEM_LIMIT)


def _gelu(x):
    return 0.5 * x * (1.0 + lax.erf(x * (1.0 / math.sqrt(2.0))))


def _sigmoid(x):
    return 1.0 / (1.0 + jnp.exp(-x))


def _rms(x, g):
    return x * lax.rsqrt(jnp.mean(x * x, axis=-1, keepdims=True) + EPS) * g


def _bf16_hi_bits(x):
    b = lax.bitcast_convert_type(x, jnp.int32)
    return (b + 0x7FFF + ((b >> 16) & 1)) & HI_MASK


def _pack_halves(x):
    n = x.shape[1] // 2
    return _bf16_hi_bits(x[:, :n]) | lax.shift_right_logical(_bf16_hi_bits(x[:, n:]), 16)


def _pack_twice(x):
    b = _bf16_hi_bits(x)
    return b | lax.shift_right_logical(b, 16)


def _inproj_body(x_ref, g_ref, w_ref, o_ref, xn_ref):
    @pl.when(pl.program_id(1) == 0)
    def _():
        xn_ref[...] = _rms(x_ref[...], g_ref[...]).astype(BF16)

    o_ref[...] = jnp.dot(xn_ref[...], w_ref[...], preferred_element_type=F32)


def _norm_matmul(h, gain, w_bf, tm, tn):
    t, d = h.shape
    n = w_bf.shape[1]
    return pl.pallas_call(
        _inproj_body,
        grid=(t // tm, n // tn),
        in_specs=[pl.BlockSpec((tm, d), lambda i, j: (i, 0)),
                  pl.BlockSpec((1, d), lambda i, j: (0, 0)),
                  pl.BlockSpec((d, tn), lambda i, j: (0, j))],
        out_specs=pl.BlockSpec((tm, tn), lambda i, j: (i, j)),
        out_shape=jax.ShapeDtypeStruct((t, n), F32),
        scratch_shapes=[pltpu.VMEM((tm, d), BF16)],
        compiler_params=_params("parallel", "arbitrary"),
        name="norm_matmul",
    )(h, gain.reshape(1, d), w_bf)


def _rope128(x, cos, sin_signed):
    return x * cos + pltpu.roll(x, 64, 1) * sin_signed


def _ret_bwd_body(q_ref, k_ref, v_ref, cos_ref, sin_ref, qw_ref, kw_ref, cd_ref, o_ref, st_ref, *, nchunk):
    @pl.when(pl.program_id(2) == 0)
    def _():
        st_ref[...] = jnp.zeros_like(st_ref)

    qw = qw_ref[0]
    kw = kw_ref[0]
    cd = cd_ref[0, 0:1, :]
    for c in reversed(range(nchunk)):
        rows = pl.ds(c * CHUNK, CHUNK)
        cos = cos_ref[rows, :]
        sin = sin_ref[rows, :]
        q = _rope128(q_ref[rows, :], cos, sin)
        k = _rope128(k_ref[rows, :], cos, sin) * (RET_DK ** -0.5)
        v = v_ref[rows, :]
        st = st_ref[...]
        o_ref[rows, :] = jnp.dot((q * qw).astype(BF16), st.astype(BF16), preferred_element_type=F32)
        kv = jnp.dot((k * kw).T.astype(BF16), v.astype(BF16), preferred_element_type=F32)
        st_ref[...] = st * cd + kv


def _ret_fwd_body(q_ref, k_ref, v_ref, g_ref, yb_ref, cos_ref, sin_ref, dm_ref, qw_ref, kw_ref, cd_ref,
                  gn_ref, o_ref, st_ref, *, nchunk):
    @pl.when(pl.program_id(2) == 0)
    def _():
        st_ref[...] = jnp.zeros_like(st_ref)

    qw = qw_ref[0]
    kw = kw_ref[0]
    cd = cd_ref[0, 0:1, :]
    dm = dm_ref[0]
    gn = gn_ref[0, 0:1, :]
    for c in range(nchunk):
        rows = pl.ds(c * CHUNK, CHUNK)
        cos = cos_ref[rows, :]
        sin = sin_ref[rows, :]
        q = _rope128(q_ref[rows, :], cos, sin)
        k = _rope128(k_ref[rows, :], cos, sin) * (RET_DK ** -0.5)
        v = v_ref[rows, :].astype(BF16)
        st = st_ref[...]
        s = lax.dot_general(q.astype(BF16), k.astype(BF16), (((1,), (1,)), ((), ())),
                            preferred_element_type=F32) * dm
        y = jnp.dot(s.astype(BF16), v, preferred_element_type=F32)
        y += jnp.dot((q * qw).astype(BF16), st.astype(BF16), preferred_element_type=F32)
        y += yb_ref[rows, :]
        kv = jnp.dot((k * kw).T.astype(BF16), v, preferred_element_type=F32)
        st_ref[...] = st * cd + kv
        y = y * lax.rsqrt(jnp.mean(y * y, axis=-1, keepdims=True) + EPS) * gn
        g = g_ref[rows, :]
        o_ref[rows, :] = (g * _sigmoid(g) * y).astype(o_ref.dtype)


def _retention(z, ret_decay, ret_norm, cos, sin, batch, seq, ts):
    t = z.shape[0]
    nchunk = ts // CHUNK
    nstep = seq // ts
    hd = RET_HEADS
    log_g = jax.nn.log_sigmoid(ret_decay.astype(F32))
    idx = jnp.arange(CHUNK, dtype=F32)
    diff = idx[:, None] - idx[None, :]
    lf = log_g[0][:, None, None]
    lb = log_g[1][:, None, None]
    dmat = jnp.where(diff[None] >= 0, jnp.exp(lf * jnp.maximum(diff, 0.0)[None]),
                     jnp.exp(lb * jnp.maximum(-diff, 0.0)[None]))
    bc = lambda a: jnp.broadcast_to(a[:, :, None], (hd, a.shape[1], LANE))
    qw_f = bc(jnp.exp(log_g[0][:, None] * (idx + 1.0)[None, :]))
    kw_f = bc(jnp.exp(log_g[0][:, None] * (CHUNK - 1 - idx)[None, :]))
    qw_b = bc(jnp.exp(log_g[1][:, None] * (CHUNK - idx)[None, :]))
    kw_b = bc(jnp.exp(log_g[1][:, None] * idx[None, :]))
    cd_f = jnp.broadcast_to(jnp.exp(log_g[0] * CHUNK)[:, None, None], (hd, 8, LANE))
    cd_b = jnp.broadcast_to(jnp.exp(log_g[1] * CHUNK)[:, None, None], (hd, 8, LANE))
    gn = jnp.broadcast_to(ret_norm.astype(F32).reshape(hd, 1, LANE), (hd, 8, LANE))

    def zspec(col0, rev):
        cb = col0 // LANE
        if rev:
            return pl.BlockSpec((ts, LANE), lambda b, h, s: (b * nstep + nstep - 1 - s, cb + h))
        return pl.BlockSpec((ts, LANE), lambda b, h, s: (b * nstep + s, cb + h))

    def tspec(rev):
        if rev:
            return pl.BlockSpec((ts, LANE), lambda b, h, s: (nstep - 1 - s, 0))
        return pl.BlockSpec((ts, LANE), lambda b, h, s: (s, 0))

    hspec = lambda r: pl.BlockSpec((1, r, LANE), lambda b, h, s: (h, 0, 0))

    yb = pl.pallas_call(
        functools.partial(_ret_bwd_body, nchunk=nchunk),
        grid=(batch, hd, nstep),
        in_specs=[zspec(Z_QR, True), zspec(Z_KR, True), zspec(Z_VR, True), tspec(True), tspec(True),
                  hspec(CHUNK), hspec(CHUNK), hspec(8)],
        out_specs=pl.BlockSpec((ts, LANE), lambda b, h, s: (b * nstep + nstep - 1 - s, h)),
        out_shape=jax.ShapeDtypeStruct((t, hd * LANE), F32),
        scratch_shapes=[pltpu.VMEM((RET_DK, LANE), F32)],
        compiler_params=_params("parallel", "parallel", "arbitrary"),
        name="retention_bwd",
    )(z, z, z, cos, sin, qw_b, kw_b, cd_b)

    return pl.pallas_call(
        functools.partial(_ret_fwd_body, nchunk=nchunk),
        grid=(batch, hd, nstep),
        in_specs=[zspec(Z_QR, False), zspec(Z_KR, False), zspec(Z_VR, False), zspec(Z_GR, False),
                  pl.BlockSpec((ts, LANE), lambda b, h, s: (b * nstep + s, h)),
                  tspec(False), tspec(False), hspec(CHUNK), hspec(CHUNK), hspec(CHUNK), hspec(8), hspec(8)],
        out_specs=pl.BlockSpec((ts, LANE), lambda b, h, s: (b * nstep + s, h)),
        out_shape=jax.ShapeDtypeStruct((t, hd * LANE), BF16),
        scratch_shapes=[pltpu.VMEM((RET_DK, LANE), F32)],
        compiler_params=_params("parallel", "parallel", "arbitrary"),
        name="retention_fwd",
    )(z, z, z, z, yb, cos, sin, dmat, qw_f, kw_f, cd_f, gn)


def _sgu_body(u_ref, v_ref, lg_ref, lb_ref, w_ref, b_ref, o_ref, *, nchunk):
    lg = lg_ref[...]
    lb = lb_ref[...]
    for c in range(nchunk):
        rows = pl.ds(c * CHUNK, CHUNK)
        vf = _gelu(v_ref[rows, :])
        mu = jnp.mean(vf, axis=-1, keepdims=True)
        vc = vf - mu
        var = jnp.mean(vc * vc, axis=-1, keepdims=True)
        vn = (vc * lax.rsqrt(var + EPS) * lg + lb).astype(BF16)
        for g in range(SGU_GROUPS):
            cols = slice(g * LANE, (g + 1) * LANE)
            mixed = jnp.dot(w_ref[g], vn[:, cols], preferred_element_type=F32) + b_ref[g]
            uf = _gelu(u_ref[rows, cols])
            o_ref[rows, cols] = (uf * mixed).astype(o_ref.dtype)


def _sgu(z, ln_g, ln_b, w_s, b_s, ts):
    t = z.shape[0]
    bias = jnp.broadcast_to(b_s.astype(F32)[:, :, None], (SGU_GROUPS, CHUNK, LANE))
    return pl.pallas_call(
        functools.partial(_sgu_body, nchunk=ts // CHUNK),
        grid=(t // ts,),
        in_specs=[pl.BlockSpec((ts, SGU_WIDTH), lambda i: (i, Z_US // SGU_WIDTH)),
                  pl.BlockSpec((ts, SGU_WIDTH), lambda i: (i, Z_VS // SGU_WIDTH)),
                  pl.BlockSpec((1, SGU_WIDTH), lambda i: (0, 0)),
                  pl.BlockSpec((1, SGU_WIDTH), lambda i: (0, 0)),
                  pl.BlockSpec((SGU_GROUPS, CHUNK, CHUNK), lambda i: (0, 0, 0)),
                  pl.BlockSpec((SGU_GROUPS, CHUNK, LANE), lambda i: (0, 0, 0))],
        out_specs=pl.BlockSpec((ts, SGU_WIDTH), lambda i: (i, 0)),
        out_shape=jax.ShapeDtypeStruct((t, SGU_WIDTH), BF16),
        compiler_params=_params("parallel"),
        name="spatial_gating",
    )(z, z, ln_g.reshape(1, -1).astype(F32), ln_b.reshape(1, -1).astype(F32), w_s.astype(BF16), bias)


def _pair_norm_rope(x, gain, cos, sin_up, sin_dn, low):
    sq = x * x
    lo = jnp.sum(jnp.where(low, sq, 0.0), axis=-1, keepdims=True)
    hi = jnp.sum(sq, axis=-1, keepdims=True) - lo
    ms = jnp.where(low, lo, hi) * (1.0 / ATT_DH)
    xn = x * lax.rsqrt(ms + EPS) * gain
    return xn * cos + pltpu.roll(xn, LANE - 32, 1) * sin_up + pltpu.roll(xn, 32, 1) * sin_dn


def _attn_body(sink_ref, q_ref, kp_ref, k_ref, kn_ref, vp_ref, v_ref, vn_ref,
               cq_ref, suq_ref, sdq_ref, ckp_ref, sukp_ref, sdkp_ref, ckn_ref, sukn_ref, sdkn_ref,
               qg_ref, kg_ref, o_ref, *, nchunk, nstep):
    s_id = pl.program_id(1)
    ts = nchunk * CHUNK
    lane = lax.broadcasted_iota(jnp.int32, (1, LANE), 1)
    low = lane < ATT_DH
    kg = kg_ref[...]
    qg = qg_ref[...]
    k_ext = jnp.concatenate([
        _pair_norm_rope(kp_ref[...], kg, ckp_ref[...], sukp_ref[...], sdkp_ref[...], low),
        _pair_norm_rope(k_ref[...], kg, cq_ref[...], suq_ref[...], sdq_ref[...], low),
        _pair_norm_rope(kn_ref[...], kg, ckn_ref[...], sukn_ref[...], sdkn_ref[...], low)], axis=0)
    v_ext = jnp.concatenate([vp_ref[...], v_ref[...], vn_ref[...]], axis=0).astype(BF16)
    k_lo = jnp.where(low, k_ext, 0.0).astype(BF16)
    k_hi = jnp.where(low, 0.0, k_ext).astype(BF16)
    qi = lax.broadcasted_iota(jnp.int32, (CHUNK, 3 * CHUNK), 0)
    kj = lax.broadcasted_iota(jnp.int32, (CHUNK, 3 * CHUNK), 1)
    band = jnp.abs(qi + CHUNK - kj) <= CHUNK
    for c in range(nchunk):
        rows = pl.ds(c * CHUNK, CHUNK)
        first = jnp.logical_and(s_id == 0, c == 0)
        last = jnp.logical_and(s_id == nstep - 1, c == nchunk - 1)
        valid = band
        if c == 0:
            valid = jnp.logical_and(valid, jnp.logical_or(kj >= CHUNK, jnp.logical_not(first)))
        if c == nchunk - 1:
            valid = jnp.logical_and(valid, jnp.logical_or(kj < 2 * CHUNK, jnp.logical_not(last)))
        kc_lo = k_lo[c * CHUNK:(c + 3) * CHUNK]
        kc_hi = k_hi[c * CHUNK:(c + 3) * CHUNK]
        vc = v_ext[c * CHUNK:(c + 3) * CHUNK]
        cos = cq_ref[rows, :]
        su = suq_ref[rows, :]
        sd = sdq_ref[rows, :]
        for pair in range(ATT_HEADS // 2):
            cols = slice(pair * LANE, (pair + 1) * LANE)
            qp = _pair_norm_rope(q_ref[rows, cols], qg, cos, su, sd, low) * (ATT_DH ** -0.5)
            kv_head = (2 * pair) // ATT_GROUP
            outs = []
            for half in range(2):
                head = 2 * pair + half
                qh = qp if half == kv_head else pltpu.roll(qp, ATT_DH, 1)
                if kv_head == 0:
                    qh = jnp.where(low, qh, 0.0)
                    kc = kc_lo
                else:
                    qh = jnp.where(low, 0.0, qh)
                    kc = kc_hi
                s = lax.dot_general(qh.astype(BF16), kc, (((1,), (1,)), ((), ())), preferred_element_type=F32)
                s = jnp.where(valid, s, NEG_INF)
                sk = sink_ref[head]
                m = jnp.maximum(jnp.max(s, axis=-1, keepdims=True), sk)
                e = jnp.exp(s - m)
                den = jnp.sum(e, axis=-1, keepdims=True) + jnp.exp(sk - m)
                o = jnp.dot(e.astype(BF16), vc, preferred_element_type=F32) / den
                outs.append(o if half == kv_head else pltpu.roll(o, ATT_DH, 1))
            o_ref[rows, cols] = jnp.where(low, outs[0], outs[1]).astype(o_ref.dtype)


def _attention(z, q_gain, k_gain, sink, cos, sin, batch, seq, ts):
    t = z.shape[0]
    nchunk = ts // CHUNK
    nstep = seq // ts
    nblk = seq // CHUNK
    cos2 = jnp.tile(jnp.concatenate([cos, cos], axis=1), (1, 2))
    zero = jnp.zeros_like(sin)
    sin_up = jnp.tile(jnp.concatenate([-sin, zero], axis=1), (1, 2))
    sin_dn = jnp.tile(jnp.concatenate([zero, sin], axis=1), (1, 2))
    qg = jnp.tile(q_gain.astype(F32), 2).reshape(1, LANE)
    kg = jnp.tile(k_gain.astype(F32), 2).reshape(1, LANE)

    kcb, vcb = Z_KA // LANE, Z_VA // LANE
    prev_blk = lambda s: jnp.maximum(s * nchunk - 1, 0)
    next_blk = lambda s: jnp.minimum((s + 1) * nchunk, nblk - 1)
    main = lambda cb: pl.BlockSpec((ts, LANE), lambda b, s: (b * nstep + s, cb))
    prev = lambda cb: pl.BlockSpec((CHUNK, LANE), lambda b, s: (b * nblk + prev_blk(s), cb))
    nxt = lambda cb: pl.BlockSpec((CHUNK, LANE), lambda b, s: (b * nblk + next_blk(s), cb))
    tmain = pl.BlockSpec((ts, LANE), lambda b, s: (s, 0))
    tprev = pl.BlockSpec((CHUNK, LANE), lambda b, s: (prev_blk(s), 0))
    tnext = pl.BlockSpec((CHUNK, LANE), lambda b, s: (next_blk(s), 0))
    one = pl.BlockSpec((1, LANE), lambda b, s: (0, 0))
    return pl.pallas_call(
        functools.partial(_attn_body, nchunk=nchunk, nstep=nstep),
        grid=(batch, nstep),
        in_specs=[pl.BlockSpec(memory_space=pltpu.SMEM),
                  pl.BlockSpec((ts, ATT_HEADS * ATT_DH), lambda b, s: (b * nstep + s, Z_QA // 512)),
                  prev(kcb), main(kcb), nxt(kcb), prev(vcb), main(vcb), nxt(vcb),
                  tmain, tmain, tmain, tprev, tprev, tprev, tnext, tnext, tnext, one, one],
        out_specs=pl.BlockSpec((ts, ATT_HEADS * ATT_DH), lambda b, s: (b * nstep + s, 0)),
        out_shape=jax.ShapeDtypeStruct((t, ATT_HEADS * ATT_DH), BF16),
        compiler_params=_params("parallel", "arbitrary"),
        name="window_attention",
    )(sink.astype(F32), z, z, z, z, z, z, z,
      cos2, sin_up, sin_dn, cos2, sin_up, sin_dn, cos2, sin_up, sin_dn, qg, kg)


def _merge_body(h_ref, gr_ref, gs_ref, ga_ref, yr_ref, ys_ref, ya_ref, wr_ref, ws_ref, wa_ref, wo_ref, o_ref):
    m = _sigmoid(gr_ref[...]) * jnp.dot(yr_ref[...], wr_ref[...], preferred_element_type=F32)
    m += _sigmoid(gs_ref[...]) * jnp.dot(ys_ref[...], ws_ref[...], preferred_element_type=F32)
    m += _sigmoid(ga_ref[...]) * jnp.dot(ya_ref[...], wa_ref[...], preferred_element_type=F32)
    o_ref[...] = h_ref[...] + jnp.dot(m.astype(BF16), wo_ref[...], preferred_element_type=F32)


def _merge(h, z, y_r, y_s, y_a, w_r, w_s, w_a, w_o, tm):
    t, d = h.shape
    row = lambda w: pl.BlockSpec((tm, w), lambda i: (i, 0))
    gate = lambda col0: pl.BlockSpec((tm, d), lambda i: (i, col0 // d))
    full = lambda a: pl.BlockSpec(a.shape, lambda i: (0, 0))
    return pl.pallas_call(
        _merge_body,
        grid=(t // tm,),
        in_specs=[row(d), gate(Z_GATE_R), gate(Z_GATE_S), gate(Z_GATE_A), row(512), row(512), row(512),
                  full(w_r), full(w_s), full(w_a), full(w_o)],
        out_specs=row(d),
        out_shape=jax.ShapeDtypeStruct((t, d), F32),
        compiler_params=_params("parallel"),
        name="branch_merge",
    )(h, z, z, z, y_r, y_s, y_a, w_r, w_s, w_a, w_o)


def _top16(vals, rows):
    n = vals.shape[0]
    out_v, out_i = [], []
    for _ in range(PEER_TOPK):
        m = jnp.max(vals, axis=0, keepdims=True)
        idx = jnp.min(jnp.where(vals == m, rows, n), axis=0, keepdims=True)
        out_v.append(m)
        out_i.append(idx)
        vals = jnp.where(rows == idx, -jnp.inf, vals)
    return out_v, out_i


def _route_body(h_ref, g_ref, wq_ref, keys_ref, hn_ref, eid_ref, gate_ref):
    hn = _rms(h_ref[...], g_ref[...])
    hn_ref[...] = _pack_halves(hn)
    q = jnp.dot(hn.astype(BF16), wq_ref[...], preferred_element_type=F32).astype(BF16)
    tm = q.shape[0]
    rows = lax.broadcasted_iota(jnp.int32, (PEER_KEYS, tm), 0)
    rows2 = lax.broadcasted_iota(jnp.int32, (PEER_TOPK * PEER_TOPK, tm), 0)
    half = PEER_QDIM // 2
    eids, gates = [], []
    for hd in range(PEER_HEADS):
        sub = []
        for p in range(2):
            qs = q[:, (2 * hd + p) * half:(2 * hd + p + 1) * half]
            s = lax.dot_general(keys_ref[hd, p], qs, (((1,), (1,)), ((), ())), preferred_element_type=F32)
            sub.append(_top16(s, rows))
        (s1, i1), (s2, i2) = sub
        s2c = jnp.concatenate(s2, axis=0)
        i2c = jnp.concatenate(i2, axis=0)
        cand_s = jnp.concatenate([a + s2c for a in s1], axis=0)
        cand_e = jnp.concatenate([a * PEER_KEYS + i2c for a in i1], axis=0)
        top_s, sel = _top16(cand_s, rows2)
        top_e = [jnp.sum(jnp.where(rows2 == i, cand_e, 0), axis=0, keepdims=True) for i in sel]
        ts_ = jnp.concatenate(top_s, axis=0)
        e = jnp.exp(ts_ - top_s[0])
        gates.append(e / jnp.sum(e, axis=0, keepdims=True))
        eids.append(jnp.concatenate(top_e, axis=0))
    eid_ref[...] = jnp.concatenate(eids, axis=0).T
    gate_ref[...] = jnp.concatenate(gates, axis=0).T


def _route(h, gain, wq_bf, keys_bf, tm):
    t, d = h.shape
    return pl.pallas_call(
        _route_body,
        grid=(t // tm,),
        in_specs=[pl.BlockSpec((tm, d), lambda i: (i, 0)),
                  pl.BlockSpec((1, d), lambda i: (0, 0)),
                  pl.BlockSpec(wq_bf.shape, lambda i: (0, 0)),
                  pl.BlockSpec(keys_bf.shape, lambda i: (0, 0, 0, 0))],
        out_specs=[pl.BlockSpec((tm, d // 2), lambda i: (i, 0)),
                   pl.BlockSpec((tm, PEER_PICKS), lambda i: (i, 0)),
                   pl.BlockSpec((tm, PEER_PICKS), lambda i: (i, 0))],
        out_shape=[jax.ShapeDtypeStruct((t, d // 2), jnp.int32),
                   jax.ShapeDtypeStruct((t, PEER_PICKS), jnp.int32),
                   jax.ShapeDtypeStruct((t, PEER_PICKS), F32)],
        compiler_params=_params("parallel"),
        name="peer_route",
    )(h, gain.reshape(1, d), wq_bf, keys_bf)


def _sc_mesh():
    return plsc.VectorSubcoreMesh(core_axis_name="core", subcore_axis_name="subcore")


def _sc_worker():
    return lax.axis_index("subcore") * SC_CORES + lax.axis_index("core")


def _sc_row_pipeline(tab_hbm, idx_v, buf, sems, compute):
    nsub = PEER_PICKS // SC_ROWS
    ng = SC_TOKENS * nsub

    def gather(g, slot):
        rows = idx_v.at[g // nsub, pl.ds((g % nsub) * SC_ROWS, SC_ROWS)]
        return pltpu.make_async_copy(tab_hbm.at[rows], buf.at[slot], sems.at[slot])

    gather(0, 0).start()

    @pl.loop(0, ng, step=2)
    def _(g):
        gather(g + 1, 1).start()
        gather(g, 0).wait()
        compute(g, 0)

        @pl.when(g + 2 < ng)
        def _():
            gather(g + 2, 0).start()

        gather(g + 1, 1).wait()
        compute(g + 1, 1)


def _sc_split(words):
    return (lax.bitcast_convert_type(words & HI_MASK, F32), lax.bitcast_convert_type(words << 16, F32))


def _expert_scores(table, eid, hn):
    t, words = hn.shape
    per_worker = t // SC_WORKERS
    nsub = PEER_PICKS // SC_ROWS
    ln = SC_LANES

    @functools.partial(
        pl.kernel, mesh=_sc_mesh(),
        out_type=jax.ShapeDtypeStruct((t, PEER_PICKS), F32),
        scratch_types=[pltpu.VMEM((SC_TOKENS, PEER_PICKS), jnp.int32),
                       pltpu.VMEM((SC_TOKENS, words), jnp.int32),
                       pltpu.VMEM((SC_TOKENS, PEER_PICKS), F32),
                       pltpu.VMEM((2, SC_ROWS, words), jnp.int32),
                       pltpu.SemaphoreType.DMA((2,))],
        compiler_params=pltpu.CompilerParams(needs_layout_passes=False),
        name="peer_expert_scores")
    def run(tab_hbm, eid_hbm, hn_hbm, out_hbm, idx_v, x_v, a_v, buf, sems):
        lane = lax.iota(jnp.int32, ln)

        def compute(g, slot):
            tok = g // nsub
            sub = g % nsub
            for grp in range(SC_ROWS // ln):
                def body(c, accs):
                    c0 = pl.ds(2 * c * ln, ln)
                    c1 = pl.ds((2 * c + 1) * ln, ln)
                    x0 = plsc.bitcast(x_v[tok, c0], BF16)
                    x1 = plsc.bitcast(x_v[tok, c1], BF16)
                    out = []
                    for r in range(ln):
                        u0 = plsc.bitcast(buf[slot, grp * ln + r, c0], BF16)
                        u1 = plsc.bitcast(buf[slot, grp * ln + r, c1], BF16)
                        hi, lo = _sc_split(plsc.bitcast(u0 * x0 + u1 * x1, jnp.int32))
                        out.append(accs[r] + hi + lo)
                    return tuple(out)

                accs = lax.fori_loop(0, words // (2 * ln), body, tuple(jnp.zeros((ln,), F32) for _ in range(ln)))
                res = jnp.zeros((ln,), F32)
                for r in range(ln):
                    res = jnp.where(lane == r, jnp.sum(accs[r]), res)
                a_v[tok, pl.ds(sub * SC_ROWS + grp * ln, ln)] = res

        @pl.loop(0, per_worker // SC_TOKENS)
        def _(blk):
            tok0 = _sc_worker() * per_worker + blk * SC_TOKENS
            pltpu.sync_copy(eid_hbm.at[pl.ds(tok0, SC_TOKENS)], idx_v)
            pltpu.sync_copy(hn_hbm.at[pl.ds(tok0, SC_TOKENS)], x_v)
            _sc_row_pipeline(tab_hbm, idx_v, buf, sems, compute)
            pltpu.sync_copy(a_v, out_hbm.at[pl.ds(tok0, SC_TOKENS)])

    return run(table, eid, hn)


def _expert_mix(table, eid, w):
    t = eid.shape[0]
    words = table.shape[1]
    d = 2 * words
    per_worker = t // SC_WORKERS
    nsub = PEER_PICKS // SC_ROWS
    ln = SC_LANES
    nvec = SC_WORDS // ln

    @functools.partial(
        pl.kernel, mesh=_sc_mesh(),
        out_type=jax.ShapeDtypeStruct((t, d), F32),
        scratch_types=[pltpu.VMEM((SC_TOKENS, PEER_PICKS), jnp.int32),
                       pltpu.VMEM((SC_TOKENS, PEER_PICKS), jnp.int32),
                       pltpu.VMEM((SC_TOKENS, d), F32),
                       pltpu.VMEM((2, SC_ROWS, words), jnp.int32),
                       pltpu.SemaphoreType.DMA((2,))],
        compiler_params=pltpu.CompilerParams(needs_layout_passes=False),
        name="peer_expert_mix")
    def run(tab_hbm, eid_hbm, w_hbm, out_hbm, idx_v, w_v, y_v, buf, sems):
        zero = jnp.zeros((ln,), F32)

        def compute(g, slot):
            tok = g // nsub
            sub = g % nsub
            tokv = jnp.full((ln,), tok, jnp.int32)

            def weight(r):
                pick = jnp.full((ln,), sub * SC_ROWS + r, jnp.int32)
                return plsc.bitcast(plsc.load_gather(w_v, [tokv, pick]), BF16)

            for cc in range(words // SC_WORDS):
                def body(rp, accs):
                    r = 2 * rp
                    w0 = weight(r)
                    w1 = weight(r + 1)
                    out = list(accs)
                    for k in range(nvec):
                        cols = pl.ds(cc * SC_WORDS + k * ln, ln)
                        v0 = plsc.bitcast(buf[slot, r, cols], BF16)
                        v1 = plsc.bitcast(buf[slot, r + 1, cols], BF16)
                        hi, lo = _sc_split(plsc.bitcast(w0 * v0 + w1 * v1, jnp.int32))
                        out[k] = accs[k] + hi
                        out[nvec + k] = accs[nvec + k] + lo
                    return tuple(out)

                lo_cols = [pl.ds(cc * SC_WORDS + k * ln, ln) for k in range(nvec)]
                hi_cols = [pl.ds(words + cc * SC_WORDS + k * ln, ln) for k in range(nvec)]
                init = tuple(y_v[tok, c] for c in lo_cols + hi_cols)
                accs = lax.fori_loop(0, SC_ROWS // 2, body, init)
                for c, acc in zip(lo_cols + hi_cols, accs):
                    y_v[tok, c] = acc

        @pl.loop(0, per_worker // SC_TOKENS)
        def _(blk):
            tok0 = _sc_worker() * per_worker + blk * SC_TOKENS
            pltpu.sync_copy(eid_hbm.at[pl.ds(tok0, SC_TOKENS)], idx_v)
            pltpu.sync_copy(w_hbm.at[pl.ds(tok0, SC_TOKENS)], w_v)

            @pl.loop(0, SC_TOKENS)
            def _(tok):
                @pl.loop(0, d // ln)
                def _(c):
                    y_v[tok, pl.ds(c * ln, ln)] = zero

            _sc_row_pipeline(tab_hbm, idx_v, buf, sems, compute)
            pltpu.sync_copy(y_v, out_hbm.at[pl.ds(tok0, SC_TOKENS)])

    return run(table, eid, w)


def _pick_weights_body(a_ref, g_ref, o_ref):
    o_ref[...] = _pack_twice(g_ref[...] * _gelu(a_ref[...]))


def _pick_weights(a, gate, tm):
    t = a.shape[0]
    spec = pl.BlockSpec((tm, PEER_PICKS), lambda i: (i, 0))
    return pl.pallas_call(
        _pick_weights_body,
        grid=(t // tm,),
        in_specs=[spec, spec],
        out_specs=spec,
        out_shape=jax.ShapeDtypeStruct(a.shape, jnp.int32),
        compiler_params=_params("parallel"),
        name="peer_pick_weights",
    )(a, gate)


def _ple_body(h_ref, y_ref, g_ref, wg_ref, p_ref, wp_ref, o_ref):
    h = h_ref[...] + y_ref[...]
    hn = _rms(h, g_ref[...]).astype(BF16)
    gate = _sigmoid(jnp.dot(hn, wg_ref[...], preferred_element_type=F32))
    emb = jnp.dot(p_ref[...].astype(BF16), wp_ref[...], preferred_element_type=F32)
    o_ref[...] = h + gate * emb


def _ple(h, y, gain, wg_bf, p, wp_bf, tm):
    t, d = h.shape
    return pl.pallas_call(
        _ple_body,
        grid=(t // tm,),
        in_specs=[pl.BlockSpec((tm, d), lambda i: (i, 0)),
                  pl.BlockSpec((tm, d), lambda i: (i, 0)),
                  pl.BlockSpec((1, d), lambda i: (0, 0)),
                  pl.BlockSpec(wg_bf.shape, lambda i: (0, 0)),
                  pl.BlockSpec((tm, p.shape[1]), lambda i: (i, 0)),
                  pl.BlockSpec(wp_bf.shape, lambda i: (0, 0))],
        out_specs=pl.BlockSpec((tm, d), lambda i: (i, 0)),
        out_shape=jax.ShapeDtypeStruct((t, d), F32),
        compiler_params=_params("parallel"),
        name="layer_embedding",
    )(h, y, gain.reshape(1, d), wg_bf, p, wp_bf)


def _rope_tables(seq, dim):
    inv = 1.0 / (ROPE_THETA ** (jnp.arange(0, dim, 2, dtype=F32) / dim))
    ang = jnp.arange(seq, dtype=F32)[:, None] * inv[None, :]
    return jnp.cos(ang), jnp.sin(ang)


def _pack_table(tab):
    n = tab.shape[1] // 2
    b = lax.bitcast_convert_type(tab.astype(BF16), jnp.uint16).astype(jnp.uint32)
    return lax.bitcast_convert_type((b[:, :n] << 16) | b[:, n:], jnp.int32)


def _permute_in_columns(w_in):
    return jnp.concatenate([w_in[:, 3840:], w_in[:, :3840]], axis=1)


def kernel(x, p, norm_mix, w_in, ret_decay, ret_norm, sgu_ln_g, sgu_ln_b, sgu_w, sgu_b, att_q_norm, att_k_norm, att_sink, w_proj_ret, w_proj_sgu, w_proj_att, w_out, norm_ffn, peer_wq, peer_keys, peer_u, peer_v, norm_ple, ple_gate, ple_proj):
    batch, seq, d = x.shape
    depth = w_in.shape[0]
    groups = PIPELINE_GROUPS if batch % PIPELINE_GROUPS == 0 else 1
    gb = batch // groups
    t = gb * seq
    ts = min(512, seq)
    tm = min(512, t)
    cos_r, sin_r = _rope_tables(seq, RET_DK)
    cos_r2 = jnp.concatenate([cos_r, cos_r], axis=1)
    sin_r2 = jnp.concatenate([-sin_r, sin_r], axis=1)
    cos_a, sin_a = _rope_tables(seq, ATT_DH)
    hs = [x[g * gb:(g + 1) * gb].reshape(t, d) for g in range(groups)]
    for i in range(depth):
        w_in_bf = _permute_in_columns(w_in[i]).astype(BF16)
        w_r, w_s, w_a = w_proj_ret[i].astype(BF16), w_proj_sgu[i].astype(BF16), w_proj_att[i].astype(BF16)
        w_o, w_q, keys = w_out[i].astype(BF16), peer_wq[i].astype(BF16), peer_keys[i].astype(BF16)
        w_g, w_p = ple_gate[i].astype(BF16), ple_proj[i].astype(BF16)
        tab_u, tab_v = _pack_table(peer_u[i]), _pack_table(peer_v[i])
        for g in range(groups):
            h = hs[g]
            z = _norm_matmul(h, norm_mix[i], w_in_bf, tm, 1152)
            y_r = _retention(z, ret_decay[i], ret_norm[i], cos_r2, sin_r2, gb, seq, ts)
            y_s = _sgu(z, sgu_ln_g[i], sgu_ln_b[i], sgu_w[i], sgu_b[i], ts)
            y_a = _attention(z, att_q_norm[i], att_k_norm[i], att_sink[i], cos_a, sin_a, gb, seq, ts)
            h = _merge(h, z, y_r, y_s, y_a, w_r, w_s, w_a, w_o, tm)
            hn, eid, gate = _route(h, norm_ffn[i], w_q, keys, min(256, t))
            a = _expert_scores(tab_u, eid, hn)
            w = _pick_weights(a, gate, tm)
            y = _expert_mix(tab_v, eid, w)
            p_g = p[i, g * gb:(g + 1) * gb].reshape(t, -1)
            hs[g] = _ple(h, y, norm_ple[i], w_g, p_g, w_p, tm)
    return jnp.concatenate(hs, axis=0).reshape(batch, seq, d)
```

```python
import functools
import math

import jax
import jax.numpy as jnp
from jax import lax
from jax.experimental import pallas as pl
from jax.experimental.pallas import tpu as pltpu
from jax.experimental.pallas import tpu_sc as plsc

F32 = jnp.float32
BF16 = jnp.bfloat16

D_MODEL = 1024
PLE_DIM = 256
CHUNK = 128
EPS = 1e-6
ROPE_THETA = 10000.0
RET_HEADS = 4
RET_DK = 128
SGU_GROUPS = 4
SGU_WIDTH = 512
ATT_HEADS = 8
ATT_KV_HEADS = 2
ATT_DH = 64
ATT_GROUP = ATT_HEADS // ATT_KV_HEADS
NEG_INF = -1e30
PEER_HEADS = 8
PEER_KEYS = 128
PEER_QDIM = 256
PEER_TOPK = 16
PEER_PICKS = PEER_HEADS * PEER_TOPK

Z_GATE_R, Z_GATE_S, Z_GATE_A = 0, 1024, 2048
Z_QR, Z_KR, Z_VR, Z_GR = 3072, 3584, 4096, 4608
Z_US, Z_VS = 5120, 5632
Z_QA, Z_KA, Z_VA = 6144, 6656, 6784
D_IN = 6912

LANE = 128
SC_CORES = 2
SC_SUBCORES = 16
SC_LANES = 16
SC_WORKERS = SC_CORES * SC_SUBCORES
SC_TOKENS = 32
SC_SCORE_RING = (4, 32)
SC_MIX_RING = (2, 64)
SC_MIX_GROUP = 4
SC_WORDS = 128
HI_MASK = -65536
VMEM_LIMIT = 56 * 1024 * 1024
PIPELINE_GROUPS = 4


def _params(*sem):
    return pltpu.CompilerParams(dimension_semantics=sem, vmem_limit_bytes=VMEM_LIMIT)


def _gelu(x):
    return 0.5 * x * (1.0 + lax.erf(x * (1.0 / math.sqrt(2.0))))


def _sigmoid(x):
    return 1.0 / (1.0 + jnp.exp(-x))


def _rms(x, g):
    return x * lax.rsqrt(jnp.mean(x * x, axis=-1, keepdims=True) + EPS) * g


def _bf16_hi_bits(x):
    b = lax.bitcast_convert_type(x, jnp.int32)
    return (b + 0x7FFF + ((b >> 16) & 1)) & HI_MASK


def _pack_halves(x):
    n = x.shape[1] // 2
    return _bf16_hi_bits(x[:, :n]) | lax.shift_right_logical(_bf16_hi_bits(x[:, n:]), 16)


def _pack_twice(x):
    b = _bf16_hi_bits(x)
    return b | lax.shift_right_logical(b, 16)


def _inproj_body(x_ref, g_ref, w_ref, o_ref, xn_ref):
    @pl.when(pl.program_id(1) == 0)
    def _():
        xn_ref[...] = _rms(x_ref[...], g_ref[...]).astype(BF16)

    o_ref[...] = jnp.dot(xn_ref[...], w_ref[...], preferred_element_type=F32)


def _norm_matmul(h, gain, w_bf, tm, tn):
    t, d = h.shape
    n = w_bf.shape[1]
    return pl.pallas_call(
        _inproj_body,
        grid=(t // tm, n // tn),
        in_specs=[pl.BlockSpec((tm, d), lambda i, j: (i, 0)),
                  pl.BlockSpec((1, d), lambda i, j: (0, 0)),
                  pl.BlockSpec((d, tn), lambda i, j: (0, j))],
        out_specs=pl.BlockSpec((tm, tn), lambda i, j: (i, j)),
        out_shape=jax.ShapeDtypeStruct((t, n), F32),
        scratch_shapes=[pltpu.VMEM((tm, d), BF16)],
        compiler_params=_params("parallel", "arbitrary"),
        name="norm_matmul",
    )(h, gain.reshape(1, d), w_bf)


def _rope128(x, cos, sin_signed):
    return x * cos + pltpu.roll(x, 64, 1) * sin_signed


def _ret_bwd_body(q_ref, k_ref, v_ref, cos_ref, sin_ref, qw_ref, kw_ref, cd_ref, o_ref, st_ref, *, nchunk):
    @pl.when(pl.program_id(2) == 0)
    def _():
        st_ref[...] = jnp.zeros_like(st_ref)

    qw = qw_ref[0]
    kw = kw_ref[0]
    cd = cd_ref[0, 0:1, :]
    for c in reversed(range(nchunk)):
        rows = pl.ds(c * CHUNK, CHUNK)
        cos = cos_ref[rows, :]
        sin = sin_ref[rows, :]
        q = _rope128(q_ref[rows, :], cos, sin)
        k = _rope128(k_ref[rows, :], cos, sin) * (RET_DK ** -0.5)
        v = v_ref[rows, :]
        st = st_ref[...]
        o_ref[rows, :] = jnp.dot((q * qw).astype(BF16), st.astype(BF16), preferred_element_type=F32)
        kv = jnp.dot((k * kw).T.astype(BF16), v.astype(BF16), preferred_element_type=F32)
        st_ref[...] = st * cd + kv


def _ret_fwd_body(q_ref, k_ref, v_ref, g_ref, yb_ref, cos_ref, sin_ref, dm_ref, qw_ref, kw_ref, cd_ref,
                  gn_ref, o_ref, st_ref, *, nchunk):
    @pl.when(pl.program_id(2) == 0)
    def _():
        st_ref[...] = jnp.zeros_like(st_ref)

    qw = qw_ref[0]
    kw = kw_ref[0]
    cd = cd_ref[0, 0:1, :]
    dm = dm_ref[0]
    gn = gn_ref[0, 0:1, :]
    for c in range(nchunk):
        rows = pl.ds(c * CHUNK, CHUNK)
        cos = cos_ref[rows, :]
        sin = sin_ref[rows, :]
        q = _rope128(q_ref[rows, :], cos, sin)
        k = _rope128(k_ref[rows, :], cos, sin) * (RET_DK ** -0.5)
        v = v_ref[rows, :].astype(BF16)
        st = st_ref[...]
        s = lax.dot_general(q.astype(BF16), k.astype(BF16), (((1,), (1,)), ((), ())),
                            preferred_element_type=F32) * dm
        y = jnp.dot(s.astype(BF16), v, preferred_element_type=F32)
        y += jnp.dot((q * qw).astype(BF16), st.astype(BF16), preferred_element_type=F32)
        y += yb_ref[rows, :]
        kv = jnp.dot((k * kw).T.astype(BF16), v, preferred_element_type=F32)
        st_ref[...] = st * cd + kv
        y = y * lax.rsqrt(jnp.mean(y * y, axis=-1, keepdims=True) + EPS) * gn
        g = g_ref[rows, :]
        o_ref[rows, :] = (g * _sigmoid(g) * y).astype(o_ref.dtype)


def _retention(z, ret_decay, ret_norm, cos, sin, batch, seq, ts):
    t = z.shape[0]
    nchunk = ts // CHUNK
    nstep = seq // ts
    hd = RET_HEADS
    log_g = jax.nn.log_sigmoid(ret_decay.astype(F32))
    idx = jnp.arange(CHUNK, dtype=F32)
    diff = idx[:, None] - idx[None, :]
    lf = log_g[0][:, None, None]
    lb = log_g[1][:, None, None]
    dmat = jnp.where(diff[None] >= 0, jnp.exp(lf * jnp.maximum(diff, 0.0)[None]),
                     jnp.exp(lb * jnp.maximum(-diff, 0.0)[None]))
    bc = lambda a: jnp.broadcast_to(a[:, :, None], (hd, a.shape[1], LANE))
    qw_f = bc(jnp.exp(log_g[0][:, None] * (idx + 1.0)[None, :]))
    kw_f = bc(jnp.exp(log_g[0][:, None] * (CHUNK - 1 - idx)[None, :]))
    qw_b = bc(jnp.exp(log_g[1][:, None] * (CHUNK - idx)[None, :]))
    kw_b = bc(jnp.exp(log_g[1][:, None] * idx[None, :]))
    cd_f = jnp.broadcast_to(jnp.exp(log_g[0] * CHUNK)[:, None, None], (hd, 8, LANE))
    cd_b = jnp.broadcast_to(jnp.exp(log_g[1] * CHUNK)[:, None, None], (hd, 8, LANE))
    gn = jnp.broadcast_to(ret_norm.astype(F32).reshape(hd, 1, LANE), (hd, 8, LANE))

    def zspec(col0, rev):
        cb = col0 // LANE
        if rev:
            return pl.BlockSpec((ts, LANE), lambda b, h, s: (b * nstep + nstep - 1 - s, cb + h))
        return pl.BlockSpec((ts, LANE), lambda b, h, s: (b * nstep + s, cb + h))

    def tspec(rev):
        if rev:
            return pl.BlockSpec((ts, LANE), lambda b, h, s: (nstep - 1 - s, 0))
        return pl.BlockSpec((ts, LANE), lambda b, h, s: (s, 0))

    hspec = lambda r: pl.BlockSpec((1, r, LANE), lambda b, h, s: (h, 0, 0))

    yb = pl.pallas_call(
        functools.partial(_ret_bwd_body, nchunk=nchunk),
        grid=(batch, hd, nstep),
        in_specs=[zspec(Z_QR, True), zspec(Z_KR, True), zspec(Z_VR, True), tspec(True), tspec(True),
                  hspec(CHUNK), hspec(CHUNK), hspec(8)],
        out_specs=pl.BlockSpec((ts, LANE), lambda b, h, s: (b * nstep + nstep - 1 - s, h)),
        out_shape=jax.ShapeDtypeStruct((t, hd * LANE), F32),
        scratch_shapes=[pltpu.VMEM((RET_DK, LANE), F32)],
        compiler_params=_params("parallel", "parallel", "arbitrary"),
        name="retention_bwd",
    )(z, z, z, cos, sin, qw_b, kw_b, cd_b)

    return pl.pallas_call(
        functools.partial(_ret_fwd_body, nchunk=nchunk),
        grid=(batch, hd, nstep),
        in_specs=[zspec(Z_QR, False), zspec(Z_KR, False), zspec(Z_VR, False), zspec(Z_GR, False),
                  pl.BlockSpec((ts, LANE), lambda b, h, s: (b * nstep + s, h)),
                  tspec(False), tspec(False), hspec(CHUNK), hspec(CHUNK), hspec(CHUNK), hspec(8), hspec(8)],
        out_specs=pl.BlockSpec((ts, LANE), lambda b, h, s: (b * nstep + s, h)),
        out_shape=jax.ShapeDtypeStruct((t, hd * LANE), BF16),
        scratch_shapes=[pltpu.VMEM((RET_DK, LANE), F32)],
        compiler_params=_params("parallel", "parallel", "arbitrary"),
        name="retention_fwd",
    )(z, z, z, z, yb, cos, sin, dmat, qw_f, kw_f, cd_f, gn)


def _sgu_body(u_ref, v_ref, lg_ref, lb_ref, w_ref, b_ref, o_ref, *, nchunk):
    lg = lg_ref[...]
    lb = lb_ref[...]
    for c in range(nchunk):
        rows = pl.ds(c * CHUNK, CHUNK)
        vf = _gelu(v_ref[rows, :])
        mu = jnp.mean(vf, axis=-1, keepdims=True)
        vc = vf - mu
        var = jnp.mean(vc * vc, axis=-1, keepdims=True)
        vn = (vc * lax.rsqrt(var + EPS) * lg + lb).astype(BF16)
        for g in range(SGU_GROUPS):
            cols = slice(g * LANE, (g + 1) * LANE)
            mixed = jnp.dot(w_ref[g], vn[:, cols], preferred_element_type=F32) + b_ref[g]
            uf = _gelu(u_ref[rows, cols])
            o_ref[rows, cols] = (uf * mixed).astype(o_ref.dtype)


def _sgu(z, ln_g, ln_b, w_s, b_s, ts):
    t = z.shape[0]
    bias = jnp.broadcast_to(b_s.astype(F32)[:, :, None], (SGU_GROUPS, CHUNK, LANE))
    return pl.pallas_call(
        functools.partial(_sgu_body, nchunk=ts // CHUNK),
        grid=(t // ts,),
        in_specs=[pl.BlockSpec((ts, SGU_WIDTH), lambda i: (i, Z_US // SGU_WIDTH)),
                  pl.BlockSpec((ts, SGU_WIDTH), lambda i: (i, Z_VS // SGU_WIDTH)),
                  pl.BlockSpec((1, SGU_WIDTH), lambda i: (0, 0)),
                  pl.BlockSpec((1, SGU_WIDTH), lambda i: (0, 0)),
                  pl.BlockSpec((SGU_GROUPS, CHUNK, CHUNK), lambda i: (0, 0, 0)),
                  pl.BlockSpec((SGU_GROUPS, CHUNK, LANE), lambda i: (0, 0, 0))],
        out_specs=pl.BlockSpec((ts, SGU_WIDTH), lambda i: (i, 0)),
        out_shape=jax.ShapeDtypeStruct((t, SGU_WIDTH), BF16),
        compiler_params=_params("parallel"),
        name="spatial_gating",
    )(z, z, ln_g.reshape(1, -1).astype(F32), ln_b.reshape(1, -1).astype(F32), w_s.astype(BF16), bias)


def _pair_norm_rope(x, gain, cos, sin_up, sin_dn, low):
    sq = x * x
    lo = jnp.sum(jnp.where(low, sq, 0.0), axis=-1, keepdims=True)
    hi = jnp.sum(sq, axis=-1, keepdims=True) - lo
    ms = jnp.where(low, lo, hi) * (1.0 / ATT_DH)
    xn = x * lax.rsqrt(ms + EPS) * gain
    return xn * cos + pltpu.roll(xn, LANE - 32, 1) * sin_up + pltpu.roll(xn, 32, 1) * sin_dn


def _attn_body(sink_ref, q_ref, kp_ref, k_ref, kn_ref, vp_ref, v_ref, vn_ref,
               cq_ref, suq_ref, sdq_ref, ckp_ref, sukp_ref, sdkp_ref, ckn_ref, sukn_ref, sdkn_ref,
               qg_ref, kg_ref, o_ref, *, nchunk, nstep):
    s_id = pl.program_id(1)
    ts = nchunk * CHUNK
    lane = lax.broadcasted_iota(jnp.int32, (1, LANE), 1)
    low = lane < ATT_DH
    kg = kg_ref[...]
    qg = qg_ref[...]
    k_ext = jnp.concatenate([
        _pair_norm_rope(kp_ref[...], kg, ckp_ref[...], sukp_ref[...], sdkp_ref[...], low),
        _pair_norm_rope(k_ref[...], kg, cq_ref[...], suq_ref[...], sdq_ref[...], low),
        _pair_norm_rope(kn_ref[...], kg, ckn_ref[...], sukn_ref[...], sdkn_ref[...], low)], axis=0)
    v_ext = jnp.concatenate([vp_ref[...], v_ref[...], vn_ref[...]], axis=0).astype(BF16)
    k_lo = jnp.where(low, k_ext, 0.0).astype(BF16)
    k_hi = jnp.where(low, 0.0, k_ext).astype(BF16)
    qi = lax.broadcasted_iota(jnp.int32, (CHUNK, 3 * CHUNK), 0)
    kj = lax.broadcasted_iota(jnp.int32, (CHUNK, 3 * CHUNK), 1)
    band = jnp.abs(qi + CHUNK - kj) <= CHUNK
    for c in range(nchunk):
        rows = pl.ds(c * CHUNK, CHUNK)
        first = jnp.logical_and(s_id == 0, c == 0)
        last = jnp.logical_and(s_id == nstep - 1, c == nchunk - 1)
        valid = band
        if c == 0:
            valid = jnp.logical_and(valid, jnp.logical_or(kj >= CHUNK, jnp.logical_not(first)))
        if c == nchunk - 1:
            valid = jnp.logical_and(valid, jnp.logical_or(kj < 2 * CHUNK, jnp.logical_not(last)))
        kc_lo = k_lo[c * CHUNK:(c + 3) * CHUNK]
        kc_hi = k_hi[c * CHUNK:(c + 3) * CHUNK]
        vc = v_ext[c * CHUNK:(c + 3) * CHUNK]
        cos = cq_ref[rows, :]
        su = suq_ref[rows, :]
        sd = sdq_ref[rows, :]
        for pair in range(ATT_HEADS // 2):
            cols = slice(pair * LANE, (pair + 1) * LANE)
            qp = _pair_norm_rope(q_ref[rows, cols], qg, cos, su, sd, low) * (ATT_DH ** -0.5)
            kv_head = (2 * pair) // ATT_GROUP
            outs = []
            for half in range(2):
                head = 2 * pair + half
                qh = qp if half == kv_head else pltpu.roll(qp, ATT_DH, 1)
                if kv_head == 0:
                    qh = jnp.where(low, qh, 0.0)
                    kc = kc_lo
                else:
                    qh = jnp.where(low, 0.0, qh)
                    kc = kc_hi
                s = lax.dot_general(qh.astype(BF16), kc, (((1,), (1,)), ((), ())), preferred_element_type=F32)
                s = jnp.where(valid, s, NEG_INF)
                sk = sink_ref[head]
                m = jnp.maximum(jnp.max(s, axis=-1, keepdims=True), sk)
                e = jnp.exp(s - m)
                den = jnp.sum(e, axis=-1, keepdims=True) + jnp.exp(sk - m)
                o = jnp.dot(e.astype(BF16), vc, preferred_element_type=F32) / den
                outs.append(o if half == kv_head else pltpu.roll(o, ATT_DH, 1))
            o_ref[rows, cols] = jnp.where(low, outs[0], outs[1]).astype(o_ref.dtype)


def _attention(z, q_gain, k_gain, sink, cos, sin, batch, seq, ts):
    t = z.shape[0]
    nchunk = ts // CHUNK
    nstep = seq // ts
    nblk = seq // CHUNK
    cos2 = jnp.tile(jnp.concatenate([cos, cos], axis=1), (1, 2))
    zero = jnp.zeros_like(sin)
    sin_up = jnp.tile(jnp.concatenate([-sin, zero], axis=1), (1, 2))
    sin_dn = jnp.tile(jnp.concatenate([zero, sin], axis=1), (1, 2))
    qg = jnp.tile(q_gain.astype(F32), 2).reshape(1, LANE)
    kg = jnp.tile(k_gain.astype(F32), 2).reshape(1, LANE)

    kcb, vcb = Z_KA // LANE, Z_VA // LANE
    prev_blk = lambda s: jnp.maximum(s * nchunk - 1, 0)
    next_blk = lambda s: jnp.minimum((s + 1) * nchunk, nblk - 1)
    main = lambda cb: pl.BlockSpec((ts, LANE), lambda b, s: (b * nstep + s, cb))
    prev = lambda cb: pl.BlockSpec((CHUNK, LANE), lambda b, s: (b * nblk + prev_blk(s), cb))
    nxt = lambda cb: pl.BlockSpec((CHUNK, LANE), lambda b, s: (b * nblk + next_blk(s), cb))
    tmain = pl.BlockSpec((ts, LANE), lambda b, s: (s, 0))
    tprev = pl.BlockSpec((CHUNK, LANE), lambda b, s: (prev_blk(s), 0))
    tnext = pl.BlockSpec((CHUNK, LANE), lambda b, s: (next_blk(s), 0))
    one = pl.BlockSpec((1, LANE), lambda b, s: (0, 0))
    return pl.pallas_call(
        functools.partial(_attn_body, nchunk=nchunk, nstep=nstep),
        grid=(batch, nstep),
        in_specs=[pl.BlockSpec(memory_space=pltpu.SMEM),
                  pl.BlockSpec((ts, ATT_HEADS * ATT_DH), lambda b, s: (b * nstep + s, Z_QA // 512)),
                  prev(kcb), main(kcb), nxt(kcb), prev(vcb), main(vcb), nxt(vcb),
                  tmain, tmain, tmain, tprev, tprev, tprev, tnext, tnext, tnext, one, one],
        out_specs=pl.BlockSpec((ts, ATT_HEADS * ATT_DH), lambda b, s: (b * nstep + s, 0)),
        out_shape=jax.ShapeDtypeStruct((t, ATT_HEADS * ATT_DH), BF16),
        compiler_params=_params("parallel", "arbitrary"),
        name="window_attention",
    )(sink.astype(F32), z, z, z, z, z, z, z,
      cos2, sin_up, sin_dn, cos2, sin_up, sin_dn, cos2, sin_up, sin_dn, qg, kg)


def _merge_body(h_ref, gr_ref, gs_ref, ga_ref, yr_ref, ys_ref, ya_ref, wr_ref, ws_ref, wa_ref, wo_ref, o_ref):
    m = _sigmoid(gr_ref[...]) * jnp.dot(yr_ref[...], wr_ref[...], preferred_element_type=F32)
    m += _sigmoid(gs_ref[...]) * jnp.dot(ys_ref[...], ws_ref[...], preferred_element_type=F32)
    m += _sigmoid(ga_ref[...]) * jnp.dot(ya_ref[...], wa_ref[...], preferred_element_type=F32)
    o_ref[...] = h_ref[...] + jnp.dot(m.astype(BF16), wo_ref[...], preferred_element_type=F32)


def _merge(h, z, y_r, y_s, y_a, w_r, w_s, w_a, w_o, tm):
    t, d = h.shape
    row = lambda w: pl.BlockSpec((tm, w), lambda i: (i, 0))
    gate = lambda col0: pl.BlockSpec((tm, d), lambda i: (i, col0 // d))
    full = lambda a: pl.BlockSpec(a.shape, lambda i: (0, 0))
    return pl.pallas_call(
        _merge_body,
        grid=(t // tm,),
        in_specs=[row(d), gate(Z_GATE_R), gate(Z_GATE_S), gate(Z_GATE_A), row(512), row(512), row(512),
                  full(w_r), full(w_s), full(w_a), full(w_o)],
        out_specs=row(d),
        out_shape=jax.ShapeDtypeStruct((t, d), F32),
        compiler_params=_params("parallel"),
        name="branch_merge",
    )(h, z, z, z, y_r, y_s, y_a, w_r, w_s, w_a, w_o)


ID_NONE = 1 << 20


def _top16(vals, ids):
    out_v, out_i = [], []
    for _ in range(PEER_TOPK):
        m = jnp.max(vals, axis=0, keepdims=True)
        idx = jnp.min(jnp.where(vals == m, ids, ID_NONE), axis=0, keepdims=True)
        out_v.append(m)
        out_i.append(idx)
        vals = jnp.where(ids == idx, -jnp.inf, vals)
    return out_v, out_i


def _pair_blocks(first, second):
    rows1, stack1 = first
    rows2, stack2 = second
    blocks = [(stack1, rows2[0])]
    blocks += [(stack1[0:8], rows2[b]) for b in range(1, 8)]
    blocks += [(rows1[0], stack2[8:16])]
    return blocks


def _pair_ids(tm):
    a8 = lax.broadcasted_iota(jnp.int32, (8, tm), 0)
    a16 = lax.broadcasted_iota(jnp.int32, (PEER_TOPK, tm), 0)
    blocks = [a16 * PEER_TOPK]
    blocks += [jnp.where(a8 < PEER_TOPK // (b + 1), a8 * PEER_TOPK + b, ID_NONE) for b in range(1, 8)]
    blocks += [a8 + 8]
    return jnp.concatenate(blocks, axis=0)


def _route_body(h_ref, g_ref, wq_ref, keys_ref, hn_ref, eid_ref, gate_ref):
    hn = _rms(h_ref[...], g_ref[...])
    hn_ref[...] = _pack_halves(hn)
    q = jnp.dot(hn.astype(BF16), wq_ref[...], preferred_element_type=F32).astype(BF16)
    tm = q.shape[0]
    rows = lax.broadcasted_iota(jnp.int32, (PEER_KEYS, tm), 0)
    pair_ids = _pair_ids(tm)
    pair_ok = pair_ids != ID_NONE
    half = PEER_QDIM // 2
    eids, gates = [], []
    for hd in range(PEER_HEADS):
        sub = []
        for p in range(2):
            qs = q[:, (2 * hd + p) * half:(2 * hd + p + 1) * half]
            s = lax.dot_general(keys_ref[hd, p], qs, (((1,), (1,)), ((), ())), preferred_element_type=F32)
            sub.append(_top16(s, rows))
        (s1, i1), (s2, i2) = sub
        scores = _pair_blocks((s1, jnp.concatenate(s1, axis=0)), (s2, jnp.concatenate(s2, axis=0)))
        experts = _pair_blocks((i1, jnp.concatenate(i1, axis=0)), (i2, jnp.concatenate(i2, axis=0)))
        cand_s = jnp.where(pair_ok, jnp.concatenate([a + b for a, b in scores], axis=0), -jnp.inf)
        cand_e = jnp.concatenate([a * PEER_KEYS + b for a, b in experts], axis=0)
        top_s, sel = _top16(cand_s, pair_ids)
        top_e = [jnp.sum(jnp.where(pair_ids == i, cand_e, 0), axis=0, keepdims=True) for i in sel]
        ts_ = jnp.concatenate(top_s, axis=0)
        e = jnp.exp(ts_ - top_s[0])
        gates.append(e / jnp.sum(e, axis=0, keepdims=True))
        eids.append(jnp.concatenate(top_e, axis=0))
    eid_ref[...] = jnp.concatenate(eids, axis=0).T
    gate_ref[...] = jnp.concatenate(gates, axis=0).T


def _route(h, gain, wq_bf, keys_bf, tm):
    t, d = h.shape
    return pl.pallas_call(
        _route_body,
        grid=(t // tm,),
        in_specs=[pl.BlockSpec((tm, d), lambda i: (i, 0)),
                  pl.BlockSpec((1, d), lambda i: (0, 0)),
                  pl.BlockSpec(wq_bf.shape, lambda i: (0, 0)),
                  pl.BlockSpec(keys_bf.shape, lambda i: (0, 0, 0, 0))],
        out_specs=[pl.BlockSpec((tm, d // 2), lambda i: (i, 0)),
                   pl.BlockSpec((tm, PEER_PICKS), lambda i: (i, 0)),
                   pl.BlockSpec((tm, PEER_PICKS), lambda i: (i, 0))],
        out_shape=[jax.ShapeDtypeStruct((t, d // 2), jnp.int32),
                   jax.ShapeDtypeStruct((t, PEER_PICKS), jnp.int32),
                   jax.ShapeDtypeStruct((t, PEER_PICKS), F32)],
        compiler_params=_params("parallel"),
        name="peer_route",
    )(h, gain.reshape(1, d), wq_bf, keys_bf)


def _sc_mesh():
    return plsc.VectorSubcoreMesh(core_axis_name="core", subcore_axis_name="subcore")


def _sc_worker():
    return lax.axis_index("subcore") * SC_CORES + lax.axis_index("core")


def _sc_row_pipeline(tab_hbm, idx_v, buf, sems, compute):
    nslot, nrow = buf.shape[0], buf.shape[1]
    nsub = PEER_PICKS // nrow
    ng = SC_TOKENS * nsub

    def gather(g, slot):
        rows = idx_v.at[g // nsub, pl.ds((g % nsub) * nrow, nrow)]
        return pltpu.make_async_copy(tab_hbm.at[rows], buf.at[slot], sems.at[slot])

    for b in range(nslot - 1):
        gather(b, b).start()

    @pl.loop(0, ng, step=nslot)
    def _(g):
        for b in range(nslot):
            ahead = g + b + nslot - 1

            @pl.when(ahead < ng)
            def _():
                gather(ahead, (b + nslot - 1) % nslot).start()

            gather(g + b, b).wait()
            compute(g + b, b)


def _sc_split(words):
    return (lax.bitcast_convert_type(words & HI_MASK, F32), lax.bitcast_convert_type(words << 16, F32))


def _expert_scores(table, eid, hn):
    t, words = hn.shape
    per_worker = t // SC_WORKERS
    nslot, nrow = SC_SCORE_RING
    nsub = PEER_PICKS // nrow
    ln = SC_LANES

    @functools.partial(
        pl.kernel, mesh=_sc_mesh(),
        out_type=jax.ShapeDtypeStruct((t, PEER_PICKS), F32),
        scratch_types=[pltpu.VMEM((SC_TOKENS, PEER_PICKS), jnp.int32),
                       pltpu.VMEM((SC_TOKENS, words), jnp.int32),
                       pltpu.VMEM((SC_TOKENS, PEER_PICKS), F32),
                       pltpu.VMEM((nslot, nrow, words), jnp.int32),
                       pltpu.SemaphoreType.DMA((nslot,))],
        compiler_params=pltpu.CompilerParams(needs_layout_passes=False),
        name="peer_expert_scores")
    def run(tab_hbm, eid_hbm, hn_hbm, out_hbm, idx_v, x_v, a_v, buf, sems):
        lane = lax.iota(jnp.int32, ln)

        def compute(g, slot):
            tok = g // nsub
            sub = g % nsub
            for grp in range(nrow // ln):
                def body(c, accs):
                    c0 = pl.ds(2 * c * ln, ln)
                    c1 = pl.ds((2 * c + 1) * ln, ln)
                    x0 = plsc.bitcast(x_v[tok, c0], BF16)
                    x1 = plsc.bitcast(x_v[tok, c1], BF16)
                    out = []
                    for r in range(ln):
                        u0 = plsc.bitcast(buf[slot, grp * ln + r, c0], BF16)
                        u1 = plsc.bitcast(buf[slot, grp * ln + r, c1], BF16)
                        hi, lo = _sc_split(plsc.bitcast(u0 * x0 + u1 * x1, jnp.int32))
                        out.append(accs[r] + hi + lo)
                    return tuple(out)

                accs = lax.fori_loop(0, words // (2 * ln), body, tuple(jnp.zeros((ln,), F32) for _ in range(ln)))
                res = jnp.zeros((ln,), F32)
                for r in range(ln):
                    res = jnp.where(lane == r, jnp.sum(accs[r]), res)
                a_v[tok, pl.ds(sub * nrow + grp * ln, ln)] = res

        @pl.loop(0, per_worker // SC_TOKENS)
        def _(blk):
            tok0 = _sc_worker() * per_worker + blk * SC_TOKENS
            pltpu.sync_copy(eid_hbm.at[pl.ds(tok0, SC_TOKENS)], idx_v)
            pltpu.sync_copy(hn_hbm.at[pl.ds(tok0, SC_TOKENS)], x_v)
            _sc_row_pipeline(tab_hbm, idx_v, buf, sems, compute)
            pltpu.sync_copy(a_v, out_hbm.at[pl.ds(tok0, SC_TOKENS)])

    return run(table, eid, hn)


def _expert_mix(table, eid, w):
    t = eid.shape[0]
    words = table.shape[1]
    d = 2 * words
    per_worker = t // SC_WORKERS
    nslot, nrow = SC_MIX_RING
    nsub = PEER_PICKS // nrow
    ln = SC_LANES
    nvec = SC_WORDS // ln

    @functools.partial(
        pl.kernel, mesh=_sc_mesh(),
        out_type=jax.ShapeDtypeStruct((t, d), F32),
        scratch_types=[pltpu.VMEM((SC_TOKENS, PEER_PICKS), jnp.int32),
                       pltpu.VMEM((SC_TOKENS, PEER_PICKS), jnp.int32),
                       pltpu.VMEM((SC_TOKENS, d), F32),
                       pltpu.VMEM((nslot, nrow, words), jnp.int32),
                       pltpu.SemaphoreType.DMA((nslot,))],
        compiler_params=pltpu.CompilerParams(needs_layout_passes=False),
        name="peer_expert_mix")
    def run(tab_hbm, eid_hbm, w_hbm, out_hbm, idx_v, w_v, y_v, buf, sems):
        zero = jnp.zeros((ln,), F32)

        def compute(g, slot):
            tok = g // nsub
            sub = g % nsub
            tokv = jnp.full((ln,), tok, jnp.int32)

            def weight(r):
                pick = jnp.full((ln,), sub * nrow + r, jnp.int32)
                return plsc.bitcast(plsc.load_gather(w_v, [tokv, pick]), BF16)

            for cc in range(words // SC_WORDS):
                def body(rg, accs):
                    r0 = SC_MIX_GROUP * rg
                    ws = [weight(r0 + j) for j in range(SC_MIX_GROUP)]
                    out = list(accs)
                    for k in range(nvec):
                        cols = pl.ds(cc * SC_WORDS + k * ln, ln)
                        prod = ws[0] * plsc.bitcast(buf[slot, r0, cols], BF16)
                        for j in range(1, SC_MIX_GROUP):
                            prod = prod + ws[j] * plsc.bitcast(buf[slot, r0 + j, cols], BF16)
                        hi, lo = _sc_split(plsc.bitcast(prod, jnp.int32))
                        out[k] = accs[k] + hi
                        out[nvec + k] = accs[nvec + k] + lo
                    return tuple(out)

                lo_cols = [pl.ds(cc * SC_WORDS + k * ln, ln) for k in range(nvec)]
                hi_cols = [pl.ds(words + cc * SC_WORDS + k * ln, ln) for k in range(nvec)]
                init = tuple(y_v[tok, c] for c in lo_cols + hi_cols)
                accs = lax.fori_loop(0, nrow // SC_MIX_GROUP, body, init)
                for c, acc in zip(lo_cols + hi_cols, accs):
                    y_v[tok, c] = acc

        @pl.loop(0, per_worker // SC_TOKENS)
        def _(blk):
            tok0 = _sc_worker() * per_worker + blk * SC_TOKENS
            pltpu.sync_copy(eid_hbm.at[pl.ds(tok0, SC_TOKENS)], idx_v)
            pltpu.sync_copy(w_hbm.at[pl.ds(tok0, SC_TOKENS)], w_v)

            @pl.loop(0, SC_TOKENS)
            def _(tok):
                @pl.loop(0, d // ln)
                def _(c):
                    y_v[tok, pl.ds(c * ln, ln)] = zero

            _sc_row_pipeline(tab_hbm, idx_v, buf, sems, compute)
            pltpu.sync_copy(y_v, out_hbm.at[pl.ds(tok0, SC_TOKENS)])

    return run(table, eid, w)


def _pick_weights_body(a_ref, g_ref, o_ref):
    o_ref[...] = _pack_twice(g_ref[...] * _gelu(a_ref[...]))


def _pick_weights(a, gate, tm):
    t = a.shape[0]
    spec = pl.BlockSpec((tm, PEER_PICKS), lambda i: (i, 0))
    return pl.pallas_call(
        _pick_weights_body,
        grid=(t // tm,),
        in_specs=[spec, spec],
        out_specs=spec,
        out_shape=jax.ShapeDtypeStruct(a.shape, jnp.int32),
        compiler_params=_params("parallel"),
        name="peer_pick_weights",
    )(a, gate)


def _ple_body(h_ref, y_ref, g_ref, wg_ref, p_ref, wp_ref, o_ref):
    h = h_ref[...] + y_ref[...]
    hn = _rms(h, g_ref[...]).astype(BF16)
    gate = _sigmoid(jnp.dot(hn, wg_ref[...], preferred_element_type=F32))
    emb = jnp.dot(p_ref[...].astype(BF16), wp_ref[...], preferred_element_type=F32)
    o_ref[...] = h + gate * emb


def _ple(h, y, gain, wg_bf, p, wp_bf, tm):
    t, d = h.shape
    return pl.pallas_call(
        _ple_body,
        grid=(t // tm,),
        in_specs=[pl.BlockSpec((tm, d), lambda i: (i, 0)),
                  pl.BlockSpec((tm, d), lambda i: (i, 0)),
                  pl.BlockSpec((1, d), lambda i: (0, 0)),
                  pl.BlockSpec(wg_bf.shape, lambda i: (0, 0)),
                  pl.BlockSpec((tm, p.shape[1]), lambda i: (i, 0)),
                  pl.BlockSpec(wp_bf.shape, lambda i: (0, 0))],
        out_specs=pl.BlockSpec((tm, d), lambda i: (i, 0)),
        out_shape=jax.ShapeDtypeStruct((t, d), F32),
        compiler_params=_params("parallel"),
        name="layer_embedding",
    )(h, y, gain.reshape(1, d), wg_bf, p, wp_bf)


def _rope_tables(seq, dim):
    inv = 1.0 / (ROPE_THETA ** (jnp.arange(0, dim, 2, dtype=F32) / dim))
    ang = jnp.arange(seq, dtype=F32)[:, None] * inv[None, :]
    return jnp.cos(ang), jnp.sin(ang)


def _pack_table(tab):
    n = tab.shape[1] // 2
    b = lax.bitcast_convert_type(tab.astype(BF16), jnp.uint16).astype(jnp.uint32)
    return lax.bitcast_convert_type((b[:, :n] << 16) | b[:, n:], jnp.int32)


def _permute_in_columns(w_in):
    return jnp.concatenate([w_in[:, 3840:], w_in[:, :3840]], axis=1)


def kernel(x, p, norm_mix, w_in, ret_decay, ret_norm, sgu_ln_g, sgu_ln_b, sgu_w, sgu_b, att_q_norm, att_k_norm, att_sink, w_proj_ret, w_proj_sgu, w_proj_att, w_out, norm_ffn, peer_wq, peer_keys, peer_u, peer_v, norm_ple, ple_gate, ple_proj):
    batch, seq, d = x.shape
    depth = w_in.shape[0]
    groups = PIPELINE_GROUPS if batch % PIPELINE_GROUPS == 0 else 1
    gb = batch // groups
    t = gb * seq
    ts = min(512, seq)
    tm = min(512, t)
    cos_r, sin_r = _rope_tables(seq, RET_DK)
    cos_r2 = jnp.concatenate([cos_r, cos_r], axis=1)
    sin_r2 = jnp.concatenate([-sin_r, sin_r], axis=1)
    cos_a, sin_a = _rope_tables(seq, ATT_DH)
    hs = [x[g * gb:(g + 1) * gb].reshape(t, d) for g in range(groups)]
    for i in range(depth):
        w_in_bf = _permute_in_columns(w_in[i]).astype(BF16)
        w_r, w_s, w_a = w_proj_ret[i].astype(BF16), w_proj_sgu[i].astype(BF16), w_proj_att[i].astype(BF16)
        w_o, w_q, keys = w_out[i].astype(BF16), peer_wq[i].astype(BF16), peer_keys[i].astype(BF16)
        w_g, w_p = ple_gate[i].astype(BF16), ple_proj[i].astype(BF16)
        tab_u, tab_v = _pack_table(peer_u[i]), _pack_table(peer_v[i])
        for g in range(groups):
            h = hs[g]
            z = _norm_matmul(h, norm_mix[i], w_in_bf, tm, 1152)
            y_r = _retention(z, ret_decay[i], ret_norm[i], cos_r2, sin_r2, gb, seq, ts)
            y_s = _sgu(z, sgu_ln_g[i], sgu_ln_b[i], sgu_w[i], sgu_b[i], ts)
            y_a = _attention(z, att_q_norm[i], att_k_norm[i], att_sink[i], cos_a, sin_a, gb, seq, ts)
            h = _merge(h, z, y_r, y_s, y_a, w_r, w_s, w_a, w_o, tm)
            hn, eid, gate = _route(h, norm_ffn[i], w_q, keys, min(256, t))
            a = _expert_scores(tab_u, eid, hn)
            w = _pick_weights(a, gate, tm)
            y = _expert_mix(tab_v, eid, w)
            p_g = p[i, g * gb:(g + 1) * gb].reshape(t, -1)
            hs[g] = _ple(h, y, norm_ple[i], w_g, p_g, w_p, tm)
    return jnp.concatenate(hs, axis=0).reshape(batch, seq, d)
```

```python
import functools
import math

import jax
import jax.numpy as jnp
from jax import lax
from jax.experimental import pallas as pl
from jax.experimental.pallas import tpu as pltpu
from jax.experimental.pallas import tpu_sc as plsc

F32 = jnp.float32
BF16 = jnp.bfloat16

D_MODEL = 1024
PLE_DIM = 256
CHUNK = 128
EPS = 1e-6
ROPE_THETA = 10000.0
RET_HEADS = 4
RET_DK = 128
SGU_GROUPS = 4
SGU_WIDTH = 512
ATT_HEADS = 8
ATT_KV_HEADS = 2
ATT_DH = 64
ATT_GROUP = ATT_HEADS // ATT_KV_HEADS
NEG_INF = -1e30
PEER_HEADS = 8
PEER_KEYS = 128
PEER_QDIM = 256
PEER_TOPK = 16
PEER_PICKS = PEER_HEADS * PEER_TOPK

Z_GATE_R, Z_GATE_S, Z_GATE_A = 0, 1024, 2048
Z_QR, Z_KR, Z_VR, Z_GR = 3072, 3584, 4096, 4608
Z_US, Z_VS = 5120, 5632
Z_QA, Z_KA, Z_VA = 6144, 6656, 6784
D_IN = 6912

LANE = 128
SC_CORES = 2
SC_SUBCORES = 16
SC_LANES = 16
SC_WORKERS = SC_CORES * SC_SUBCORES
SC_TOKENS = 32
SC_SCORE_RING = (4, 32)
SC_MIX_RING = (2, 64)
SC_MIX_GROUP = 4
SC_WORDS = 128
HI_MASK = -65536
VMEM_LIMIT = 56 * 1024 * 1024
PIPELINE_GROUPS = 4


def _params(*sem):
    return pltpu.CompilerParams(dimension_semantics=sem, vmem_limit_bytes=VMEM_LIMIT)


def _gelu(x):
    return 0.5 * x * (1.0 + lax.erf(x * (1.0 / math.sqrt(2.0))))


def _sigmoid(x):
    return 1.0 / (1.0 + jnp.exp(-x))


def _rms(x, g):
    return x * lax.rsqrt(jnp.mean(x * x, axis=-1, keepdims=True) + EPS) * g


def _bf16_hi_bits(x):
    b = lax.bitcast_convert_type(x, jnp.int32)
    return (b + 0x7FFF + ((b >> 16) & 1)) & HI_MASK


def _pack_halves(x):
    n = x.shape[1] // 2
    return _bf16_hi_bits(x[:, :n]) | lax.shift_right_logical(_bf16_hi_bits(x[:, n:]), 16)


def _pack_twice(x):
    b = _bf16_hi_bits(x)
    return b | lax.shift_right_logical(b, 16)


def _inproj_body(x_ref, g_ref, w_ref, o_ref, xn_ref):
    @pl.when(pl.program_id(1) == 0)
    def _():
        xn_ref[...] = _rms(x_ref[...], g_ref[...]).astype(BF16)

    o_ref[...] = jnp.dot(xn_ref[...], w_ref[...], preferred_element_type=F32)


def _norm_matmul(h, gain, w_bf, tm, tn):
    t, d = h.shape
    n = w_bf.shape[1]
    return pl.pallas_call(
        _inproj_body,
        grid=(t // tm, n // tn),
        in_specs=[pl.BlockSpec((tm, d), lambda i, j: (i, 0)),
                  pl.BlockSpec((1, d), lambda i, j: (0, 0)),
                  pl.BlockSpec((d, tn), lambda i, j: (0, j))],
        out_specs=pl.BlockSpec((tm, tn), lambda i, j: (i, j)),
        out_shape=jax.ShapeDtypeStruct((t, n), F32),
        scratch_shapes=[pltpu.VMEM((tm, d), BF16)],
        compiler_params=_params("parallel", "arbitrary"),
        name="norm_matmul",
    )(h, gain.reshape(1, d), w_bf)


def _rope128(x, cos, sin_signed):
    return x * cos + pltpu.roll(x, 64, 1) * sin_signed


def _ret_bwd_body(q_ref, k_ref, v_ref, cos_ref, sin_ref, qw_ref, kw_ref, cd_ref, o_ref, st_ref, *, nchunk):
    @pl.when(pl.program_id(2) == 0)
    def _():
        st_ref[...] = jnp.zeros_like(st_ref)

    qw = qw_ref[0]
    kw = kw_ref[0]
    cd = cd_ref[0, 0:1, :]
    for c in reversed(range(nchunk)):
        rows = pl.ds(c * CHUNK, CHUNK)
        cos = cos_ref[rows, :]
        sin = sin_ref[rows, :]
        q = _rope128(q_ref[rows, :], cos, sin)
        k = _rope128(k_ref[rows, :], cos, sin) * (RET_DK ** -0.5)
        v = v_ref[rows, :]
        st = st_ref[...]
        o_ref[rows, :] = jnp.dot((q * qw).astype(BF16), st.astype(BF16), preferred_element_type=F32)
        kv = jnp.dot((k * kw).T.astype(BF16), v.astype(BF16), preferred_element_type=F32)
        st_ref[...] = st * cd + kv


def _ret_fwd_body(q_ref, k_ref, v_ref, g_ref, yb_ref, cos_ref, sin_ref, dm_ref, qw_ref, kw_ref, cd_ref,
                  gn_ref, o_ref, st_ref, *, nchunk):
    @pl.when(pl.program_id(2) == 0)
    def _():
        st_ref[...] = jnp.zeros_like(st_ref)

    qw = qw_ref[0]
    kw = kw_ref[0]
    cd = cd_ref[0, 0:1, :]
    dm = dm_ref[0]
    gn = gn_ref[0, 0:1, :]
    for c in range(nchunk):
        rows = pl.ds(c * CHUNK, CHUNK)
        cos = cos_ref[rows, :]
        sin = sin_ref[rows, :]
        q = _rope128(q_ref[rows, :], cos, sin)
        k = _rope128(k_ref[rows, :], cos, sin) * (RET_DK ** -0.5)
        v = v_ref[rows, :].astype(BF16)
        st = st_ref[...]
        s = lax.dot_general(q.astype(BF16), k.astype(BF16), (((1,), (1,)), ((), ())),
                            preferred_element_type=F32) * dm
        y = jnp.dot(s.astype(BF16), v, preferred_element_type=F32)
        y += jnp.dot((q * qw).astype(BF16), st.astype(BF16), preferred_element_type=F32)
        y += yb_ref[rows, :]
        kv = jnp.dot((k * kw).T.astype(BF16), v, preferred_element_type=F32)
        st_ref[...] = st * cd + kv
        y = y * lax.rsqrt(jnp.mean(y * y, axis=-1, keepdims=True) + EPS) * gn
        g = g_ref[rows, :]
        o_ref[rows, :] = (g * _sigmoid(g) * y).astype(o_ref.dtype)


def _retention(z, ret_decay, ret_norm, cos, sin, batch, seq, ts):
    t = z.shape[0]
    nchunk = ts // CHUNK
    nstep = seq // ts
    hd = RET_HEADS
    log_g = jax.nn.log_sigmoid(ret_decay.astype(F32))
    idx = jnp.arange(CHUNK, dtype=F32)
    diff = idx[:, None] - idx[None, :]
    lf = log_g[0][:, None, None]
    lb = log_g[1][:, None, None]
    dmat = jnp.where(diff[None] >= 0, jnp.exp(lf * jnp.maximum(diff, 0.0)[None]),
                     jnp.exp(lb * jnp.maximum(-diff, 0.0)[None]))
    bc = lambda a: jnp.broadcast_to(a[:, :, None], (hd, a.shape[1], LANE))
    qw_f = bc(jnp.exp(log_g[0][:, None] * (idx + 1.0)[None, :]))
    kw_f = bc(jnp.exp(log_g[0][:, None] * (CHUNK - 1 - idx)[None, :]))
    qw_b = bc(jnp.exp(log_g[1][:, None] * (CHUNK - idx)[None, :]))
    kw_b = bc(jnp.exp(log_g[1][:, None] * idx[None, :]))
    cd_f = jnp.broadcast_to(jnp.exp(log_g[0] * CHUNK)[:, None, None], (hd, 8, LANE))
    cd_b = jnp.broadcast_to(jnp.exp(log_g[1] * CHUNK)[:, None, None], (hd, 8, LANE))
    gn = jnp.broadcast_to(ret_norm.astype(F32).reshape(hd, 1, LANE), (hd, 8, LANE))

    def zspec(col0, rev):
        cb = col0 // LANE
        if rev:
            return pl.BlockSpec((ts, LANE), lambda b, h, s: (b * nstep + nstep - 1 - s, cb + h))
        return pl.BlockSpec((ts, LANE), lambda b, h, s: (b * nstep + s, cb + h))

    def tspec(rev):
        if rev:
            return pl.BlockSpec((ts, LANE), lambda b, h, s: (nstep - 1 - s, 0))
        return pl.BlockSpec((ts, LANE), lambda b, h, s: (s, 0))

    hspec = lambda r: pl.BlockSpec((1, r, LANE), lambda b, h, s: (h, 0, 0))

    yb = pl.pallas_call(
        functools.partial(_ret_bwd_body, nchunk=nchunk),
        grid=(batch, hd, nstep),
        in_specs=[zspec(Z_QR, True), zspec(Z_KR, True), zspec(Z_VR, True), tspec(True), tspec(True),
                  hspec(CHUNK), hspec(CHUNK), hspec(8)],
        out_specs=pl.BlockSpec((ts, LANE), lambda b, h, s: (b * nstep + nstep - 1 - s, h)),
        out_shape=jax.ShapeDtypeStruct((t, hd * LANE), F32),
        scratch_shapes=[pltpu.VMEM((RET_DK, LANE), F32)],
        compiler_params=_params("parallel", "parallel", "arbitrary"),
        name="retention_bwd",
    )(z, z, z, cos, sin, qw_b, kw_b, cd_b)

    return pl.pallas_call(
        functools.partial(_ret_fwd_body, nchunk=nchunk),
        grid=(batch, hd, nstep),
        in_specs=[zspec(Z_QR, False), zspec(Z_KR, False), zspec(Z_VR, False), zspec(Z_GR, False),
                  pl.BlockSpec((ts, LANE), lambda b, h, s: (b * nstep + s, h)),
                  tspec(False), tspec(False), hspec(CHUNK), hspec(CHUNK), hspec(CHUNK), hspec(8), hspec(8)],
        out_specs=pl.BlockSpec((ts, LANE), lambda b, h, s: (b * nstep + s, h)),
        out_shape=jax.ShapeDtypeStruct((t, hd * LANE), BF16),
        scratch_shapes=[pltpu.VMEM((RET_DK, LANE), F32)],
        compiler_params=_params("parallel", "parallel", "arbitrary"),
        name="retention_fwd",
    )(z, z, z, z, yb, cos, sin, dmat, qw_f, kw_f, cd_f, gn)


def _sgu_body(u_ref, v_ref, lg_ref, lb_ref, w_ref, b_ref, o_ref, *, nchunk):
    lg = lg_ref[...]
    lb = lb_ref[...]
    for c in range(nchunk):
        rows = pl.ds(c * CHUNK, CHUNK)
        vf = _gelu(v_ref[rows, :])
        mu = jnp.mean(vf, axis=-1, keepdims=True)
        vc = vf - mu
        var = jnp.mean(vc * vc, axis=-1, keepdims=True)
        vn = (vc * lax.rsqrt(var + EPS) * lg + lb).astype(BF16)
        for g in range(SGU_GROUPS):
            cols = slice(g * LANE, (g + 1) * LANE)
            mixed = jnp.dot(w_ref[g], vn[:, cols], preferred_element_type=F32) + b_ref[g]
            uf = _gelu(u_ref[rows, cols])
            o_ref[rows, cols] = (uf * mixed).astype(o_ref.dtype)


def _sgu(z, ln_g, ln_b, w_s, b_s, ts):
    t = z.shape[0]
    bias = jnp.broadcast_to(b_s.astype(F32)[:, :, None], (SGU_GROUPS, CHUNK, LANE))
    return pl.pallas_call(
        functools.partial(_sgu_body, nchunk=ts // CHUNK),
        grid=(t // ts,),
        in_specs=[pl.BlockSpec((ts, SGU_WIDTH), lambda i: (i, Z_US // SGU_WIDTH)),
                  pl.BlockSpec((ts, SGU_WIDTH), lambda i: (i, Z_VS // SGU_WIDTH)),
                  pl.BlockSpec((1, SGU_WIDTH), lambda i: (0, 0)),
                  pl.BlockSpec((1, SGU_WIDTH), lambda i: (0, 0)),
                  pl.BlockSpec((SGU_GROUPS, CHUNK, CHUNK), lambda i: (0, 0, 0)),
                  pl.BlockSpec((SGU_GROUPS, CHUNK, LANE), lambda i: (0, 0, 0))],
        out_specs=pl.BlockSpec((ts, SGU_WIDTH), lambda i: (i, 0)),
        out_shape=jax.ShapeDtypeStruct((t, SGU_WIDTH), BF16),
        compiler_params=_params("parallel"),
        name="spatial_gating",
    )(z, z, ln_g.reshape(1, -1).astype(F32), ln_b.reshape(1, -1).astype(F32), w_s.astype(BF16), bias)


def _pair_norm_rope(x, gain, cos, sin_up, sin_dn, low):
    sq = x * x
    lo = jnp.sum(jnp.where(low, sq, 0.0), axis=-1, keepdims=True)
    hi = jnp.sum(sq, axis=-1, keepdims=True) - lo
    ms = jnp.where(low, lo, hi) * (1.0 / ATT_DH)
    xn = x * lax.rsqrt(ms + EPS) * gain
    return xn * cos + pltpu.roll(xn, LANE - 32, 1) * sin_up + pltpu.roll(xn, 32, 1) * sin_dn


def _attn_body(sink_ref, q_ref, kp_ref, k_ref, kn_ref, vp_ref, v_ref, vn_ref,
               cq_ref, suq_ref, sdq_ref, ckp_ref, sukp_ref, sdkp_ref, ckn_ref, sukn_ref, sdkn_ref,
               qg_ref, kg_ref, o_ref, *, nchunk, nstep):
    s_id = pl.program_id(1)
    ts = nchunk * CHUNK
    lane = lax.broadcasted_iota(jnp.int32, (1, LANE), 1)
    low = lane < ATT_DH
    kg = kg_ref[...]
    qg = qg_ref[...]
    k_ext = jnp.concatenate([
        _pair_norm_rope(kp_ref[...], kg, ckp_ref[...], sukp_ref[...], sdkp_ref[...], low),
        _pair_norm_rope(k_ref[...], kg, cq_ref[...], suq_ref[...], sdq_ref[...], low),
        _pair_norm_rope(kn_ref[...], kg, ckn_ref[...], sukn_ref[...], sdkn_ref[...], low)], axis=0)
    v_ext = jnp.concatenate([vp_ref[...], v_ref[...], vn_ref[...]], axis=0).astype(BF16)
    k_lo = jnp.where(low, k_ext, 0.0).astype(BF16)
    k_hi = jnp.where(low, 0.0, k_ext).astype(BF16)
    qi = lax.broadcasted_iota(jnp.int32, (CHUNK, 3 * CHUNK), 0)
    kj = lax.broadcasted_iota(jnp.int32, (CHUNK, 3 * CHUNK), 1)
    band = jnp.abs(qi + CHUNK - kj) <= CHUNK
    for c in range(nchunk):
        rows = pl.ds(c * CHUNK, CHUNK)
        first = jnp.logical_and(s_id == 0, c == 0)
        last = jnp.logical_and(s_id == nstep - 1, c == nchunk - 1)
        valid = band
        if c == 0:
            valid = jnp.logical_and(valid, jnp.logical_or(kj >= CHUNK, jnp.logical_not(first)))
        if c == nchunk - 1:
            valid = jnp.logical_and(valid, jnp.logical_or(kj < 2 * CHUNK, jnp.logical_not(last)))
        kc_lo = k_lo[c * CHUNK:(c + 3) * CHUNK]
        kc_hi = k_hi[c * CHUNK:(c + 3) * CHUNK]
        vc = v_ext[c * CHUNK:(c + 3) * CHUNK]
        cos = cq_ref[rows, :]
        su = suq_ref[rows, :]
        sd = sdq_ref[rows, :]
        for pair in range(ATT_HEADS // 2):
            cols = slice(pair * LANE, (pair + 1) * LANE)
            qp = _pair_norm_rope(q_ref[rows, cols], qg, cos, su, sd, low) * (ATT_DH ** -0.5)
            kv_head = (2 * pair) // ATT_GROUP
            outs = []
            for half in range(2):
                head = 2 * pair + half
                qh = qp if half == kv_head else pltpu.roll(qp, ATT_DH, 1)
                if kv_head == 0:
                    qh = jnp.where(low, qh, 0.0)
                    kc = kc_lo
                else:
                    qh = jnp.where(low, 0.0, qh)
                    kc = kc_hi
                s = lax.dot_general(qh.astype(BF16), kc, (((1,), (1,)), ((), ())), preferred_element_type=F32)
                s = jnp.where(valid, s, NEG_INF)
                sk = sink_ref[head]
                m = jnp.maximum(jnp.max(s, axis=-1, keepdims=True), sk)
                e = jnp.exp(s - m)
                den = jnp.sum(e, axis=-1, keepdims=True) + jnp.exp(sk - m)
                o = jnp.dot(e.astype(BF16), vc, preferred_element_type=F32) / den
                outs.append(o if half == kv_head else pltpu.roll(o, ATT_DH, 1))
            o_ref[rows, cols] = jnp.where(low, outs[0], outs[1]).astype(o_ref.dtype)


def _attention(z, q_gain, k_gain, sink, cos, sin, batch, seq, ts):
    t = z.shape[0]
    nchunk = ts // CHUNK
    nstep = seq // ts
    nblk = seq // CHUNK
    cos2 = jnp.tile(jnp.concatenate([cos, cos], axis=1), (1, 2))
    zero = jnp.zeros_like(sin)
    sin_up = jnp.tile(jnp.concatenate([-sin, zero], axis=1), (1, 2))
    sin_dn = jnp.tile(jnp.concatenate([zero, sin], axis=1), (1, 2))
    qg = jnp.tile(q_gain.astype(F32), 2).reshape(1, LANE)
    kg = jnp.tile(k_gain.astype(F32), 2).reshape(1, LANE)

    kcb, vcb = Z_KA // LANE, Z_VA // LANE
    prev_blk = lambda s: jnp.maximum(s * nchunk - 1, 0)
    next_blk = lambda s: jnp.minimum((s + 1) * nchunk, nblk - 1)
    main = lambda cb: pl.BlockSpec((ts, LANE), lambda b, s: (b * nstep + s, cb))
    prev = lambda cb: pl.BlockSpec((CHUNK, LANE), lambda b, s: (b * nblk + prev_blk(s), cb))
    nxt = lambda cb: pl.BlockSpec((CHUNK, LANE), lambda b, s: (b * nblk + next_blk(s), cb))
    tmain = pl.BlockSpec((ts, LANE), lambda b, s: (s, 0))
    tprev = pl.BlockSpec((CHUNK, LANE), lambda b, s: (prev_blk(s), 0))
    tnext = pl.BlockSpec((CHUNK, LANE), lambda b, s: (next_blk(s), 0))
    one = pl.BlockSpec((1, LANE), lambda b, s: (0, 0))
    return pl.pallas_call(
        functools.partial(_attn_body, nchunk=nchunk, nstep=nstep),
        grid=(batch, nstep),
        in_specs=[pl.BlockSpec(memory_space=pltpu.SMEM),
                  pl.BlockSpec((ts, ATT_HEADS * ATT_DH), lambda b, s: (b * nstep + s, Z_QA // 512)),
                  prev(kcb), main(kcb), nxt(kcb), prev(vcb), main(vcb), nxt(vcb),
                  tmain, tmain, tmain, tprev, tprev, tprev, tnext, tnext, tnext, one, one],
        out_specs=pl.BlockSpec((ts, ATT_HEADS * ATT_DH), lambda b, s: (b * nstep + s, 0)),
        out_shape=jax.ShapeDtypeStruct((t, ATT_HEADS * ATT_DH), BF16),
        compiler_params=_params("parallel", "arbitrary"),
        name="window_attention",
    )(sink.astype(F32), z, z, z, z, z, z, z,
      cos2, sin_up, sin_dn, cos2, sin_up, sin_dn, cos2, sin_up, sin_dn, qg, kg)


def _merge_body(h_ref, gr_ref, gs_ref, ga_ref, yr_ref, ys_ref, ya_ref, wr_ref, ws_ref, wa_ref, wo_ref, o_ref):
    m = _sigmoid(gr_ref[...]) * jnp.dot(yr_ref[...], wr_ref[...], preferred_element_type=F32)
    m += _sigmoid(gs_ref[...]) * jnp.dot(ys_ref[...], ws_ref[...], preferred_element_type=F32)
    m += _sigmoid(ga_ref[...]) * jnp.dot(ya_ref[...], wa_ref[...], preferred_element_type=F32)
    o_ref[...] = h_ref[...] + jnp.dot(m.astype(BF16), wo_ref[...], preferred_element_type=F32)


def _merge(h, z, y_r, y_s, y_a, w_r, w_s, w_a, w_o, tm):
    t, d = h.shape
    row = lambda w: pl.BlockSpec((tm, w), lambda i: (i, 0))
    gate = lambda col0: pl.BlockSpec((tm, d), lambda i: (i, col0 // d))
    full = lambda a: pl.BlockSpec(a.shape, lambda i: (0, 0))
    return pl.pallas_call(
        _merge_body,
        grid=(t // tm,),
        in_specs=[row(d), gate(Z_GATE_R), gate(Z_GATE_S), gate(Z_GATE_A), row(512), row(512), row(512),
                  full(w_r), full(w_s), full(w_a), full(w_o)],
        out_specs=row(d),
        out_shape=jax.ShapeDtypeStruct((t, d), F32),
        compiler_params=_params("parallel"),
        name="branch_merge",
    )(h, z, z, z, y_r, y_s, y_a, w_r, w_s, w_a, w_o)


ID_NONE = 1 << 20


def _top16(vals, ids):
    out_v, out_i = [], []
    for _ in range(PEER_TOPK):
        m = jnp.max(vals, axis=0, keepdims=True)
        idx = jnp.min(jnp.where(vals == m, ids, ID_NONE), axis=0, keepdims=True)
        out_v.append(m)
        out_i.append(idx)
        vals = jnp.where(ids == idx, -jnp.inf, vals)
    return out_v, out_i


def _pair_blocks(first, second):
    rows1, stack1 = first
    rows2, stack2 = second
    blocks = [(stack1, rows2[0])]
    blocks += [(stack1[0:8], rows2[b]) for b in range(1, 8)]
    blocks += [(rows1[0], stack2[8:16])]
    return blocks


def _pair_ids(tm):
    a8 = lax.broadcasted_iota(jnp.int32, (8, tm), 0)
    a16 = lax.broadcasted_iota(jnp.int32, (PEER_TOPK, tm), 0)
    blocks = [a16 * PEER_TOPK]
    blocks += [jnp.where(a8 < PEER_TOPK // (b + 1), a8 * PEER_TOPK + b, ID_NONE) for b in range(1, 8)]
    blocks += [a8 + 8]
    return jnp.concatenate(blocks, axis=0)


def _route_body(h_ref, g_ref, wq_ref, keys_ref, hn_ref, eid_ref, gate_ref):
    hn = _rms(h_ref[...], g_ref[...])
    hn_ref[...] = _pack_halves(hn)
    q = jnp.dot(hn.astype(BF16), wq_ref[...], preferred_element_type=F32).astype(BF16)
    tm = q.shape[0]
    rows = lax.broadcasted_iota(jnp.int32, (PEER_KEYS, tm), 0)
    pair_ids = _pair_ids(tm)
    pair_ok = pair_ids != ID_NONE
    half = PEER_QDIM // 2
    eids, gates = [], []
    for hd in range(PEER_HEADS):
        sub = []
        for p in range(2):
            qs = q[:, (2 * hd + p) * half:(2 * hd + p + 1) * half]
            s = lax.dot_general(keys_ref[hd, p], qs, (((1,), (1,)), ((), ())), preferred_element_type=F32)
            sub.append(_top16(s, rows))
        (s1, i1), (s2, i2) = sub
        scores = _pair_blocks((s1, jnp.concatenate(s1, axis=0)), (s2, jnp.concatenate(s2, axis=0)))
        experts = _pair_blocks((i1, jnp.concatenate(i1, axis=0)), (i2, jnp.concatenate(i2, axis=0)))
        cand_s = jnp.where(pair_ok, jnp.concatenate([a + b for a, b in scores], axis=0), -jnp.inf)
        cand_e = jnp.concatenate([a * PEER_KEYS + b for a, b in experts], axis=0)
        top_s, sel = _top16(cand_s, pair_ids)
        top_e = [jnp.sum(jnp.where(pair_ids == i, cand_e, 0), axis=0, keepdims=True) for i in sel]
        ts_ = jnp.concatenate(top_s, axis=0)
        e = jnp.exp(ts_ - top_s[0])
        gates.append(e / jnp.sum(e, axis=0, keepdims=True))
        eids.append(jnp.concatenate(top_e, axis=0))
    eid_ref[...] = jnp.concatenate(eids, axis=0).T
    gate_ref[...] = jnp.concatenate(gates, axis=0).T


def _route(h, gain, wq_bf, keys_bf, tm):
    t, d = h.shape
    return pl.pallas_call(
        _route_body,
        grid=(t // tm,),
        in_specs=[pl.BlockSpec((tm, d), lambda i: (i, 0)),
                  pl.BlockSpec((1, d), lambda i: (0, 0)),
                  pl.BlockSpec(wq_bf.shape, lambda i: (0, 0)),
                  pl.BlockSpec(keys_bf.shape, lambda i: (0, 0, 0, 0))],
        out_specs=[pl.BlockSpec((tm, d // 2), lambda i: (i, 0)),
                   pl.BlockSpec((tm, PEER_PICKS), lambda i: (i, 0)),
                   pl.BlockSpec((tm, PEER_PICKS), lambda i: (i, 0))],
        out_shape=[jax.ShapeDtypeStruct((t, d // 2), jnp.int32),
                   jax.ShapeDtypeStruct((t, PEER_PICKS), jnp.int32),
                   jax.ShapeDtypeStruct((t, PEER_PICKS), F32)],
        compiler_params=_params("parallel"),
        name="peer_route",
    )(h, gain.reshape(1, d), wq_bf, keys_bf)


def _sc_mesh():
    return plsc.VectorSubcoreMesh(core_axis_name="core", subcore_axis_name="subcore")


def _sc_worker():
    return lax.axis_index("subcore") * SC_CORES + lax.axis_index("core")


def _sc_row_pipeline(tab_hbm, idx_v, buf, sems, compute):
    nslot, nrow = buf.shape[0], buf.shape[1]
    nsub = PEER_PICKS // nrow
    ng = SC_TOKENS * nsub

    def gather(g, slot):
        rows = idx_v.at[g // nsub, pl.ds((g % nsub) * nrow, nrow)]
        return pltpu.make_async_copy(tab_hbm.at[rows], buf.at[slot], sems.at[slot])

    for b in range(nslot - 1):
        gather(b, b).start()

    @pl.loop(0, ng, step=nslot)
    def _(g):
        for b in range(nslot):
            ahead = g + b + nslot - 1

            @pl.when(ahead < ng)
            def _():
                gather(ahead, (b + nslot - 1) % nslot).start()

            gather(g + b, b).wait()
            compute(g + b, b)


def _sc_split(words):
    return (lax.bitcast_convert_type(words & HI_MASK, F32), lax.bitcast_convert_type(words << 16, F32))


def _expert_scores(table, eid, hn):
    t, words = hn.shape
    per_worker = t // SC_WORKERS
    nslot, nrow = SC_SCORE_RING
    nsub = PEER_PICKS // nrow
    ln = SC_LANES

    @functools.partial(
        pl.kernel, mesh=_sc_mesh(),
        out_type=jax.ShapeDtypeStruct((t, PEER_PICKS), F32),
        scratch_types=[pltpu.VMEM((SC_TOKENS, PEER_PICKS), jnp.int32),
                       pltpu.VMEM((SC_TOKENS, words), jnp.int32),
                       pltpu.VMEM((SC_TOKENS, PEER_PICKS), F32),
                       pltpu.VMEM((nslot, nrow, words), jnp.int32),
                       pltpu.SemaphoreType.DMA((nslot,))],
        compiler_params=pltpu.CompilerParams(needs_layout_passes=False),
        name="peer_expert_scores")
    def run(tab_hbm, eid_hbm, hn_hbm, out_hbm, idx_v, x_v, a_v, buf, sems):
        lane = lax.iota(jnp.int32, ln)

        def compute(g, slot):
            tok = g // nsub
            sub = g % nsub
            for grp in range(nrow // ln):
                def body(c, accs):
                    c0 = pl.ds(2 * c * ln, ln)
                    c1 = pl.ds((2 * c + 1) * ln, ln)
                    x0 = plsc.bitcast(x_v[tok, c0], BF16)
                    x1 = plsc.bitcast(x_v[tok, c1], BF16)
                    out = []
                    for r in range(ln):
                        u0 = plsc.bitcast(buf[slot, grp * ln + r, c0], BF16)
                        u1 = plsc.bitcast(buf[slot, grp * ln + r, c1], BF16)
                        hi, lo = _sc_split(plsc.bitcast(u0 * x0 + u1 * x1, jnp.int32))
                        out.append(accs[r] + hi + lo)
                    return tuple(out)

                accs = lax.fori_loop(0, words // (2 * ln), body, tuple(jnp.zeros((ln,), F32) for _ in range(ln)))
                res = jnp.zeros((ln,), F32)
                for r in range(ln):
                    res = jnp.where(lane == r, jnp.sum(accs[r]), res)
                a_v[tok, pl.ds(sub * nrow + grp * ln, ln)] = res

        @pl.loop(0, per_worker // SC_TOKENS)
        def _(blk):
            tok0 = _sc_worker() * per_worker + blk * SC_TOKENS
            pltpu.sync_copy(eid_hbm.at[pl.ds(tok0, SC_TOKENS)], idx_v)
            pltpu.sync_copy(hn_hbm.at[pl.ds(tok0, SC_TOKENS)], x_v)
            _sc_row_pipeline(tab_hbm, idx_v, buf, sems, compute)
            pltpu.sync_copy(a_v, out_hbm.at[pl.ds(tok0, SC_TOKENS)])

    return run(table, eid, hn)


def _expert_mix(table, eid, w):
    t = eid.shape[0]
    words = table.shape[1]
    d = 2 * words
    per_worker = t // SC_WORKERS
    nslot, nrow = SC_MIX_RING
    nsub = PEER_PICKS // nrow
    ln = SC_LANES
    nvec = SC_WORDS // ln

    @functools.partial(
        pl.kernel, mesh=_sc_mesh(),
        out_type=jax.ShapeDtypeStruct((t, d), F32),
        scratch_types=[pltpu.VMEM((SC_TOKENS, PEER_PICKS), jnp.int32),
                       pltpu.VMEM((SC_TOKENS, PEER_PICKS), jnp.int32),
                       pltpu.VMEM((SC_TOKENS, d), F32),
                       pltpu.VMEM((nslot, nrow, words), jnp.int32),
                       pltpu.SemaphoreType.DMA((nslot,))],
        compiler_params=pltpu.CompilerParams(needs_layout_passes=False),
        name="peer_expert_mix")
    def run(tab_hbm, eid_hbm, w_hbm, out_hbm, idx_v, w_v, y_v, buf, sems):
        zero = jnp.zeros((ln,), F32)

        def compute(g, slot):
            tok = g // nsub
            sub = g % nsub
            tokv = jnp.full((ln,), tok, jnp.int32)

            def weight(r):
                pick = jnp.full((ln,), sub * nrow + r, jnp.int32)
                return plsc.bitcast(plsc.load_gather(w_v, [tokv, pick]), BF16)

            for cc in range(words // SC_WORDS):
                def body(rg, accs):
                    r0 = SC_MIX_GROUP * rg
                    ws = [weight(r0 + j) for j in range(SC_MIX_GROUP)]
                    out = list(accs)
                    for k in range(nvec):
                        cols = pl.ds(cc * SC_WORDS + k * ln, ln)
                        prod = ws[0] * plsc.bitcast(buf[slot, r0, cols], BF16)
                        for j in range(1, SC_MIX_GROUP):
                            prod = prod + ws[j] * plsc.bitcast(buf[slot, r0 + j, cols], BF16)
                        hi, lo = _sc_split(plsc.bitcast(prod, jnp.int32))
                        out[k] = accs[k] + hi
                        out[nvec + k] = accs[nvec + k] + lo
                    return tuple(out)

                lo_cols = [pl.ds(cc * SC_WORDS + k * ln, ln) for k in range(nvec)]
                hi_cols = [pl.ds(words + cc * SC_WORDS + k * ln, ln) for k in range(nvec)]
                init = tuple(y_v[tok, c] for c in lo_cols + hi_cols)
                accs = lax.fori_loop(0, nrow // SC_MIX_GROUP, body, init)
                for c, acc in zip(lo_cols + hi_cols, accs):
                    y_v[tok, c] = acc

        @pl.loop(0, per_worker // SC_TOKENS)
        def _(blk):
            tok0 = _sc_worker() * per_worker + blk * SC_TOKENS
            pltpu.sync_copy(eid_hbm.at[pl.ds(tok0, SC_TOKENS)], idx_v)
            pltpu.sync_copy(w_hbm.at[pl.ds(tok0, SC_TOKENS)], w_v)

            @pl.loop(0, SC_TOKENS)
            def _(tok):
                @pl.loop(0, d // ln)
                def _(c):
                    y_v[tok, pl.ds(c * ln, ln)] = zero

            _sc_row_pipeline(tab_hbm, idx_v, buf, sems, compute)
            pltpu.sync_copy(y_v, out_hbm.at[pl.ds(tok0, SC_TOKENS)])

    return run(table, eid, w)


def _pick_weights_body(a_ref, g_ref, o_ref):
    o_ref[...] = _pack_twice(g_ref[...] * _gelu(a_ref[...]))


def _pick_weights(a, gate, tm):
    t = a.shape[0]
    spec = pl.BlockSpec((tm, PEER_PICKS), lambda i: (i, 0))
    return pl.pallas_call(
        _pick_weights_body,
        grid=(t // tm,),
        in_specs=[spec, spec],
        out_specs=spec,
        out_shape=jax.ShapeDtypeStruct(a.shape, jnp.int32),
        compiler_params=_params("parallel"),
        name="peer_pick_weights",
    )(a, gate)


def _ple_body(h_ref, y_ref, g_ref, wg_ref, p_ref, wp_ref, o_ref):
    h = h_ref[...] + y_ref[...]
    hn = _rms(h, g_ref[...]).astype(BF16)
    gate = _sigmoid(jnp.dot(hn, wg_ref[...], preferred_element_type=F32))
    emb = jnp.dot(p_ref[...].astype(BF16), wp_ref[...], preferred_element_type=F32)
    o_ref[...] = h + gate * emb


def _ple(h, y, gain, wg_bf, p, wp_bf, tm):
    t, d = h.shape
    return pl.pallas_call(
        _ple_body,
        grid=(t // tm,),
        in_specs=[pl.BlockSpec((tm, d), lambda i: (i, 0)),
                  pl.BlockSpec((tm, d), lambda i: (i, 0)),
                  pl.BlockSpec((1, d), lambda i: (0, 0)),
                  pl.BlockSpec(wg_bf.shape, lambda i: (0, 0)),
                  pl.BlockSpec((tm, p.shape[1]), lambda i: (i, 0)),
                  pl.BlockSpec(wp_bf.shape, lambda i: (0, 0))],
        out_specs=pl.BlockSpec((tm, d), lambda i: (i, 0)),
        out_shape=jax.ShapeDtypeStruct((t, d), F32),
        compiler_params=_params("parallel"),
        name="layer_embedding",
    )(h, y, gain.reshape(1, d), wg_bf, p, wp_bf)


def _rope_tables(seq, dim):
    inv = 1.0 / (ROPE_THETA ** (jnp.arange(0, dim, 2, dtype=F32) / dim))
    ang = jnp.arange(seq, dtype=F32)[:, None] * inv[None, :]
    return jnp.cos(ang), jnp.sin(ang)


def _pack_table(tab):
    n = tab.shape[1] // 2
    b = lax.bitcast_convert_type(tab.astype(BF16), jnp.uint16).astype(jnp.uint32)
    return lax.bitcast_convert_type((b[:, :n] << 16) | b[:, n:], jnp.int32)


def _permute_in_columns(w_in):
    return jnp.concatenate([w_in[:, 3840:], w_in[:, :3840]], axis=1)


def kernel(x, p, norm_mix, w_in, ret_decay, ret_norm, sgu_ln_g, sgu_ln_b, sgu_w, sgu_b, att_q_norm, att_k_norm, att_sink, w_proj_ret, w_proj_sgu, w_proj_att, w_out, norm_ffn, peer_wq, peer_keys, peer_u, peer_v, norm_ple, ple_gate, ple_proj):
    batch, seq, d = x.shape
    depth = w_in.shape[0]
    groups = PIPELINE_GROUPS if batch % PIPELINE_GROUPS == 0 else 1
    gb = batch // groups
    t = gb * seq
    ts = min(512, seq)
    tm = min(512, t)
    cos_r, sin_r = _rope_tables(seq, RET_DK)
    cos_r2 = jnp.concatenate([cos_r, cos_r], axis=1)
    sin_r2 = jnp.concatenate([-sin_r, sin_r], axis=1)
    cos_a, sin_a = _rope_tables(seq, ATT_DH)
    lw = []
    for i in range(depth):
        lw.append(dict(
            w_in=_permute_in_columns(w_in[i]).astype(BF16),
            w_r=w_proj_ret[i].astype(BF16), w_s=w_proj_sgu[i].astype(BF16), w_a=w_proj_att[i].astype(BF16),
            w_o=w_out[i].astype(BF16), w_q=peer_wq[i].astype(BF16), keys=peer_keys[i].astype(BF16),
            w_g=ple_gate[i].astype(BF16), w_p=ple_proj[i].astype(BF16),
            tab_u=_pack_table(peer_u[i]), tab_v=_pack_table(peer_v[i])))

    def mixers(i, h):
        c = lw[i]
        z = _norm_matmul(h, norm_mix[i], c["w_in"], tm, 1152)
        y_r = _retention(z, ret_decay[i], ret_norm[i], cos_r2, sin_r2, gb, seq, ts)
        y_s = _sgu(z, sgu_ln_g[i], sgu_ln_b[i], sgu_w[i], sgu_b[i], ts)
        y_a = _attention(z, att_q_norm[i], att_k_norm[i], att_sink[i], cos_a, sin_a, gb, seq, ts)
        h = _merge(h, z, y_r, y_s, y_a, c["w_r"], c["w_s"], c["w_a"], c["w_o"], tm)
        hn, eid, gate = _route(h, norm_ffn[i], c["w_q"], c["keys"], min(256, t))
        return dict(h=h, eid=eid, gate=gate, a=_expert_scores(c["tab_u"], eid, hn))

    def finish(i, g, job, y):
        p_g = p[i, g * gb:(g + 1) * gb].reshape(t, -1)
        return _ple(job["h"], y, norm_ple[i], lw[i]["w_g"], p_g, lw[i]["w_p"], tm)

    jobs = [(i, g) for i in range(depth) for g in range(groups)]
    hs = [x[g * gb:(g + 1) * gb].reshape(t, d) for g in range(groups)]
    state, mixed = {}, {}

    def issue_mix(job_id, tied):
        if job_id in mixed:
            return tied
        w = _pick_weights(state[job_id]["a"], state[job_id]["gate"], tm)
        if tied is not None:
            tied, w = lax.optimization_barrier((tied, w))
        mixed[job_id] = _expert_mix(lw[job_id[0]]["tab_v"], state[job_id]["eid"], w)
        return tied

    for k, (i, g) in enumerate(jobs):
        if i > 0:
            issue_mix((i - 1, g), None)
            hs[g] = finish(i - 1, g, state[(i - 1, g)], mixed[(i - 1, g)])
        h_in = hs[g]
        if k >= 2:
            h_in = issue_mix(jobs[k - 2], h_in)
        state[(i, g)] = mixers(i, h_in)
    for job_id in jobs[max(len(jobs) - 2, 0):]:
        issue_mix(job_id, None)
    out = [finish(depth - 1, g, state[(depth - 1, g)], mixed[(depth - 1, g)]) for g in range(groups)]
    return jnp.concatenate(out, axis=0).reshape(batch, seq, d)
```

```python
import functools
import math

import jax
import jax.numpy as jnp
from jax import lax
from jax.experimental import pallas as pl
from jax.experimental.pallas import tpu as pltpu
from jax.experimental.pallas import tpu_sc as plsc

F32 = jnp.float32
BF16 = jnp.bfloat16

D_MODEL = 1024
PLE_DIM = 256
CHUNK = 128
EPS = 1e-6
ROPE_THETA = 10000.0
RET_HEADS = 4
RET_DK = 128
SGU_GROUPS = 4
SGU_WIDTH = 512
ATT_HEADS = 8
ATT_KV_HEADS = 2
ATT_DH = 64
ATT_GROUP = ATT_HEADS // ATT_KV_HEADS
NEG_INF = -1e30
PEER_HEADS = 8
PEER_KEYS = 128
PEER_QDIM = 256
PEER_TOPK = 16
PEER_PICKS = PEER_HEADS * PEER_TOPK

Z_GATE_R, Z_GATE_S, Z_GATE_A = 0, 1024, 2048
Z_QR, Z_KR, Z_VR, Z_GR = 3072, 3584, 4096, 4608
Z_US, Z_VS = 5120, 5632
Z_QA, Z_KA, Z_VA = 6144, 6656, 6784
D_IN = 6912

LANE = 128
SC_CORES = 2
SC_SUBCORES = 16
SC_LANES = 16
SC_WORKERS = SC_CORES * SC_SUBCORES
SC_TOKENS = 32
SC_SCORE_RING = (4, 32)
SC_MIX_RING = (2, 64)
SC_MIX_GROUP = 4
SC_WORDS = 128
HI_MASK = -65536
VMEM_LIMIT = 56 * 1024 * 1024
PIPELINE_GROUPS = 4


def _params(*sem):
    return pltpu.CompilerParams(dimension_semantics=sem, vmem_limit_bytes=VMEM_LIMIT)


def _gelu(x):
    return 0.5 * x * (1.0 + lax.erf(x * (1.0 / math.sqrt(2.0))))


def _sigmoid(x):
    return 1.0 / (1.0 + jnp.exp(-x))


def _rms(x, g):
    return x * lax.rsqrt(jnp.mean(x * x, axis=-1, keepdims=True) + EPS) * g


def _bf16_hi_bits(x):
    b = lax.bitcast_convert_type(x, jnp.int32)
    return (b + 0x7FFF + ((b >> 16) & 1)) & HI_MASK


def _pack_halves(x):
    n = x.shape[1] // 2
    return _bf16_hi_bits(x[:, :n]) | lax.shift_right_logical(_bf16_hi_bits(x[:, n:]), 16)


def _pack_twice(x):
    b = _bf16_hi_bits(x)
    return b | lax.shift_right_logical(b, 16)


INPROJ_COLS = 1152


def _inproj_body(x_ref, g_ref, w_ref, o_ref):
    xn = _rms(x_ref[...], g_ref[...]).astype(BF16)
    for j in range(w_ref.shape[1] // INPROJ_COLS):
        cols = slice(j * INPROJ_COLS, (j + 1) * INPROJ_COLS)
        o_ref[:, cols] = jnp.dot(xn, w_ref[:, cols], preferred_element_type=F32).astype(o_ref.dtype)


def _in_projection(h, gain, w_bf, tm):
    t, d = h.shape
    n = w_bf.shape[1]
    return pl.pallas_call(
        _inproj_body,
        grid=(t // tm,),
        in_specs=[pl.BlockSpec((tm, d), lambda i: (i, 0)),
                  pl.BlockSpec((1, d), lambda i: (0, 0)),
                  pl.BlockSpec((d, n), lambda i: (0, 0))],
        out_specs=pl.BlockSpec((tm, n), lambda i: (i, 0)),
        out_shape=jax.ShapeDtypeStruct((t, n), BF16),
        compiler_params=_params("parallel"),
        name="in_projection",
    )(h, gain.reshape(1, d), w_bf)


def _rope128(x, cos, sin_signed):
    return x * cos + pltpu.roll(x, 64, 1) * sin_signed


def _ret_bwd_body(q_ref, k_ref, v_ref, cos_ref, sin_ref, qw_ref, kw_ref, cd_ref, o_ref, st_ref, *, nchunk):
    @pl.when(pl.program_id(2) == 0)
    def _():
        st_ref[...] = jnp.zeros_like(st_ref)

    qw = qw_ref[0]
    kw = kw_ref[0]
    cd = cd_ref[0, 0:1, :]
    for c in reversed(range(nchunk)):
        rows = pl.ds(c * CHUNK, CHUNK)
        cos = cos_ref[rows, :]
        sin = sin_ref[rows, :]
        q = _rope128(q_ref[rows, :].astype(F32), cos, sin)
        k = _rope128(k_ref[rows, :].astype(F32), cos, sin) * (RET_DK ** -0.5)
        v = v_ref[rows, :]
        st = st_ref[...]
        o_ref[rows, :] = jnp.dot((q * qw).astype(BF16), st.astype(BF16), preferred_element_type=F32)
        kv = jnp.dot((k * kw).T.astype(BF16), v, preferred_element_type=F32)
        st_ref[...] = st * cd + kv


def _ret_fwd_body(q_ref, k_ref, v_ref, g_ref, yb_ref, cos_ref, sin_ref, dm_ref, qw_ref, kw_ref, cd_ref,
                  gn_ref, o_ref, st_ref, *, nchunk):
    @pl.when(pl.program_id(2) == 0)
    def _():
        st_ref[...] = jnp.zeros_like(st_ref)

    qw = qw_ref[0]
    kw = kw_ref[0]
    cd = cd_ref[0, 0:1, :]
    dm = dm_ref[0]
    gn = gn_ref[0, 0:1, :]
    for c in range(nchunk):
        rows = pl.ds(c * CHUNK, CHUNK)
        cos = cos_ref[rows, :]
        sin = sin_ref[rows, :]
        q = _rope128(q_ref[rows, :].astype(F32), cos, sin)
        k = _rope128(k_ref[rows, :].astype(F32), cos, sin) * (RET_DK ** -0.5)
        v = v_ref[rows, :]
        st = st_ref[...]
        s = lax.dot_general(q.astype(BF16), k.astype(BF16), (((1,), (1,)), ((), ())),
                            preferred_element_type=F32) * dm
        y = jnp.dot(s.astype(BF16), v, preferred_element_type=F32)
        y += jnp.dot((q * qw).astype(BF16), st.astype(BF16), preferred_element_type=F32)
        y += yb_ref[rows, :]
        kv = jnp.dot((k * kw).T.astype(BF16), v, preferred_element_type=F32)
        st_ref[...] = st * cd + kv
        y = y * lax.rsqrt(jnp.mean(y * y, axis=-1, keepdims=True) + EPS) * gn
        g = g_ref[rows, :].astype(F32)
        o_ref[rows, :] = (g * _sigmoid(g) * y).astype(o_ref.dtype)


def _retention(z, ret_decay, ret_norm, cos, sin, batch, seq, ts):
    t = z.shape[0]
    nchunk = ts // CHUNK
    nstep = seq // ts
    hd = RET_HEADS
    log_g = jax.nn.log_sigmoid(ret_decay.astype(F32))
    idx = jnp.arange(CHUNK, dtype=F32)
    diff = idx[:, None] - idx[None, :]
    lf = log_g[0][:, None, None]
    lb = log_g[1][:, None, None]
    dmat = jnp.where(diff[None] >= 0, jnp.exp(lf * jnp.maximum(diff, 0.0)[None]),
                     jnp.exp(lb * jnp.maximum(-diff, 0.0)[None]))
    bc = lambda a: jnp.broadcast_to(a[:, :, None], (hd, a.shape[1], LANE))
    qw_f = bc(jnp.exp(log_g[0][:, None] * (idx + 1.0)[None, :]))
    kw_f = bc(jnp.exp(log_g[0][:, None] * (CHUNK - 1 - idx)[None, :]))
    qw_b = bc(jnp.exp(log_g[1][:, None] * (CHUNK - idx)[None, :]))
    kw_b = bc(jnp.exp(log_g[1][:, None] * idx[None, :]))
    cd_f = jnp.broadcast_to(jnp.exp(log_g[0] * CHUNK)[:, None, None], (hd, 8, LANE))
    cd_b = jnp.broadcast_to(jnp.exp(log_g[1] * CHUNK)[:, None, None], (hd, 8, LANE))
    gn = jnp.broadcast_to(ret_norm.astype(F32).reshape(hd, 1, LANE), (hd, 8, LANE))

    def zspec(col0, rev):
        cb = col0 // LANE
        if rev:
            return pl.BlockSpec((ts, LANE), lambda b, h, s: (b * nstep + nstep - 1 - s, cb + h))
        return pl.BlockSpec((ts, LANE), lambda b, h, s: (b * nstep + s, cb + h))

    def tspec(rev):
        if rev:
            return pl.BlockSpec((ts, LANE), lambda b, h, s: (nstep - 1 - s, 0))
        return pl.BlockSpec((ts, LANE), lambda b, h, s: (s, 0))

    hspec = lambda r: pl.BlockSpec((1, r, LANE), lambda b, h, s: (h, 0, 0))

    yb = pl.pallas_call(
        functools.partial(_ret_bwd_body, nchunk=nchunk),
        grid=(batch, hd, nstep),
        in_specs=[zspec(Z_QR, True), zspec(Z_KR, True), zspec(Z_VR, True), tspec(True), tspec(True),
                  hspec(CHUNK), hspec(CHUNK), hspec(8)],
        out_specs=pl.BlockSpec((ts, LANE), lambda b, h, s: (b * nstep + nstep - 1 - s, h)),
        out_shape=jax.ShapeDtypeStruct((t, hd * LANE), F32),
        scratch_shapes=[pltpu.VMEM((RET_DK, LANE), F32)],
        compiler_params=_params("parallel", "parallel", "arbitrary"),
        name="retention_bwd",
    )(z, z, z, cos, sin, qw_b, kw_b, cd_b)

    return pl.pallas_call(
        functools.partial(_ret_fwd_body, nchunk=nchunk),
        grid=(batch, hd, nstep),
        in_specs=[zspec(Z_QR, False), zspec(Z_KR, False), zspec(Z_VR, False), zspec(Z_GR, False),
                  pl.BlockSpec((ts, LANE), lambda b, h, s: (b * nstep + s, h)),
                  tspec(False), tspec(False), hspec(CHUNK), hspec(CHUNK), hspec(CHUNK), hspec(8), hspec(8)],
        out_specs=pl.BlockSpec((ts, LANE), lambda b, h, s: (b * nstep + s, h)),
        out_shape=jax.ShapeDtypeStruct((t, hd * LANE), BF16),
        scratch_shapes=[pltpu.VMEM((RET_DK, LANE), F32)],
        compiler_params=_params("parallel", "parallel", "arbitrary"),
        name="retention_fwd",
    )(z, z, z, z, yb, cos, sin, dmat, qw_f, kw_f, cd_f, gn)


def _sgu_body(u_ref, v_ref, lg_ref, lb_ref, w_ref, b_ref, o_ref, *, nchunk):
    lg = lg_ref[...]
    lb = lb_ref[...]
    for c in range(nchunk):
        rows = pl.ds(c * CHUNK, CHUNK)
        vf = _gelu(v_ref[rows, :].astype(F32))
        mu = jnp.mean(vf, axis=-1, keepdims=True)
        vc = vf - mu
        var = jnp.mean(vc * vc, axis=-1, keepdims=True)
        vn = (vc * lax.rsqrt(var + EPS) * lg + lb).astype(BF16)
        for g in range(SGU_GROUPS):
            cols = slice(g * LANE, (g + 1) * LANE)
            mixed = jnp.dot(w_ref[g], vn[:, cols], preferred_element_type=F32) + b_ref[g]
            uf = _gelu(u_ref[rows, cols].astype(F32))
            o_ref[rows, cols] = (uf * mixed).astype(o_ref.dtype)


def _sgu(z, ln_g, ln_b, w_s, b_s, ts):
    t = z.shape[0]
    bias = jnp.broadcast_to(b_s.astype(F32)[:, :, None], (SGU_GROUPS, CHUNK, LANE))
    return pl.pallas_call(
        functools.partial(_sgu_body, nchunk=ts // CHUNK),
        grid=(t // ts,),
        in_specs=[pl.BlockSpec((ts, SGU_WIDTH), lambda i: (i, Z_US // SGU_WIDTH)),
                  pl.BlockSpec((ts, SGU_WIDTH), lambda i: (i, Z_VS // SGU_WIDTH)),
                  pl.BlockSpec((1, SGU_WIDTH), lambda i: (0, 0)),
                  pl.BlockSpec((1, SGU_WIDTH), lambda i: (0, 0)),
                  pl.BlockSpec((SGU_GROUPS, CHUNK, CHUNK), lambda i: (0, 0, 0)),
                  pl.BlockSpec((SGU_GROUPS, CHUNK, LANE), lambda i: (0, 0, 0))],
        out_specs=pl.BlockSpec((ts, SGU_WIDTH), lambda i: (i, 0)),
        out_shape=jax.ShapeDtypeStruct((t, SGU_WIDTH), BF16),
        compiler_params=_params("parallel"),
        name="spatial_gating",
    )(z, z, ln_g.reshape(1, -1).astype(F32), ln_b.reshape(1, -1).astype(F32), w_s.astype(BF16), bias)


def _pair_norm_rope(x, gain, cos, sin_up, sin_dn, low):
    sq = x * x
    lo = jnp.sum(jnp.where(low, sq, 0.0), axis=-1, keepdims=True)
    hi = jnp.sum(sq, axis=-1, keepdims=True) - lo
    ms = jnp.where(low, lo, hi) * (1.0 / ATT_DH)
    xn = x * lax.rsqrt(ms + EPS) * gain
    return xn * cos + pltpu.roll(xn, LANE - 32, 1) * sin_up + pltpu.roll(xn, 32, 1) * sin_dn


def _attn_body(sink_ref, q_ref, kp_ref, k_ref, kn_ref, vp_ref, v_ref, vn_ref,
               cq_ref, suq_ref, sdq_ref, ckp_ref, sukp_ref, sdkp_ref, ckn_ref, sukn_ref, sdkn_ref,
               qg_ref, kg_ref, o_ref, *, nchunk, nstep):
    s_id = pl.program_id(1)
    ts = nchunk * CHUNK
    lane = lax.broadcasted_iota(jnp.int32, (1, LANE), 1)
    low = lane < ATT_DH
    kg = kg_ref[...]
    qg = qg_ref[...]
    k_ext = jnp.concatenate([
        _pair_norm_rope(kp_ref[...].astype(F32), kg, ckp_ref[...], sukp_ref[...], sdkp_ref[...], low),
        _pair_norm_rope(k_ref[...].astype(F32), kg, cq_ref[...], suq_ref[...], sdq_ref[...], low),
        _pair_norm_rope(kn_ref[...].astype(F32), kg, ckn_ref[...], sukn_ref[...], sdkn_ref[...], low)], axis=0)
    v_ext = jnp.concatenate([vp_ref[...], v_ref[...], vn_ref[...]], axis=0)
    k_lo = jnp.where(low, k_ext, 0.0).astype(BF16)
    k_hi = jnp.where(low, 0.0, k_ext).astype(BF16)
    qi = lax.broadcasted_iota(jnp.int32, (CHUNK, 3 * CHUNK), 0)
    kj = lax.broadcasted_iota(jnp.int32, (CHUNK, 3 * CHUNK), 1)
    band = jnp.abs(qi + CHUNK - kj) <= CHUNK
    for c in range(nchunk):
        rows = pl.ds(c * CHUNK, CHUNK)
        first = jnp.logical_and(s_id == 0, c == 0)
        last = jnp.logical_and(s_id == nstep - 1, c == nchunk - 1)
        valid = band
        if c == 0:
            valid = jnp.logical_and(valid, jnp.logical_or(kj >= CHUNK, jnp.logical_not(first)))
        if c == nchunk - 1:
            valid = jnp.logical_and(valid, jnp.logical_or(kj < 2 * CHUNK, jnp.logical_not(last)))
        kc_lo = k_lo[c * CHUNK:(c + 3) * CHUNK]
        kc_hi = k_hi[c * CHUNK:(c + 3) * CHUNK]
        vc = v_ext[c * CHUNK:(c + 3) * CHUNK]
        cos = cq_ref[rows, :]
        su = suq_ref[rows, :]
        sd = sdq_ref[rows, :]
        for pair in range(ATT_HEADS // 2):
            cols = slice(pair * LANE, (pair + 1) * LANE)
            qp = _pair_norm_rope(q_ref[rows, cols].astype(F32), qg, cos, su, sd, low) * (ATT_DH ** -0.5)
            kv_head = (2 * pair) // ATT_GROUP
            outs = []
            for half in range(2):
                head = 2 * pair + half
                qh = qp if half == kv_head else pltpu.roll(qp, ATT_DH, 1)
                if kv_head == 0:
                    qh = jnp.where(low, qh, 0.0)
                    kc = kc_lo
                else:
                    qh = jnp.where(low, 0.0, qh)
                    kc = kc_hi
                s = lax.dot_general(qh.astype(BF16), kc, (((1,), (1,)), ((), ())), preferred_element_type=F32)
                s = jnp.where(valid, s, NEG_INF)
                sk = sink_ref[head]
                m = jnp.maximum(jnp.max(s, axis=-1, keepdims=True), sk)
                e = jnp.exp(s - m)
                den = jnp.sum(e, axis=-1, keepdims=True) + jnp.exp(sk - m)
                o = jnp.dot(e.astype(BF16), vc, preferred_element_type=F32) / den
                outs.append(o if half == kv_head else pltpu.roll(o, ATT_DH, 1))
            o_ref[rows, cols] = jnp.where(low, outs[0], outs[1]).astype(o_ref.dtype)


def _attention(z, q_gain, k_gain, sink, cos, sin, batch, seq, ts):
    t = z.shape[0]
    nchunk = ts // CHUNK
    nstep = seq // ts
    nblk = seq // CHUNK
    cos2 = jnp.tile(jnp.concatenate([cos, cos], axis=1), (1, 2))
    zero = jnp.zeros_like(sin)
    sin_up = jnp.tile(jnp.concatenate([-sin, zero], axis=1), (1, 2))
    sin_dn = jnp.tile(jnp.concatenate([zero, sin], axis=1), (1, 2))
    qg = jnp.tile(q_gain.astype(F32), 2).reshape(1, LANE)
    kg = jnp.tile(k_gain.astype(F32), 2).reshape(1, LANE)

    kcb, vcb = Z_KA // LANE, Z_VA // LANE
    prev_blk = lambda s: jnp.maximum(s * nchunk - 1, 0)
    next_blk = lambda s: jnp.minimum((s + 1) * nchunk, nblk - 1)
    main = lambda cb: pl.BlockSpec((ts, LANE), lambda b, s: (b * nstep + s, cb))
    prev = lambda cb: pl.BlockSpec((CHUNK, LANE), lambda b, s: (b * nblk + prev_blk(s), cb))
    nxt = lambda cb: pl.BlockSpec((CHUNK, LANE), lambda b, s: (b * nblk + next_blk(s), cb))
    tmain = pl.BlockSpec((ts, LANE), lambda b, s: (s, 0))
    tprev = pl.BlockSpec((CHUNK, LANE), lambda b, s: (prev_blk(s), 0))
    tnext = pl.BlockSpec((CHUNK, LANE), lambda b, s: (next_blk(s), 0))
    one = pl.BlockSpec((1, LANE), lambda b, s: (0, 0))
    return pl.pallas_call(
        functools.partial(_attn_body, nchunk=nchunk, nstep=nstep),
        grid=(batch, nstep),
        in_specs=[pl.BlockSpec(memory_space=pltpu.SMEM),
                  pl.BlockSpec((ts, ATT_HEADS * ATT_DH), lambda b, s: (b * nstep + s, Z_QA // 512)),
                  prev(kcb), main(kcb), nxt(kcb), prev(vcb), main(vcb), nxt(vcb),
                  tmain, tmain, tmain, tprev, tprev, tprev, tnext, tnext, tnext, one, one],
        out_specs=pl.BlockSpec((ts, ATT_HEADS * ATT_DH), lambda b, s: (b * nstep + s, 0)),
        out_shape=jax.ShapeDtypeStruct((t, ATT_HEADS * ATT_DH), BF16),
        compiler_params=_params("parallel", "arbitrary"),
        name="window_attention",
    )(sink.astype(F32), z, z, z, z, z, z, z,
      cos2, sin_up, sin_dn, cos2, sin_up, sin_dn, cos2, sin_up, sin_dn, qg, kg)


def _merge_body(h_ref, gr_ref, gs_ref, ga_ref, yr_ref, ys_ref, ya_ref, wr_ref, ws_ref, wa_ref, wo_ref, o_ref):
    m = _sigmoid(gr_ref[...].astype(F32)) * jnp.dot(yr_ref[...], wr_ref[...], preferred_element_type=F32)
    m += _sigmoid(gs_ref[...].astype(F32)) * jnp.dot(ys_ref[...], ws_ref[...], preferred_element_type=F32)
    m += _sigmoid(ga_ref[...].astype(F32)) * jnp.dot(ya_ref[...], wa_ref[...], preferred_element_type=F32)
    o_ref[...] = h_ref[...] + jnp.dot(m.astype(BF16), wo_ref[...], preferred_element_type=F32)


def _merge(h, z, y_r, y_s, y_a, w_r, w_s, w_a, w_o, tm):
    t, d = h.shape
    row = lambda w: pl.BlockSpec((tm, w), lambda i: (i, 0))
    gate = lambda col0: pl.BlockSpec((tm, d), lambda i: (i, col0 // d))
    full = lambda a: pl.BlockSpec(a.shape, lambda i: (0, 0))
    return pl.pallas_call(
        _merge_body,
        grid=(t // tm,),
        in_specs=[row(d), gate(Z_GATE_R), gate(Z_GATE_S), gate(Z_GATE_A), row(512), row(512), row(512),
                  full(w_r), full(w_s), full(w_a), full(w_o)],
        out_specs=row(d),
        out_shape=jax.ShapeDtypeStruct((t, d), F32),
        compiler_params=_params("parallel"),
        name="branch_merge",
    )(h, z, z, z, y_r, y_s, y_a, w_r, w_s, w_a, w_o)


ID_NONE = 1 << 20


def _top16(vals, ids):
    out_v, out_i = [], []
    for _ in range(PEER_TOPK):
        m = jnp.max(vals, axis=0, keepdims=True)
        idx = jnp.min(jnp.where(vals == m, ids, ID_NONE), axis=0, keepdims=True)
        out_v.append(m)
        out_i.append(idx)
        vals = jnp.where(ids == idx, -jnp.inf, vals)
    return out_v, out_i


def _pair_blocks(first, second):
    rows1, stack1 = first
    rows2, stack2 = second
    blocks = [(stack1, rows2[0])]
    blocks += [(stack1[0:8], rows2[b]) for b in range(1, 8)]
    blocks += [(rows1[0], stack2[8:16])]
    return blocks


def _pair_ids(tm):
    a8 = lax.broadcasted_iota(jnp.int32, (8, tm), 0)
    a16 = lax.broadcasted_iota(jnp.int32, (PEER_TOPK, tm), 0)
    blocks = [a16 * PEER_TOPK]
    blocks += [jnp.where(a8 < PEER_TOPK // (b + 1), a8 * PEER_TOPK + b, ID_NONE) for b in range(1, 8)]
    blocks += [a8 + 8]
    return jnp.concatenate(blocks, axis=0)


def _route_body(h_ref, g_ref, wq_ref, keys_ref, hn_ref, eid_ref, gate_ref):
    hn = _rms(h_ref[...], g_ref[...])
    hn_ref[...] = _pack_halves(hn)
    q = jnp.dot(hn.astype(BF16), wq_ref[...], preferred_element_type=F32).astype(BF16)
    tm = q.shape[0]
    rows = lax.broadcasted_iota(jnp.int32, (PEER_KEYS, tm), 0)
    pair_ids = _pair_ids(tm)
    pair_ok = pair_ids != ID_NONE
    half = PEER_QDIM // 2
    eids, gates = [], []
    for hd in range(PEER_HEADS):
        sub = []
        for p in range(2):
            qs = q[:, (2 * hd + p) * half:(2 * hd + p + 1) * half]
            s = lax.dot_general(keys_ref[hd, p], qs, (((1,), (1,)), ((), ())), preferred_element_type=F32)
            sub.append(_top16(s, rows))
        (s1, i1), (s2, i2) = sub
        scores = _pair_blocks((s1, jnp.concatenate(s1, axis=0)), (s2, jnp.concatenate(s2, axis=0)))
        experts = _pair_blocks((i1, jnp.concatenate(i1, axis=0)), (i2, jnp.concatenate(i2, axis=0)))
        cand_s = jnp.where(pair_ok, jnp.concatenate([a + b for a, b in scores], axis=0), -jnp.inf)
        cand_e = jnp.concatenate([a * PEER_KEYS + b for a, b in experts], axis=0)
        top_s, sel = _top16(cand_s, pair_ids)
        top_e = [jnp.sum(jnp.where(pair_ids == i, cand_e, 0), axis=0, keepdims=True) for i in sel]
        ts_ = jnp.concatenate(top_s, axis=0)
        e = jnp.exp(ts_ - top_s[0])
        gates.append(e / jnp.sum(e, axis=0, keepdims=True))
        eids.append(jnp.concatenate(top_e, axis=0))
    eid_ref[...] = jnp.concatenate(eids, axis=0).T
    gate_ref[...] = jnp.concatenate(gates, axis=0).T


def _route(h, gain, wq_bf, keys_bf, tm):
    t, d = h.shape
    return pl.pallas_call(
        _route_body,
        grid=(t // tm,),
        in_specs=[pl.BlockSpec((tm, d), lambda i: (i, 0)),
                  pl.BlockSpec((1, d), lambda i: (0, 0)),
                  pl.BlockSpec(wq_bf.shape, lambda i: (0, 0)),
                  pl.BlockSpec(keys_bf.shape, lambda i: (0, 0, 0, 0))],
        out_specs=[pl.BlockSpec((tm, d // 2), lambda i: (i, 0)),
                   pl.BlockSpec((tm, PEER_PICKS), lambda i: (i, 0)),
                   pl.BlockSpec((tm, PEER_PICKS), lambda i: (i, 0))],
        out_shape=[jax.ShapeDtypeStruct((t, d // 2), jnp.int32),
                   jax.ShapeDtypeStruct((t, PEER_PICKS), jnp.int32),
                   jax.ShapeDtypeStruct((t, PEER_PICKS), F32)],
        compiler_params=_params("parallel"),
        name="peer_route",
    )(h, gain.reshape(1, d), wq_bf, keys_bf)


def _sc_mesh():
    return plsc.VectorSubcoreMesh(core_axis_name="core", subcore_axis_name="subcore")


def _sc_worker():
    return lax.axis_index("subcore") * SC_CORES + lax.axis_index("core")


def _sc_row_pipeline(tab_hbm, idx_v, buf, sems, compute):
    nslot, nrow = buf.shape[0], buf.shape[1]
    nsub = PEER_PICKS // nrow
    ng = SC_TOKENS * nsub

    def gather(g, slot):
        rows = idx_v.at[g // nsub, pl.ds((g % nsub) * nrow, nrow)]
        return pltpu.make_async_copy(tab_hbm.at[rows], buf.at[slot], sems.at[slot])

    for b in range(nslot - 1):
        gather(b, b).start()

    @pl.loop(0, ng, step=nslot)
    def _(g):
        for b in range(nslot):
            ahead = g + b + nslot - 1

            @pl.when(ahead < ng)
            def _():
                gather(ahead, (b + nslot - 1) % nslot).start()

            gather(g + b, b).wait()
            compute(g + b, b)


def _sc_split(words):
    return (lax.bitcast_convert_type(words & HI_MASK, F32), lax.bitcast_convert_type(words << 16, F32))


def _expert_scores(table, eid, hn):
    t, words = hn.shape
    per_worker = t // SC_WORKERS
    nslot, nrow = SC_SCORE_RING
    nsub = PEER_PICKS // nrow
    ln = SC_LANES

    @functools.partial(
        pl.kernel, mesh=_sc_mesh(),
        out_type=jax.ShapeDtypeStruct((t, PEER_PICKS), F32),
        scratch_types=[pltpu.VMEM((SC_TOKENS, PEER_PICKS), jnp.int32),
                       pltpu.VMEM((SC_TOKENS, words), jnp.int32),
                       pltpu.VMEM((SC_TOKENS, PEER_PICKS), F32),
                       pltpu.VMEM((nslot, nrow, words), jnp.int32),
                       pltpu.SemaphoreType.DMA((nslot,))],
        compiler_params=pltpu.CompilerParams(needs_layout_passes=False),
        name="peer_expert_scores")
    def run(tab_hbm, eid_hbm, hn_hbm, out_hbm, idx_v, x_v, a_v, buf, sems):
        lane = lax.iota(jnp.int32, ln)

        def compute(g, slot):
            tok = g // nsub
            sub = g % nsub
            for grp in range(nrow // ln):
                def body(c, accs):
                    c0 = pl.ds(2 * c * ln, ln)
                    c1 = pl.ds((2 * c + 1) * ln, ln)
                    x0 = plsc.bitcast(x_v[tok, c0], BF16)
                    x1 = plsc.bitcast(x_v[tok, c1], BF16)
                    out = []
                    for r in range(ln):
                        u0 = plsc.bitcast(buf[slot, grp * ln + r, c0], BF16)
                        u1 = plsc.bitcast(buf[slot, grp * ln + r, c1], BF16)
                        hi, lo = _sc_split(plsc.bitcast(u0 * x0 + u1 * x1, jnp.int32))
                        out.append(accs[r] + hi + lo)
                    return tuple(out)

                accs = lax.fori_loop(0, words // (2 * ln), body, tuple(jnp.zeros((ln,), F32) for _ in range(ln)))
                res = jnp.zeros((ln,), F32)
                for r in range(ln):
                    res = jnp.where(lane == r, jnp.sum(accs[r]), res)
                a_v[tok, pl.ds(sub * nrow + grp * ln, ln)] = res

        @pl.loop(0, per_worker // SC_TOKENS)
        def _(blk):
            tok0 = _sc_worker() * per_worker + blk * SC_TOKENS
            pltpu.sync_copy(eid_hbm.at[pl.ds(tok0, SC_TOKENS)], idx_v)
            pltpu.sync_copy(hn_hbm.at[pl.ds(tok0, SC_TOKENS)], x_v)
            _sc_row_pipeline(tab_hbm, idx_v, buf, sems, compute)
            pltpu.sync_copy(a_v, out_hbm.at[pl.ds(tok0, SC_TOKENS)])

    return run(table, eid, hn)


def _expert_mix(table, eid, w):
    t = eid.shape[0]
    words = table.shape[1]
    d = 2 * words
    per_worker = t // SC_WORKERS
    nslot, nrow = SC_MIX_RING
    nsub = PEER_PICKS // nrow
    ln = SC_LANES
    nvec = SC_WORDS // ln

    @functools.partial(
        pl.kernel, mesh=_sc_mesh(),
        out_type=jax.ShapeDtypeStruct((t, d), F32),
        scratch_types=[pltpu.VMEM((SC_TOKENS, PEER_PICKS), jnp.int32),
                       pltpu.VMEM((SC_TOKENS, PEER_PICKS), jnp.int32),
                       pltpu.VMEM((SC_TOKENS, d), F32),
                       pltpu.VMEM((nslot, nrow, words), jnp.int32),
                       pltpu.SemaphoreType.DMA((nslot,))],
        compiler_params=pltpu.CompilerParams(needs_layout_passes=False),
        name="peer_expert_mix")
    def run(tab_hbm, eid_hbm, w_hbm, out_hbm, idx_v, w_v, y_v, buf, sems):
        zero = jnp.zeros((ln,), F32)

        def compute(g, slot):
            tok = g // nsub
            sub = g % nsub
            tokv = jnp.full((ln,), tok, jnp.int32)

            def weight(r):
                pick = jnp.full((ln,), sub * nrow + r, jnp.int32)
                return plsc.bitcast(plsc.load_gather(w_v, [tokv, pick]), BF16)

            for cc in range(words // SC_WORDS):
                def body(rg, accs):
                    r0 = SC_MIX_GROUP * rg
                    ws = [weight(r0 + j) for j in range(SC_MIX_GROUP)]
                    out = list(accs)
                    for k in range(nvec):
                        cols = pl.ds(cc * SC_WORDS + k * ln, ln)
                        prod = ws[0] * plsc.bitcast(buf[slot, r0, cols], BF16)
                        for j in range(1, SC_MIX_GROUP):
                            prod = prod + ws[j] * plsc.bitcast(buf[slot, r0 + j, cols], BF16)
                        hi, lo = _sc_split(plsc.bitcast(prod, jnp.int32))
                        out[k] = accs[k] + hi
                        out[nvec + k] = accs[nvec + k] + lo
                    return tuple(out)

                lo_cols = [pl.ds(cc * SC_WORDS + k * ln, ln) for k in range(nvec)]
                hi_cols = [pl.ds(words + cc * SC_WORDS + k * ln, ln) for k in range(nvec)]
                init = tuple(y_v[tok, c] for c in lo_cols + hi_cols)
                accs = lax.fori_loop(0, nrow // SC_MIX_GROUP, body, init)
                for c, acc in zip(lo_cols + hi_cols, accs):
                    y_v[tok, c] = acc

        @pl.loop(0, per_worker // SC_TOKENS)
        def _(blk):
            tok0 = _sc_worker() * per_worker + blk * SC_TOKENS
            pltpu.sync_copy(eid_hbm.at[pl.ds(tok0, SC_TOKENS)], idx_v)
            pltpu.sync_copy(w_hbm.at[pl.ds(tok0, SC_TOKENS)], w_v)

            @pl.loop(0, SC_TOKENS)
            def _(tok):
                @pl.loop(0, d // ln)
                def _(c):
                    y_v[tok, pl.ds(c * ln, ln)] = zero

            _sc_row_pipeline(tab_hbm, idx_v, buf, sems, compute)
            pltpu.sync_copy(y_v, out_hbm.at[pl.ds(tok0, SC_TOKENS)])

    return run(table, eid, w)


def _pick_weights_body(a_ref, g_ref, o_ref):
    o_ref[...] = _pack_twice(g_ref[...] * _gelu(a_ref[...]))


def _pick_weights(a, gate, tm):
    t = a.shape[0]
    spec = pl.BlockSpec((tm, PEER_PICKS), lambda i: (i, 0))
    return pl.pallas_call(
        _pick_weights_body,
        grid=(t // tm,),
        in_specs=[spec, spec],
        out_specs=spec,
        out_shape=jax.ShapeDtypeStruct(a.shape, jnp.int32),
        compiler_params=_params("parallel"),
        name="peer_pick_weights",
    )(a, gate)


def _ple_body(h_ref, y_ref, g_ref, wg_ref, p_ref, wp_ref, o_ref):
    h = h_ref[...] + y_ref[...]
    hn = _rms(h, g_ref[...]).astype(BF16)
    gate = _sigmoid(jnp.dot(hn, wg_ref[...], preferred_element_type=F32))
    emb = jnp.dot(p_ref[...].astype(BF16), wp_ref[...], preferred_element_type=F32)
    o_ref[...] = h + gate * emb


def _ple(h, y, gain, wg_bf, p, wp_bf, tm):
    t, d = h.shape
    return pl.pallas_call(
        _ple_body,
        grid=(t // tm,),
        in_specs=[pl.BlockSpec((tm, d), lambda i: (i, 0)),
                  pl.BlockSpec((tm, d), lambda i: (i, 0)),
                  pl.BlockSpec((1, d), lambda i: (0, 0)),
                  pl.BlockSpec(wg_bf.shape, lambda i: (0, 0)),
                  pl.BlockSpec((tm, p.shape[1]), lambda i: (i, 0)),
                  pl.BlockSpec(wp_bf.shape, lambda i: (0, 0))],
        out_specs=pl.BlockSpec((tm, d), lambda i: (i, 0)),
        out_shape=jax.ShapeDtypeStruct((t, d), F32),
        compiler_params=_params("parallel"),
        name="layer_embedding",
    )(h, y, gain.reshape(1, d), wg_bf, p, wp_bf)


def _rope_tables(seq, dim):
    inv = 1.0 / (ROPE_THETA ** (jnp.arange(0, dim, 2, dtype=F32) / dim))
    ang = jnp.arange(seq, dtype=F32)[:, None] * inv[None, :]
    return jnp.cos(ang), jnp.sin(ang)


def _pack_table(tab):
    n = tab.shape[1] // 2
    b = lax.bitcast_convert_type(tab.astype(BF16), jnp.uint16).astype(jnp.uint32)
    return lax.bitcast_convert_type((b[:, :n] << 16) | b[:, n:], jnp.int32)


def _permute_in_columns(w_in):
    return jnp.concatenate([w_in[:, 3840:], w_in[:, :3840]], axis=1)


def kernel(x, p, norm_mix, w_in, ret_decay, ret_norm, sgu_ln_g, sgu_ln_b, sgu_w, sgu_b, att_q_norm, att_k_norm, att_sink, w_proj_ret, w_proj_sgu, w_proj_att, w_out, norm_ffn, peer_wq, peer_keys, peer_u, peer_v, norm_ple, ple_gate, ple_proj):
    batch, seq, d = x.shape
    depth = w_in.shape[0]
    groups = PIPELINE_GROUPS if batch % PIPELINE_GROUPS == 0 else 1
    gb = batch // groups
    t = gb * seq
    ts = min(512, seq)
    tm = min(512, t)
    cos_r, sin_r = _rope_tables(seq, RET_DK)
    cos_r2 = jnp.concatenate([cos_r, cos_r], axis=1)
    sin_r2 = jnp.concatenate([-sin_r, sin_r], axis=1)
    cos_a, sin_a = _rope_tables(seq, ATT_DH)
    hs = [x[g * gb:(g + 1) * gb].reshape(t, d) for g in range(groups)]
    for i in range(depth):
        w_in_bf = _permute_in_columns(w_in[i]).astype(BF16)
        w_r, w_s, w_a = w_proj_ret[i].astype(BF16), w_proj_sgu[i].astype(BF16), w_proj_att[i].astype(BF16)
        w_o, w_q, keys = w_out[i].astype(BF16), peer_wq[i].astype(BF16), peer_keys[i].astype(BF16)
        w_g, w_p = ple_gate[i].astype(BF16), ple_proj[i].astype(BF16)
        tab_u, tab_v = _pack_table(peer_u[i]), _pack_table(peer_v[i])
        for g in range(groups):
            h = hs[g]
            z = _in_projection(h, norm_mix[i], w_in_bf, tm)
            y_r = _retention(z, ret_decay[i], ret_norm[i], cos_r2, sin_r2, gb, seq, ts)
            y_s = _sgu(z, sgu_ln_g[i], sgu_ln_b[i], sgu_w[i], sgu_b[i], ts)
            y_a = _attention(z, att_q_norm[i], att_k_norm[i], att_sink[i], cos_a, sin_a, gb, seq, ts)
            h = _merge(h, z, y_r, y_s, y_a, w_r, w_s, w_a, w_o, tm)
            hn, eid, gate = _route(h, norm_ffn[i], w_q, keys, min(256, t))
            a = _expert_scores(tab_u, eid, hn)
            w = _pick_weights(a, gate, tm)
            y = _expert_mix(tab_v, eid, w)
            p_g = p[i, g * gb:(g + 1) * gb].reshape(t, -1)
            hs[g] = _ple(h, y, norm_ple[i], w_g, p_g, w_p, tm)
    return jnp.concatenate(hs, axis=0).reshape(batch, seq, d)
```

```python
import functools
import math

import jax
import jax.numpy as jnp
from jax import lax
from jax.experimental import pallas as pl
from jax.experimental.pallas import tpu as pltpu
from jax.experimental.pallas import tpu_sc as plsc

F32 = jnp.float32
BF16 = jnp.bfloat16

D_MODEL = 1024
PLE_DIM = 256
CHUNK = 128
EPS = 1e-6
ROPE_THETA = 10000.0
RET_HEADS = 4
RET_DK = 128
SGU_GROUPS = 4
SGU_WIDTH = 512
ATT_HEADS = 8
ATT_KV_HEADS = 2
ATT_DH = 64
ATT_GROUP = ATT_HEADS // ATT_KV_HEADS
NEG_INF = -1e30
PEER_HEADS = 8
PEER_KEYS = 128
PEER_QDIM = 256
PEER_TOPK = 16
PEER_PICKS = PEER_HEADS * PEER_TOPK

Z_GATE_R, Z_GATE_S, Z_GATE_A = 0, 1024, 2048
Z_QR, Z_KR, Z_VR, Z_GR = 3072, 3584, 4096, 4608
Z_US, Z_VS = 5120, 5632
Z_QA, Z_KA, Z_VA = 6144, 6656, 6784
D_IN = 6912

LANE = 128
SC_CORES = 2
SC_SUBCORES = 16
SC_LANES = 16
SC_WORKERS = SC_CORES * SC_SUBCORES
SC_TOKENS = 32
SC_SCORE_RING = (4, 32)
SC_MIX_RING = (2, 64)
SC_MIX_GROUP = 4
SC_WORDS = 128
HI_MASK = -65536
VMEM_LIMIT = 56 * 1024 * 1024
PIPELINE_GROUPS = 8


def _params(*sem):
    return pltpu.CompilerParams(dimension_semantics=sem, vmem_limit_bytes=VMEM_LIMIT)


def _gelu(x):
    return 0.5 * x * (1.0 + lax.erf(x * (1.0 / math.sqrt(2.0))))


def _sigmoid(x):
    return 1.0 / (1.0 + jnp.exp(-x))


def _rms(x, g):
    return x * lax.rsqrt(jnp.mean(x * x, axis=-1, keepdims=True) + EPS) * g


def _bf16_hi_bits(x):
    b = lax.bitcast_convert_type(x, jnp.int32)
    return (b + 0x7FFF + ((b >> 16) & 1)) & HI_MASK


def _pack_halves(x):
    n = x.shape[1] // 2
    return _bf16_hi_bits(x[:, :n]) | lax.shift_right_logical(_bf16_hi_bits(x[:, n:]), 16)


def _pack_twice(x):
    b = _bf16_hi_bits(x)
    return b | lax.shift_right_logical(b, 16)


INPROJ_COLS = 1152


def _inproj_body(x_ref, g_ref, w_ref, o_ref):
    xn = _rms(x_ref[...], g_ref[...]).astype(BF16)
    for j in range(w_ref.shape[1] // INPROJ_COLS):
        cols = slice(j * INPROJ_COLS, (j + 1) * INPROJ_COLS)
        o_ref[:, cols] = jnp.dot(xn, w_ref[:, cols], preferred_element_type=F32).astype(o_ref.dtype)


def _in_projection(h, gain, w_bf, tm):
    t, d = h.shape
    n = w_bf.shape[1]
    return pl.pallas_call(
        _inproj_body,
        grid=(t // tm,),
        in_specs=[pl.BlockSpec((tm, d), lambda i: (i, 0)),
                  pl.BlockSpec((1, d), lambda i: (0, 0)),
                  pl.BlockSpec((d, n), lambda i: (0, 0))],
        out_specs=pl.BlockSpec((tm, n), lambda i: (i, 0)),
        out_shape=jax.ShapeDtypeStruct((t, n), BF16),
        compiler_params=_params("parallel"),
        name="in_projection",
    )(h, gain.reshape(1, d), w_bf)


def _rope128(x, cos, sin_signed):
    return x * cos + pltpu.roll(x, 64, 1) * sin_signed


def _ret_bwd_body(q_ref, k_ref, v_ref, cos_ref, sin_ref, qw_ref, kw_ref, cd_ref, o_ref, st_ref, *, nchunk):
    @pl.when(pl.program_id(2) == 0)
    def _():
        st_ref[...] = jnp.zeros_like(st_ref)

    qw = qw_ref[0]
    kw = kw_ref[0]
    cd = cd_ref[0, 0:1, :]
    for c in reversed(range(nchunk)):
        rows = pl.ds(c * CHUNK, CHUNK)
        cos = cos_ref[rows, :]
        sin = sin_ref[rows, :]
        q = _rope128(q_ref[rows, :].astype(F32), cos, sin)
        k = _rope128(k_ref[rows, :].astype(F32), cos, sin) * (RET_DK ** -0.5)
        v = v_ref[rows, :]
        st = st_ref[...]
        o_ref[rows, :] = jnp.dot((q * qw).astype(BF16), st.astype(BF16), preferred_element_type=F32)
        kv = jnp.dot((k * kw).T.astype(BF16), v, preferred_element_type=F32)
        st_ref[...] = st * cd + kv


def _ret_fwd_body(q_ref, k_ref, v_ref, g_ref, yb_ref, cos_ref, sin_ref, dm_ref, qw_ref, kw_ref, cd_ref,
                  gn_ref, o_ref, st_ref, *, nchunk):
    @pl.when(pl.program_id(2) == 0)
    def _():
        st_ref[...] = jnp.zeros_like(st_ref)

    qw = qw_ref[0]
    kw = kw_ref[0]
    cd = cd_ref[0, 0:1, :]
    dm = dm_ref[0]
    gn = gn_ref[0, 0:1, :]
    for c in range(nchunk):
        rows = pl.ds(c * CHUNK, CHUNK)
        cos = cos_ref[rows, :]
        sin = sin_ref[rows, :]
        q = _rope128(q_ref[rows, :].astype(F32), cos, sin)
        k = _rope128(k_ref[rows, :].astype(F32), cos, sin) * (RET_DK ** -0.5)
        v = v_ref[rows, :]
        st = st_ref[...]
        s = lax.dot_general(q.astype(BF16), k.astype(BF16), (((1,), (1,)), ((), ())),
                            preferred_element_type=F32) * dm
        y = jnp.dot(s.astype(BF16), v, preferred_element_type=F32)
        y += jnp.dot((q * qw).astype(BF16), st.astype(BF16), preferred_element_type=F32)
        y += yb_ref[rows, :]
        kv = jnp.dot((k * kw).T.astype(BF16), v, preferred_element_type=F32)
        st_ref[...] = st * cd + kv
        y = y * lax.rsqrt(jnp.mean(y * y, axis=-1, keepdims=True) + EPS) * gn
        g = g_ref[rows, :].astype(F32)
        o_ref[rows, :] = (g * _sigmoid(g) * y).astype(o_ref.dtype)


def _retention(z, ret_decay, ret_norm, cos, sin, batch, seq, ts):
    t = z.shape[0]
    nchunk = ts // CHUNK
    nstep = seq // ts
    hd = RET_HEADS
    log_g = jax.nn.log_sigmoid(ret_decay.astype(F32))
    idx = jnp.arange(CHUNK, dtype=F32)
    diff = idx[:, None] - idx[None, :]
    lf = log_g[0][:, None, None]
    lb = log_g[1][:, None, None]
    dmat = jnp.where(diff[None] >= 0, jnp.exp(lf * jnp.maximum(diff, 0.0)[None]),
                     jnp.exp(lb * jnp.maximum(-diff, 0.0)[None]))
    bc = lambda a: jnp.broadcast_to(a[:, :, None], (hd, a.shape[1], LANE))
    qw_f = bc(jnp.exp(log_g[0][:, None] * (idx + 1.0)[None, :]))
    kw_f = bc(jnp.exp(log_g[0][:, None] * (CHUNK - 1 - idx)[None, :]))
    qw_b = bc(jnp.exp(log_g[1][:, None] * (CHUNK - idx)[None, :]))
    kw_b = bc(jnp.exp(log_g[1][:, None] * idx[None, :]))
    cd_f = jnp.broadcast_to(jnp.exp(log_g[0] * CHUNK)[:, None, None], (hd, 8, LANE))
    cd_b = jnp.broadcast_to(jnp.exp(log_g[1] * CHUNK)[:, None, None], (hd, 8, LANE))
    gn = jnp.broadcast_to(ret_norm.astype(F32).reshape(hd, 1, LANE), (hd, 8, LANE))

    def zspec(col0, rev):
        cb = col0 // LANE
        if rev:
            return pl.BlockSpec((ts, LANE), lambda b, h, s: (b * nstep + nstep - 1 - s, cb + h))
        return pl.BlockSpec((ts, LANE), lambda b, h, s: (b * nstep + s, cb + h))

    def tspec(rev):
        if rev:
            return pl.BlockSpec((ts, LANE), lambda b, h, s: (nstep - 1 - s, 0))
        return pl.BlockSpec((ts, LANE), lambda b, h, s: (s, 0))

    hspec = lambda r: pl.BlockSpec((1, r, LANE), lambda b, h, s: (h, 0, 0))

    yb = pl.pallas_call(
        functools.partial(_ret_bwd_body, nchunk=nchunk),
        grid=(batch, hd, nstep),
        in_specs=[zspec(Z_QR, True), zspec(Z_KR, True), zspec(Z_VR, True), tspec(True), tspec(True),
                  hspec(CHUNK), hspec(CHUNK), hspec(8)],
        out_specs=pl.BlockSpec((ts, LANE), lambda b, h, s: (b * nstep + nstep - 1 - s, h)),
        out_shape=jax.ShapeDtypeStruct((t, hd * LANE), F32),
        scratch_shapes=[pltpu.VMEM((RET_DK, LANE), F32)],
        compiler_params=_params("parallel", "parallel", "arbitrary"),
        name="retention_bwd",
    )(z, z, z, cos, sin, qw_b, kw_b, cd_b)

    return pl.pallas_call(
        functools.partial(_ret_fwd_body, nchunk=nchunk),
        grid=(batch, hd, nstep),
        in_specs=[zspec(Z_QR, False), zspec(Z_KR, False), zspec(Z_VR, False), zspec(Z_GR, False),
                  pl.BlockSpec((ts, LANE), lambda b, h, s: (b * nstep + s, h)),
                  tspec(False), tspec(False), hspec(CHUNK), hspec(CHUNK), hspec(CHUNK), hspec(8), hspec(8)],
        out_specs=pl.BlockSpec((ts, LANE), lambda b, h, s: (b * nstep + s, h)),
        out_shape=jax.ShapeDtypeStruct((t, hd * LANE), BF16),
        scratch_shapes=[pltpu.VMEM((RET_DK, LANE), F32)],
        compiler_params=_params("parallel", "parallel", "arbitrary"),
        name="retention_fwd",
    )(z, z, z, z, yb, cos, sin, dmat, qw_f, kw_f, cd_f, gn)


def _sgu_body(u_ref, v_ref, lg_ref, lb_ref, w_ref, b_ref, o_ref, *, nchunk):
    lg = lg_ref[...]
    lb = lb_ref[...]
    for c in range(nchunk):
        rows = pl.ds(c * CHUNK, CHUNK)
        vf = _gelu(v_ref[rows, :].astype(F32))
        mu = jnp.mean(vf, axis=-1, keepdims=True)
        vc = vf - mu
        var = jnp.mean(vc * vc, axis=-1, keepdims=True)
        vn = (vc * lax.rsqrt(var + EPS) * lg + lb).astype(BF16)
        for g in range(SGU_GROUPS):
            cols = slice(g * LANE, (g + 1) * LANE)
            mixed = jnp.dot(w_ref[g], vn[:, cols], preferred_element_type=F32) + b_ref[g]
            uf = _gelu(u_ref[rows, cols].astype(F32))
            o_ref[rows, cols] = (uf * mixed).astype(o_ref.dtype)


def _sgu(z, ln_g, ln_b, w_s, b_s, ts):
    t = z.shape[0]
    bias = jnp.broadcast_to(b_s.astype(F32)[:, :, None], (SGU_GROUPS, CHUNK, LANE))
    return pl.pallas_call(
        functools.partial(_sgu_body, nchunk=ts // CHUNK),
        grid=(t // ts,),
        in_specs=[pl.BlockSpec((ts, SGU_WIDTH), lambda i: (i, Z_US // SGU_WIDTH)),
                  pl.BlockSpec((ts, SGU_WIDTH), lambda i: (i, Z_VS // SGU_WIDTH)),
                  pl.BlockSpec((1, SGU_WIDTH), lambda i: (0, 0)),
                  pl.BlockSpec((1, SGU_WIDTH), lambda i: (0, 0)),
                  pl.BlockSpec((SGU_GROUPS, CHUNK, CHUNK), lambda i: (0, 0, 0)),
                  pl.BlockSpec((SGU_GROUPS, CHUNK, LANE), lambda i: (0, 0, 0))],
        out_specs=pl.BlockSpec((ts, SGU_WIDTH), lambda i: (i, 0)),
        out_shape=jax.ShapeDtypeStruct((t, SGU_WIDTH), BF16),
        compiler_params=_params("parallel"),
        name="spatial_gating",
    )(z, z, ln_g.reshape(1, -1).astype(F32), ln_b.reshape(1, -1).astype(F32), w_s.astype(BF16), bias)


def _pair_norm_rope(x, gain, cos, sin_up, sin_dn, low):
    sq = x * x
    lo = jnp.sum(jnp.where(low, sq, 0.0), axis=-1, keepdims=True)
    hi = jnp.sum(sq, axis=-1, keepdims=True) - lo
    ms = jnp.where(low, lo, hi) * (1.0 / ATT_DH)
    xn = x * lax.rsqrt(ms + EPS) * gain
    return xn * cos + pltpu.roll(xn, LANE - 32, 1) * sin_up + pltpu.roll(xn, 32, 1) * sin_dn


def _attn_body(sink_ref, q_ref, kp_ref, k_ref, kn_ref, vp_ref, v_ref, vn_ref,
               cq_ref, suq_ref, sdq_ref, ckp_ref, sukp_ref, sdkp_ref, ckn_ref, sukn_ref, sdkn_ref,
               qg_ref, kg_ref, o_ref, *, nchunk, nstep):
    s_id = pl.program_id(1)
    ts = nchunk * CHUNK
    lane = lax.broadcasted_iota(jnp.int32, (1, LANE), 1)
    low = lane < ATT_DH
    kg = kg_ref[...]
    qg = qg_ref[...]
    k_ext = jnp.concatenate([
        _pair_norm_rope(kp_ref[...].astype(F32), kg, ckp_ref[...], sukp_ref[...], sdkp_ref[...], low),
        _pair_norm_rope(k_ref[...].astype(F32), kg, cq_ref[...], suq_ref[...], sdq_ref[...], low),
        _pair_norm_rope(kn_ref[...].astype(F32), kg, ckn_ref[...], sukn_ref[...], sdkn_ref[...], low)], axis=0)
    v_ext = jnp.concatenate([vp_ref[...], v_ref[...], vn_ref[...]], axis=0)
    k_lo = jnp.where(low, k_ext, 0.0).astype(BF16)
    k_hi = jnp.where(low, 0.0, k_ext).astype(BF16)
    qi = lax.broadcasted_iota(jnp.int32, (CHUNK, 3 * CHUNK), 0)
    kj = lax.broadcasted_iota(jnp.int32, (CHUNK, 3 * CHUNK), 1)
    band = jnp.abs(qi + CHUNK - kj) <= CHUNK
    for c in range(nchunk):
        rows = pl.ds(c * CHUNK, CHUNK)
        first = jnp.logical_and(s_id == 0, c == 0)
        last = jnp.logical_and(s_id == nstep - 1, c == nchunk - 1)
        valid = band
        if c == 0:
            valid = jnp.logical_and(valid, jnp.logical_or(kj >= CHUNK, jnp.logical_not(first)))
        if c == nchunk - 1:
            valid = jnp.logical_and(valid, jnp.logical_or(kj < 2 * CHUNK, jnp.logical_not(last)))
        kc_lo = k_lo[c * CHUNK:(c + 3) * CHUNK]
        kc_hi = k_hi[c * CHUNK:(c + 3) * CHUNK]
        vc = v_ext[c * CHUNK:(c + 3) * CHUNK]
        cos = cq_ref[rows, :]
        su = suq_ref[rows, :]
        sd = sdq_ref[rows, :]
        for pair in range(ATT_HEADS // 2):
            cols = slice(pair * LANE, (pair + 1) * LANE)
            qp = _pair_norm_rope(q_ref[rows, cols].astype(F32), qg, cos, su, sd, low) * (ATT_DH ** -0.5)
            kv_head = (2 * pair) // ATT_GROUP
            outs = []
            for half in range(2):
                head = 2 * pair + half
                qh = qp if half == kv_head else pltpu.roll(qp, ATT_DH, 1)
                if kv_head == 0:
                    qh = jnp.where(low, qh, 0.0)
                    kc = kc_lo
                else:
                    qh = jnp.where(low, 0.0, qh)
                    kc = kc_hi
                s = lax.dot_general(qh.astype(BF16), kc, (((1,), (1,)), ((), ())), preferred_element_type=F32)
                s = jnp.where(valid, s, NEG_INF)
                sk = sink_ref[head]
                m = jnp.maximum(jnp.max(s, axis=-1, keepdims=True), sk)
                e = jnp.exp(s - m)
                den = jnp.sum(e, axis=-1, keepdims=True) + jnp.exp(sk - m)
                o = jnp.dot(e.astype(BF16), vc, preferred_element_type=F32) / den
                outs.append(o if half == kv_head else pltpu.roll(o, ATT_DH, 1))
            o_ref[rows, cols] = jnp.where(low, outs[0], outs[1]).astype(o_ref.dtype)


def _attention(z, q_gain, k_gain, sink, cos, sin, batch, seq, ts):
    t = z.shape[0]
    nchunk = ts // CHUNK
    nstep = seq // ts
    nblk = seq // CHUNK
    cos2 = jnp.tile(jnp.concatenate([cos, cos], axis=1), (1, 2))
    zero = jnp.zeros_like(sin)
    sin_up = jnp.tile(jnp.concatenate([-sin, zero], axis=1), (1, 2))
    sin_dn = jnp.tile(jnp.concatenate([zero, sin], axis=1), (1, 2))
    qg = jnp.tile(q_gain.astype(F32), 2).reshape(1, LANE)
    kg = jnp.tile(k_gain.astype(F32), 2).reshape(1, LANE)

    kcb, vcb = Z_KA // LANE, Z_VA // LANE
    prev_blk = lambda s: jnp.maximum(s * nchunk - 1, 0)
    next_blk = lambda s: jnp.minimum((s + 1) * nchunk, nblk - 1)
    main = lambda cb: pl.BlockSpec((ts, LANE), lambda b, s: (b * nstep + s, cb))
    prev = lambda cb: pl.BlockSpec((CHUNK, LANE), lambda b, s: (b * nblk + prev_blk(s), cb))
    nxt = lambda cb: pl.BlockSpec((CHUNK, LANE), lambda b, s: (b * nblk + next_blk(s), cb))
    tmain = pl.BlockSpec((ts, LANE), lambda b, s: (s, 0))
    tprev = pl.BlockSpec((CHUNK, LANE), lambda b, s: (prev_blk(s), 0))
    tnext = pl.BlockSpec((CHUNK, LANE), lambda b, s: (next_blk(s), 0))
    one = pl.BlockSpec((1, LANE), lambda b, s: (0, 0))
    return pl.pallas_call(
        functools.partial(_attn_body, nchunk=nchunk, nstep=nstep),
        grid=(batch, nstep),
        in_specs=[pl.BlockSpec(memory_space=pltpu.SMEM),
                  pl.BlockSpec((ts, ATT_HEADS * ATT_DH), lambda b, s: (b * nstep + s, Z_QA // 512)),
                  prev(kcb), main(kcb), nxt(kcb), prev(vcb), main(vcb), nxt(vcb),
                  tmain, tmain, tmain, tprev, tprev, tprev, tnext, tnext, tnext, one, one],
        out_specs=pl.BlockSpec((ts, ATT_HEADS * ATT_DH), lambda b, s: (b * nstep + s, 0)),
        out_shape=jax.ShapeDtypeStruct((t, ATT_HEADS * ATT_DH), BF16),
        compiler_params=_params("parallel", "arbitrary"),
        name="window_attention",
    )(sink.astype(F32), z, z, z, z, z, z, z,
      cos2, sin_up, sin_dn, cos2, sin_up, sin_dn, cos2, sin_up, sin_dn, qg, kg)


def _merge_body(h_ref, gr_ref, gs_ref, ga_ref, yr_ref, ys_ref, ya_ref, wr_ref, ws_ref, wa_ref, wo_ref, o_ref):
    m = _sigmoid(gr_ref[...].astype(F32)) * jnp.dot(yr_ref[...], wr_ref[...], preferred_element_type=F32)
    m += _sigmoid(gs_ref[...].astype(F32)) * jnp.dot(ys_ref[...], ws_ref[...], preferred_element_type=F32)
    m += _sigmoid(ga_ref[...].astype(F32)) * jnp.dot(ya_ref[...], wa_ref[...], preferred_element_type=F32)
    o_ref[...] = h_ref[...] + jnp.dot(m.astype(BF16), wo_ref[...], preferred_element_type=F32)


def _merge(h, z, y_r, y_s, y_a, w_r, w_s, w_a, w_o, tm):
    t, d = h.shape
    row = lambda w: pl.BlockSpec((tm, w), lambda i: (i, 0))
    gate = lambda col0: pl.BlockSpec((tm, d), lambda i: (i, col0 // d))
    full = lambda a: pl.BlockSpec(a.shape, lambda i: (0, 0))
    return pl.pallas_call(
        _merge_body,
        grid=(t // tm,),
        in_specs=[row(d), gate(Z_GATE_R), gate(Z_GATE_S), gate(Z_GATE_A), row(512), row(512), row(512),
                  full(w_r), full(w_s), full(w_a), full(w_o)],
        out_specs=row(d),
        out_shape=jax.ShapeDtypeStruct((t, d), F32),
        compiler_params=_params("parallel"),
        name="branch_merge",
    )(h, z, z, z, y_r, y_s, y_a, w_r, w_s, w_a, w_o)


ID_NONE = 1 << 20


def _top16(vals, ids):
    out_v, out_i = [], []
    for _ in range(PEER_TOPK):
        m = jnp.max(vals, axis=0, keepdims=True)
        idx = jnp.min(jnp.where(vals == m, ids, ID_NONE), axis=0, keepdims=True)
        out_v.append(m)
        out_i.append(idx)
        vals = jnp.where(ids == idx, -jnp.inf, vals)
    return out_v, out_i


def _pair_blocks(first, second):
    rows1, stack1 = first
    rows2, stack2 = second
    blocks = [(stack1, rows2[0])]
    blocks += [(stack1[0:8], rows2[b]) for b in range(1, 8)]
    blocks += [(rows1[0], stack2[8:16])]
    return blocks


def _pair_ids(tm):
    a8 = lax.broadcasted_iota(jnp.int32, (8, tm), 0)
    a16 = lax.broadcasted_iota(jnp.int32, (PEER_TOPK, tm), 0)
    blocks = [a16 * PEER_TOPK]
    blocks += [jnp.where(a8 < PEER_TOPK // (b + 1), a8 * PEER_TOPK + b, ID_NONE) for b in range(1, 8)]
    blocks += [a8 + 8]
    return jnp.concatenate(blocks, axis=0)


def _route_body(h_ref, g_ref, wq_ref, keys_ref, hn_ref, eid_ref, gate_ref):
    hn = _rms(h_ref[...], g_ref[...])
    hn_ref[...] = _pack_halves(hn)
    q = jnp.dot(hn.astype(BF16), wq_ref[...], preferred_element_type=F32).astype(BF16)
    tm = q.shape[0]
    rows = lax.broadcasted_iota(jnp.int32, (PEER_KEYS, tm), 0)
    pair_ids = _pair_ids(tm)
    pair_ok = pair_ids != ID_NONE
    half = PEER_QDIM // 2
    eids, gates = [], []
    for hd in range(PEER_HEADS):
        sub = []
        for p in range(2):
            qs = q[:, (2 * hd + p) * half:(2 * hd + p + 1) * half]
            s = lax.dot_general(keys_ref[hd, p], qs, (((1,), (1,)), ((), ())), preferred_element_type=F32)
            sub.append(_top16(s, rows))
        (s1, i1), (s2, i2) = sub
        scores = _pair_blocks((s1, jnp.concatenate(s1, axis=0)), (s2, jnp.concatenate(s2, axis=0)))
        experts = _pair_blocks((i1, jnp.concatenate(i1, axis=0)), (i2, jnp.concatenate(i2, axis=0)))
        cand_s = jnp.where(pair_ok, jnp.concatenate([a + b for a, b in scores], axis=0), -jnp.inf)
        cand_e = jnp.concatenate([a * PEER_KEYS + b for a, b in experts], axis=0)
        top_s, sel = _top16(cand_s, pair_ids)
        top_e = [jnp.sum(jnp.where(pair_ids == i, cand_e, 0), axis=0, keepdims=True) for i in sel]
        ts_ = jnp.concatenate(top_s, axis=0)
        e = jnp.exp(ts_ - top_s[0])
        gates.append(e / jnp.sum(e, axis=0, keepdims=True))
        eids.append(jnp.concatenate(top_e, axis=0))
    eid_ref[...] = jnp.concatenate(eids, axis=0).T
    gate_ref[...] = jnp.concatenate(gates, axis=0).T


def _route(h, gain, wq_bf, keys_bf, tm):
    t, d = h.shape
    return pl.pallas_call(
        _route_body,
        grid=(t // tm,),
        in_specs=[pl.BlockSpec((tm, d), lambda i: (i, 0)),
                  pl.BlockSpec((1, d), lambda i: (0, 0)),
                  pl.BlockSpec(wq_bf.shape, lambda i: (0, 0)),
                  pl.BlockSpec(keys_bf.shape, lambda i: (0, 0, 0, 0))],
        out_specs=[pl.BlockSpec((tm, d // 2), lambda i: (i, 0)),
                   pl.BlockSpec((tm, PEER_PICKS), lambda i: (i, 0)),
                   pl.BlockSpec((tm, PEER_PICKS), lambda i: (i, 0))],
        out_shape=[jax.ShapeDtypeStruct((t, d // 2), jnp.int32),
                   jax.ShapeDtypeStruct((t, PEER_PICKS), jnp.int32),
                   jax.ShapeDtypeStruct((t, PEER_PICKS), F32)],
        compiler_params=_params("parallel"),
        name="peer_route",
    )(h, gain.reshape(1, d), wq_bf, keys_bf)


def _sc_mesh():
    return plsc.VectorSubcoreMesh(core_axis_name="core", subcore_axis_name="subcore")


def _sc_worker():
    return lax.axis_index("subcore") * SC_CORES + lax.axis_index("core")


def _sc_row_pipeline(tab_hbm, idx_v, buf, sems, compute):
    nslot, nrow = buf.shape[0], buf.shape[1]
    nsub = PEER_PICKS // nrow
    ng = SC_TOKENS * nsub

    def gather(g, slot):
        rows = idx_v.at[g // nsub, pl.ds((g % nsub) * nrow, nrow)]
        return pltpu.make_async_copy(tab_hbm.at[rows], buf.at[slot], sems.at[slot])

    for b in range(nslot - 1):
        gather(b, b).start()

    @pl.loop(0, ng, step=nslot)
    def _(g):
        for b in range(nslot):
            ahead = g + b + nslot - 1

            @pl.when(ahead < ng)
            def _():
                gather(ahead, (b + nslot - 1) % nslot).start()

            gather(g + b, b).wait()
            compute(g + b, b)


def _sc_split(words):
    return (lax.bitcast_convert_type(words & HI_MASK, F32), lax.bitcast_convert_type(words << 16, F32))


def _expert_scores(table, eid, hn):
    t, words = hn.shape
    per_worker = t // SC_WORKERS
    nslot, nrow = SC_SCORE_RING
    nsub = PEER_PICKS // nrow
    ln = SC_LANES

    @functools.partial(
        pl.kernel, mesh=_sc_mesh(),
        out_type=jax.ShapeDtypeStruct((t, PEER_PICKS), F32),
        scratch_types=[pltpu.VMEM((SC_TOKENS, PEER_PICKS), jnp.int32),
                       pltpu.VMEM((SC_TOKENS, words), jnp.int32),
                       pltpu.VMEM((SC_TOKENS, PEER_PICKS), F32),
                       pltpu.VMEM((nslot, nrow, words), jnp.int32),
                       pltpu.SemaphoreType.DMA((nslot,))],
        compiler_params=pltpu.CompilerParams(needs_layout_passes=False),
        name="peer_expert_scores")
    def run(tab_hbm, eid_hbm, hn_hbm, out_hbm, idx_v, x_v, a_v, buf, sems):
        lane = lax.iota(jnp.int32, ln)

        def compute(g, slot):
            tok = g // nsub
            sub = g % nsub
            for grp in range(nrow // ln):
                def body(c, accs):
                    c0 = pl.ds(2 * c * ln, ln)
                    c1 = pl.ds((2 * c + 1) * ln, ln)
                    x0 = plsc.bitcast(x_v[tok, c0], BF16)
                    x1 = plsc.bitcast(x_v[tok, c1], BF16)
                    out = []
                    for r in range(ln):
                        u0 = plsc.bitcast(buf[slot, grp * ln + r, c0], BF16)
                        u1 = plsc.bitcast(buf[slot, grp * ln + r, c1], BF16)
                        hi, lo = _sc_split(plsc.bitcast(u0 * x0 + u1 * x1, jnp.int32))
                        out.append(accs[r] + hi + lo)
                    return tuple(out)

                accs = lax.fori_loop(0, words // (2 * ln), body, tuple(jnp.zeros((ln,), F32) for _ in range(ln)))
                res = jnp.zeros((ln,), F32)
                for r in range(ln):
                    res = jnp.where(lane == r, jnp.sum(accs[r]), res)
                a_v[tok, pl.ds(sub * nrow + grp * ln, ln)] = res

        @pl.loop(0, per_worker // SC_TOKENS)
        def _(blk):
            tok0 = _sc_worker() * per_worker + blk * SC_TOKENS
            pltpu.sync_copy(eid_hbm.at[pl.ds(tok0, SC_TOKENS)], idx_v)
            pltpu.sync_copy(hn_hbm.at[pl.ds(tok0, SC_TOKENS)], x_v)
            _sc_row_pipeline(tab_hbm, idx_v, buf, sems, compute)
            pltpu.sync_copy(a_v, out_hbm.at[pl.ds(tok0, SC_TOKENS)])

    return run(table, eid, hn)


def _expert_mix(table, eid, w):
    t = eid.shape[0]
    words = table.shape[1]
    d = 2 * words
    per_worker = t // SC_WORKERS
    nslot, nrow = SC_MIX_RING
    nsub = PEER_PICKS // nrow
    ln = SC_LANES
    nvec = SC_WORDS // ln

    @functools.partial(
        pl.kernel, mesh=_sc_mesh(),
        out_type=jax.ShapeDtypeStruct((t, d), F32),
        scratch_types=[pltpu.VMEM((SC_TOKENS, PEER_PICKS), jnp.int32),
                       pltpu.VMEM((SC_TOKENS, PEER_PICKS), jnp.int32),
                       pltpu.VMEM((SC_TOKENS, d), F32),
                       pltpu.VMEM((nslot, nrow, words), jnp.int32),
                       pltpu.SemaphoreType.DMA((nslot,))],
        compiler_params=pltpu.CompilerParams(needs_layout_passes=False),
        name="peer_expert_mix")
    def run(tab_hbm, eid_hbm, w_hbm, out_hbm, idx_v, w_v, y_v, buf, sems):
        zero = jnp.zeros((ln,), F32)

        def compute(g, slot):
            tok = g // nsub
            sub = g % nsub
            tokv = jnp.full((ln,), tok, jnp.int32)

            def weight(r):
                pick = jnp.full((ln,), sub * nrow + r, jnp.int32)
                return plsc.bitcast(plsc.load_gather(w_v, [tokv, pick]), BF16)

            for cc in range(words // SC_WORDS):
                def body(rg, accs):
                    r0 = SC_MIX_GROUP * rg
                    ws = [weight(r0 + j) for j in range(SC_MIX_GROUP)]
                    out = list(accs)
                    for k in range(nvec):
                        cols = pl.ds(cc * SC_WORDS + k * ln, ln)
                        prod = ws[0] * plsc.bitcast(buf[slot, r0, cols], BF16)
                        for j in range(1, SC_MIX_GROUP):
                            prod = prod + ws[j] * plsc.bitcast(buf[slot, r0 + j, cols], BF16)
                        hi, lo = _sc_split(plsc.bitcast(prod, jnp.int32))
                        out[k] = accs[k] + hi
                        out[nvec + k] = accs[nvec + k] + lo
                    return tuple(out)

                lo_cols = [pl.ds(cc * SC_WORDS + k * ln, ln) for k in range(nvec)]
                hi_cols = [pl.ds(words + cc * SC_WORDS + k * ln, ln) for k in range(nvec)]
                init = tuple(y_v[tok, c] for c in lo_cols + hi_cols)
                accs = lax.fori_loop(0, nrow // SC_MIX_GROUP, body, init)
                for c, acc in zip(lo_cols + hi_cols, accs):
                    y_v[tok, c] = acc

        @pl.loop(0, per_worker // SC_TOKENS)
        def _(blk):
            tok0 = _sc_worker() * per_worker + blk * SC_TOKENS
            pltpu.sync_copy(eid_hbm.at[pl.ds(tok0, SC_TOKENS)], idx_v)
            pltpu.sync_copy(w_hbm.at[pl.ds(tok0, SC_TOKENS)], w_v)

            @pl.loop(0, SC_TOKENS)
            def _(tok):
                @pl.loop(0, d // ln)
                def _(c):
                    y_v[tok, pl.ds(c * ln, ln)] = zero

            _sc_row_pipeline(tab_hbm, idx_v, buf, sems, compute)
            pltpu.sync_copy(y_v, out_hbm.at[pl.ds(tok0, SC_TOKENS)])

    return run(table, eid, w)


def _pick_weights_body(a_ref, g_ref, o_ref):
    o_ref[...] = _pack_twice(g_ref[...] * _gelu(a_ref[...]))


def _pick_weights(a, gate, tm):
    t = a.shape[0]
    spec = pl.BlockSpec((tm, PEER_PICKS), lambda i: (i, 0))
    return pl.pallas_call(
        _pick_weights_body,
        grid=(t // tm,),
        in_specs=[spec, spec],
        out_specs=spec,
        out_shape=jax.ShapeDtypeStruct(a.shape, jnp.int32),
        compiler_params=_params("parallel"),
        name="peer_pick_weights",
    )(a, gate)


def _ple_body(h_ref, y_ref, g_ref, wg_ref, p_ref, wp_ref, o_ref):
    h = h_ref[...] + y_ref[...]
    hn = _rms(h, g_ref[...]).astype(BF16)
    gate = _sigmoid(jnp.dot(hn, wg_ref[...], preferred_element_type=F32))
    emb = jnp.dot(p_ref[...].astype(BF16), wp_ref[...], preferred_element_type=F32)
    o_ref[...] = h + gate * emb


def _ple(h, y, gain, wg_bf, p, wp_bf, tm):
    t, d = h.shape
    return pl.pallas_call(
        _ple_body,
        grid=(t // tm,),
        in_specs=[pl.BlockSpec((tm, d), lambda i: (i, 0)),
                  pl.BlockSpec((tm, d), lambda i: (i, 0)),
                  pl.BlockSpec((1, d), lambda i: (0, 0)),
                  pl.BlockSpec(wg_bf.shape, lambda i: (0, 0)),
                  pl.BlockSpec((tm, p.shape[1]), lambda i: (i, 0)),
                  pl.BlockSpec(wp_bf.shape, lambda i: (0, 0))],
        out_specs=pl.BlockSpec((tm, d), lambda i: (i, 0)),
        out_shape=jax.ShapeDtypeStruct((t, d), F32),
        compiler_params=_params("parallel"),
        name="layer_embedding",
    )(h, y, gain.reshape(1, d), wg_bf, p, wp_bf)


def _rope_tables(seq, dim):
    inv = 1.0 / (ROPE_THETA ** (jnp.arange(0, dim, 2, dtype=F32) / dim))
    ang = jnp.arange(seq, dtype=F32)[:, None] * inv[None, :]
    return jnp.cos(ang), jnp.sin(ang)


def _pack_table(tab):
    n = tab.shape[1] // 2
    b = lax.bitcast_convert_type(tab.astype(BF16), jnp.uint16).astype(jnp.uint32)
    return lax.bitcast_convert_type((b[:, :n] << 16) | b[:, n:], jnp.int32)


def _permute_in_columns(w_in):
    return jnp.concatenate([w_in[:, 3840:], w_in[:, :3840]], axis=1)


def kernel(x, p, norm_mix, w_in, ret_decay, ret_norm, sgu_ln_g, sgu_ln_b, sgu_w, sgu_b, att_q_norm, att_k_norm, att_sink, w_proj_ret, w_proj_sgu, w_proj_att, w_out, norm_ffn, peer_wq, peer_keys, peer_u, peer_v, norm_ple, ple_gate, ple_proj):
    batch, seq, d = x.shape
    depth = w_in.shape[0]
    groups = PIPELINE_GROUPS if batch % PIPELINE_GROUPS == 0 else 1
    gb = batch // groups
    t = gb * seq
    ts = min(512, seq)
    tm = min(512, t)
    cos_r, sin_r = _rope_tables(seq, RET_DK)
    cos_r2 = jnp.concatenate([cos_r, cos_r], axis=1)
    sin_r2 = jnp.concatenate([-sin_r, sin_r], axis=1)
    cos_a, sin_a = _rope_tables(seq, ATT_DH)
    hs = [x[g * gb:(g + 1) * gb].reshape(t, d) for g in range(groups)]
    for i in range(depth):
        w_in_bf = _permute_in_columns(w_in[i]).astype(BF16)
        w_r, w_s, w_a = w_proj_ret[i].astype(BF16), w_proj_sgu[i].astype(BF16), w_proj_att[i].astype(BF16)
        w_o, w_q, keys = w_out[i].astype(BF16), peer_wq[i].astype(BF16), peer_keys[i].astype(BF16)
        w_g, w_p = ple_gate[i].astype(BF16), ple_proj[i].astype(BF16)
        tab_u, tab_v = _pack_table(peer_u[i]), _pack_table(peer_v[i])
        for g in range(groups):
            h = hs[g]
            z = _in_projection(h, norm_mix[i], w_in_bf, tm)
            y_r = _retention(z, ret_decay[i], ret_norm[i], cos_r2, sin_r2, gb, seq, ts)
            y_s = _sgu(z, sgu_ln_g[i], sgu_ln_b[i], sgu_w[i], sgu_b[i], ts)
            y_a = _attention(z, att_q_norm[i], att_k_norm[i], att_sink[i], cos_a, sin_a, gb, seq, ts)
            h = _merge(h, z, y_r, y_s, y_a, w_r, w_s, w_a, w_o, tm)
            hn, eid, gate = _route(h, norm_ffn[i], w_q, keys, min(256, t))
            a = _expert_scores(tab_u, eid, hn)
            w = _pick_weights(a, gate, tm)
            y = _expert_mix(tab_v, eid, w)
            p_g = p[i, g * gb:(g + 1) * gb].reshape(t, -1)
            hs[g] = _ple(h, y, norm_ple[i], w_g, p_g, w_p, tm)
    return jnp.concatenate(hs, axis=0).reshape(batch, seq, d)
```

```python
import functools
import math

import jax
import jax.numpy as jnp
from jax import lax
from jax.experimental import pallas as pl
from jax.experimental.pallas import tpu as pltpu
from jax.experimental.pallas import tpu_sc as plsc

F32 = jnp.float32
BF16 = jnp.bfloat16

D_MODEL = 1024
PLE_DIM = 256
CHUNK = 128
EPS = 1e-6
ROPE_THETA = 10000.0
RET_HEADS = 4
RET_DK = 128
SGU_GROUPS = 4
SGU_WIDTH = 512
ATT_HEADS = 8
ATT_KV_HEADS = 2
ATT_DH = 64
ATT_GROUP = ATT_HEADS // ATT_KV_HEADS
NEG_INF = -1e30
PEER_HEADS = 8
PEER_KEYS = 128
PEER_QDIM = 256
PEER_TOPK = 16
PEER_PICKS = PEER_HEADS * PEER_TOPK

Z_GATE_R, Z_GATE_S, Z_GATE_A = 0, 1024, 2048
Z_QR, Z_KR, Z_VR, Z_GR = 3072, 3584, 4096, 4608
Z_US, Z_VS = 5120, 5632
Z_QA, Z_KA, Z_VA = 6144, 6656, 6784
D_IN = 6912

LANE = 128
SC_CORES = 2
SC_SUBCORES = 16
SC_LANES = 16
SC_WORKERS = SC_CORES * SC_SUBCORES
SC_TOKENS = 32
SC_SCORE_RING = (4, 32)
SC_MIX_RING = (2, 64)
SC_MIX_GROUP = 4
SC_WORDS = 128
HI_MASK = -65536
VMEM_LIMIT = 56 * 1024 * 1024
PIPELINE_GROUPS = 8


def _params(*sem):
    return pltpu.CompilerParams(dimension_semantics=sem, vmem_limit_bytes=VMEM_LIMIT)


def _gelu(x):
    return 0.5 * x * (1.0 + lax.erf(x * (1.0 / math.sqrt(2.0))))


def _sigmoid(x):
    return 1.0 / (1.0 + jnp.exp(-x))


def _rms(x, g):
    return x * lax.rsqrt(jnp.mean(x * x, axis=-1, keepdims=True) + EPS) * g


def _bf16_hi_bits(x):
    b = lax.bitcast_convert_type(x, jnp.int32)
    return (b + 0x7FFF + ((b >> 16) & 1)) & HI_MASK


def _pack_halves(x):
    n = x.shape[1] // 2
    return _bf16_hi_bits(x[:, :n]) | lax.shift_right_logical(_bf16_hi_bits(x[:, n:]), 16)


def _pack_twice(x):
    b = _bf16_hi_bits(x)
    return b | lax.shift_right_logical(b, 16)


INPROJ_COLS = 1152


def _inproj_body(x_ref, g_ref, w_ref, o_ref):
    xn = _rms(x_ref[...], g_ref[...]).astype(BF16)
    for j in range(w_ref.shape[1] // INPROJ_COLS):
        cols = slice(j * INPROJ_COLS, (j + 1) * INPROJ_COLS)
        o_ref[:, cols] = jnp.dot(xn, w_ref[:, cols], preferred_element_type=F32).astype(o_ref.dtype)


def _in_projection(h, gain, w_bf, tm):
    t, d = h.shape
    n = w_bf.shape[1]
    return pl.pallas_call(
        _inproj_body,
        grid=(t // tm,),
        in_specs=[pl.BlockSpec((tm, d), lambda i: (i, 0)),
                  pl.BlockSpec((1, d), lambda i: (0, 0)),
                  pl.BlockSpec((d, n), lambda i: (0, 0))],
        out_specs=pl.BlockSpec((tm, n), lambda i: (i, 0)),
        out_shape=jax.ShapeDtypeStruct((t, n), BF16),
        compiler_params=_params("parallel"),
        name="in_projection",
    )(h, gain.reshape(1, d), w_bf)


def _rope128(x, cos, sin_signed):
    return x * cos + pltpu.roll(x, 64, 1) * sin_signed


def _ret_bwd_body(q_ref, k_ref, v_ref, cos_ref, sin_ref, qw_ref, kw_ref, cd_ref, o_ref, st_ref, *, nchunk):
    @pl.when(pl.program_id(2) == 0)
    def _():
        st_ref[...] = jnp.zeros_like(st_ref)

    qw = qw_ref[0]
    kw = kw_ref[0]
    cd = cd_ref[0, 0:1, :]
    for c in reversed(range(nchunk)):
        rows = pl.ds(c * CHUNK, CHUNK)
        cos = cos_ref[rows, :]
        sin = sin_ref[rows, :]
        q = _rope128(q_ref[rows, :].astype(F32), cos, sin)
        k = _rope128(k_ref[rows, :].astype(F32), cos, sin) * (RET_DK ** -0.5)
        v = v_ref[rows, :]
        st = st_ref[...]
        o_ref[rows, :] = jnp.dot((q * qw).astype(BF16), st.astype(BF16), preferred_element_type=F32)
        kv = jnp.dot((k * kw).T.astype(BF16), v, preferred_element_type=F32)
        st_ref[...] = st * cd + kv


def _ret_fwd_body(q_ref, k_ref, v_ref, g_ref, yb_ref, cos_ref, sin_ref, dm_ref, qw_ref, kw_ref, cd_ref,
                  gn_ref, o_ref, st_ref, *, nchunk):
    @pl.when(pl.program_id(2) == 0)
    def _():
        st_ref[...] = jnp.zeros_like(st_ref)

    qw = qw_ref[0]
    kw = kw_ref[0]
    cd = cd_ref[0, 0:1, :]
    dm = dm_ref[0]
    gn = gn_ref[0, 0:1, :]
    for c in range(nchunk):
        rows = pl.ds(c * CHUNK, CHUNK)
        cos = cos_ref[rows, :]
        sin = sin_ref[rows, :]
        q = _rope128(q_ref[rows, :].astype(F32), cos, sin)
        k = _rope128(k_ref[rows, :].astype(F32), cos, sin) * (RET_DK ** -0.5)
        v = v_ref[rows, :]
        st = st_ref[...]
        s = lax.dot_general(q.astype(BF16), k.astype(BF16), (((1,), (1,)), ((), ())),
                            preferred_element_type=F32) * dm
        y = jnp.dot(s.astype(BF16), v, preferred_element_type=F32)
        y += jnp.dot((q * qw).astype(BF16), st.astype(BF16), preferred_element_type=F32)
        y += yb_ref[rows, :]
        kv = jnp.dot((k * kw).T.astype(BF16), v, preferred_element_type=F32)
        st_ref[...] = st * cd + kv
        y = y * lax.rsqrt(jnp.mean(y * y, axis=-1, keepdims=True) + EPS) * gn
        g = g_ref[rows, :].astype(F32)
        o_ref[rows, :] = (g * _sigmoid(g) * y).astype(o_ref.dtype)


def _retention(z, ret_decay, ret_norm, cos, sin, batch, seq, ts):
    t = z.shape[0]
    nchunk = ts // CHUNK
    nstep = seq // ts
    hd = RET_HEADS
    log_g = jax.nn.log_sigmoid(ret_decay.astype(F32))
    idx = jnp.arange(CHUNK, dtype=F32)
    diff = idx[:, None] - idx[None, :]
    lf = log_g[0][:, None, None]
    lb = log_g[1][:, None, None]
    dmat = jnp.where(diff[None] >= 0, jnp.exp(lf * jnp.maximum(diff, 0.0)[None]),
                     jnp.exp(lb * jnp.maximum(-diff, 0.0)[None]))
    bc = lambda a: jnp.broadcast_to(a[:, :, None], (hd, a.shape[1], LANE))
    qw_f = bc(jnp.exp(log_g[0][:, None] * (idx + 1.0)[None, :]))
    kw_f = bc(jnp.exp(log_g[0][:, None] * (CHUNK - 1 - idx)[None, :]))
    qw_b = bc(jnp.exp(log_g[1][:, None] * (CHUNK - idx)[None, :]))
    kw_b = bc(jnp.exp(log_g[1][:, None] * idx[None, :]))
    cd_f = jnp.broadcast_to(jnp.exp(log_g[0] * CHUNK)[:, None, None], (hd, 8, LANE))
    cd_b = jnp.broadcast_to(jnp.exp(log_g[1] * CHUNK)[:, None, None], (hd, 8, LANE))
    gn = jnp.broadcast_to(ret_norm.astype(F32).reshape(hd, 1, LANE), (hd, 8, LANE))

    def zspec(col0, rev):
        cb = col0 // LANE
        if rev:
            return pl.BlockSpec((ts, LANE), lambda b, h, s: (b * nstep + nstep - 1 - s, cb + h))
        return pl.BlockSpec((ts, LANE), lambda b, h, s: (b * nstep + s, cb + h))

    def tspec(rev):
        if rev:
            return pl.BlockSpec((ts, LANE), lambda b, h, s: (nstep - 1 - s, 0))
        return pl.BlockSpec((ts, LANE), lambda b, h, s: (s, 0))

    hspec = lambda r: pl.BlockSpec((1, r, LANE), lambda b, h, s: (h, 0, 0))

    yb = pl.pallas_call(
        functools.partial(_ret_bwd_body, nchunk=nchunk),
        grid=(batch, hd, nstep),
        in_specs=[zspec(Z_QR, True), zspec(Z_KR, True), zspec(Z_VR, True), tspec(True), tspec(True),
                  hspec(CHUNK), hspec(CHUNK), hspec(8)],
        out_specs=pl.BlockSpec((ts, LANE), lambda b, h, s: (b * nstep + nstep - 1 - s, h)),
        out_shape=jax.ShapeDtypeStruct((t, hd * LANE), F32),
        scratch_shapes=[pltpu.VMEM((RET_DK, LANE), F32)],
        compiler_params=_params("parallel", "parallel", "arbitrary"),
        name="retention_bwd",
    )(z, z, z, cos, sin, qw_b, kw_b, cd_b)

    return pl.pallas_call(
        functools.partial(_ret_fwd_body, nchunk=nchunk),
        grid=(batch, hd, nstep),
        in_specs=[zspec(Z_QR, False), zspec(Z_KR, False), zspec(Z_VR, False), zspec(Z_GR, False),
                  pl.BlockSpec((ts, LANE), lambda b, h, s: (b * nstep + s, h)),
                  tspec(False), tspec(False), hspec(CHUNK), hspec(CHUNK), hspec(CHUNK), hspec(8), hspec(8)],
        out_specs=pl.BlockSpec((ts, LANE), lambda b, h, s: (b * nstep + s, h)),
        out_shape=jax.ShapeDtypeStruct((t, hd * LANE), BF16),
        scratch_shapes=[pltpu.VMEM((RET_DK, LANE), F32)],
        compiler_params=_params("parallel", "parallel", "arbitrary"),
        name="retention_fwd",
    )(z, z, z, z, yb, cos, sin, dmat, qw_f, kw_f, cd_f, gn)


def _sgu_body(u_ref, v_ref, lg_ref, lb_ref, w_ref, b_ref, o_ref, *, nchunk):
    lg = lg_ref[...]
    lb = lb_ref[...]
    for c in range(nchunk):
        rows = pl.ds(c * CHUNK, CHUNK)
        vf = _gelu(v_ref[rows, :].astype(F32))
        mu = jnp.mean(vf, axis=-1, keepdims=True)
        vc = vf - mu
        var = jnp.mean(vc * vc, axis=-1, keepdims=True)
        vn = (vc * lax.rsqrt(var + EPS) * lg + lb).astype(BF16)
        for g in range(SGU_GROUPS):
            cols = slice(g * LANE, (g + 1) * LANE)
            mixed = jnp.dot(w_ref[g], vn[:, cols], preferred_element_type=F32) + b_ref[g]
            uf = _gelu(u_ref[rows, cols].astype(F32))
            o_ref[rows, cols] = (uf * mixed).astype(o_ref.dtype)


def _sgu(z, ln_g, ln_b, w_s, b_s, ts):
    t = z.shape[0]
    bias = jnp.broadcast_to(b_s.astype(F32)[:, :, None], (SGU_GROUPS, CHUNK, LANE))
    return pl.pallas_call(
        functools.partial(_sgu_body, nchunk=ts // CHUNK),
        grid=(t // ts,),
        in_specs=[pl.BlockSpec((ts, SGU_WIDTH), lambda i: (i, Z_US // SGU_WIDTH)),
                  pl.BlockSpec((ts, SGU_WIDTH), lambda i: (i, Z_VS // SGU_WIDTH)),
                  pl.BlockSpec((1, SGU_WIDTH), lambda i: (0, 0)),
                  pl.BlockSpec((1, SGU_WIDTH), lambda i: (0, 0)),
                  pl.BlockSpec((SGU_GROUPS, CHUNK, CHUNK), lambda i: (0, 0, 0)),
                  pl.BlockSpec((SGU_GROUPS, CHUNK, LANE), lambda i: (0, 0, 0))],
        out_specs=pl.BlockSpec((ts, SGU_WIDTH), lambda i: (i, 0)),
        out_shape=jax.ShapeDtypeStruct((t, SGU_WIDTH), BF16),
        compiler_params=_params("parallel"),
        name="spatial_gating",
    )(z, z, ln_g.reshape(1, -1).astype(F32), ln_b.reshape(1, -1).astype(F32), w_s.astype(BF16), bias)


def _pair_norm_rope(x, gain, cos, sin_up, sin_dn, low):
    sq = x * x
    lo = jnp.sum(jnp.where(low, sq, 0.0), axis=-1, keepdims=True)
    hi = jnp.sum(sq, axis=-1, keepdims=True) - lo
    ms = jnp.where(low, lo, hi) * (1.0 / ATT_DH)
    xn = x * lax.rsqrt(ms + EPS) * gain
    return xn * cos + pltpu.roll(xn, LANE - 32, 1) * sin_up + pltpu.roll(xn, 32, 1) * sin_dn


def _attn_body(sink_ref, q_ref, kp_ref, k_ref, kn_ref, vp_ref, v_ref, vn_ref,
               cq_ref, suq_ref, sdq_ref, ckp_ref, sukp_ref, sdkp_ref, ckn_ref, sukn_ref, sdkn_ref,
               qg_ref, kg_ref, o_ref, *, nchunk, nstep):
    s_id = pl.program_id(1)
    ts = nchunk * CHUNK
    lane = lax.broadcasted_iota(jnp.int32, (1, LANE), 1)
    low = lane < ATT_DH
    kg = kg_ref[...]
    qg = qg_ref[...]
    k_ext = jnp.concatenate([
        _pair_norm_rope(kp_ref[...].astype(F32), kg, ckp_ref[...], sukp_ref[...], sdkp_ref[...], low),
        _pair_norm_rope(k_ref[...].astype(F32), kg, cq_ref[...], suq_ref[...], sdq_ref[...], low),
        _pair_norm_rope(kn_ref[...].astype(F32), kg, ckn_ref[...], sukn_ref[...], sdkn_ref[...], low)], axis=0)
    v_ext = jnp.concatenate([vp_ref[...], v_ref[...], vn_ref[...]], axis=0)
    k_lo = jnp.where(low, k_ext, 0.0).astype(BF16)
    k_hi = jnp.where(low, 0.0, k_ext).astype(BF16)
    qi = lax.broadcasted_iota(jnp.int32, (CHUNK, 3 * CHUNK), 0)
    kj = lax.broadcasted_iota(jnp.int32, (CHUNK, 3 * CHUNK), 1)
    band = jnp.abs(qi + CHUNK - kj) <= CHUNK
    for c in range(nchunk):
        rows = pl.ds(c * CHUNK, CHUNK)
        first = jnp.logical_and(s_id == 0, c == 0)
        last = jnp.logical_and(s_id == nstep - 1, c == nchunk - 1)
        valid = band
        if c == 0:
            valid = jnp.logical_and(valid, jnp.logical_or(kj >= CHUNK, jnp.logical_not(first)))
        if c == nchunk - 1:
            valid = jnp.logical_and(valid, jnp.logical_or(kj < 2 * CHUNK, jnp.logical_not(last)))
        kc_lo = k_lo[c * CHUNK:(c + 3) * CHUNK]
        kc_hi = k_hi[c * CHUNK:(c + 3) * CHUNK]
        vc = v_ext[c * CHUNK:(c + 3) * CHUNK]
        cos = cq_ref[rows, :]
        su = suq_ref[rows, :]
        sd = sdq_ref[rows, :]
        for pair in range(ATT_HEADS // 2):
            cols = slice(pair * LANE, (pair + 1) * LANE)
            qp = _pair_norm_rope(q_ref[rows, cols].astype(F32), qg, cos, su, sd, low) * (ATT_DH ** -0.5)
            kv_head = (2 * pair) // ATT_GROUP
            outs = []
            for half in range(2):
                head = 2 * pair + half
                qh = qp if half == kv_head else pltpu.roll(qp, ATT_DH, 1)
                if kv_head == 0:
                    qh = jnp.where(low, qh, 0.0)
                    kc = kc_lo
                else:
                    qh = jnp.where(low, 0.0, qh)
                    kc = kc_hi
                s = lax.dot_general(qh.astype(BF16), kc, (((1,), (1,)), ((), ())), preferred_element_type=F32)
                s = jnp.where(valid, s, NEG_INF)
                sk = sink_ref[head]
                m = jnp.maximum(jnp.max(s, axis=-1, keepdims=True), sk)
                e = jnp.exp(s - m)
                den = jnp.sum(e, axis=-1, keepdims=True) + jnp.exp(sk - m)
                o = jnp.dot(e.astype(BF16), vc, preferred_element_type=F32) / den
                outs.append(o if half == kv_head else pltpu.roll(o, ATT_DH, 1))
            o_ref[rows, cols] = jnp.where(low, outs[0], outs[1]).astype(o_ref.dtype)


def _attention(z, q_gain, k_gain, sink, cos, sin, batch, seq, ts):
    t = z.shape[0]
    nchunk = ts // CHUNK
    nstep = seq // ts
    nblk = seq // CHUNK
    cos2 = jnp.tile(jnp.concatenate([cos, cos], axis=1), (1, 2))
    zero = jnp.zeros_like(sin)
    sin_up = jnp.tile(jnp.concatenate([-sin, zero], axis=1), (1, 2))
    sin_dn = jnp.tile(jnp.concatenate([zero, sin], axis=1), (1, 2))
    qg = jnp.tile(q_gain.astype(F32), 2).reshape(1, LANE)
    kg = jnp.tile(k_gain.astype(F32), 2).reshape(1, LANE)

    kcb, vcb = Z_KA // LANE, Z_VA // LANE
    prev_blk = lambda s: jnp.maximum(s * nchunk - 1, 0)
    next_blk = lambda s: jnp.minimum((s + 1) * nchunk, nblk - 1)
    main = lambda cb: pl.BlockSpec((ts, LANE), lambda b, s: (b * nstep + s, cb))
    prev = lambda cb: pl.BlockSpec((CHUNK, LANE), lambda b, s: (b * nblk + prev_blk(s), cb))
    nxt = lambda cb: pl.BlockSpec((CHUNK, LANE), lambda b, s: (b * nblk + next_blk(s), cb))
    tmain = pl.BlockSpec((ts, LANE), lambda b, s: (s, 0))
    tprev = pl.BlockSpec((CHUNK, LANE), lambda b, s: (prev_blk(s), 0))
    tnext = pl.BlockSpec((CHUNK, LANE), lambda b, s: (next_blk(s), 0))
    one = pl.BlockSpec((1, LANE), lambda b, s: (0, 0))
    return pl.pallas_call(
        functools.partial(_attn_body, nchunk=nchunk, nstep=nstep),
        grid=(batch, nstep),
        in_specs=[pl.BlockSpec(memory_space=pltpu.SMEM),
                  pl.BlockSpec((ts, ATT_HEADS * ATT_DH), lambda b, s: (b * nstep + s, Z_QA // 512)),
                  prev(kcb), main(kcb), nxt(kcb), prev(vcb), main(vcb), nxt(vcb),
                  tmain, tmain, tmain, tprev, tprev, tprev, tnext, tnext, tnext, one, one],
        out_specs=pl.BlockSpec((ts, ATT_HEADS * ATT_DH), lambda b, s: (b * nstep + s, 0)),
        out_shape=jax.ShapeDtypeStruct((t, ATT_HEADS * ATT_DH), BF16),
        compiler_params=_params("parallel", "arbitrary"),
        name="window_attention",
    )(sink.astype(F32), z, z, z, z, z, z, z,
      cos2, sin_up, sin_dn, cos2, sin_up, sin_dn, cos2, sin_up, sin_dn, qg, kg)


def _merge_body(h_ref, gr_ref, gs_ref, ga_ref, yr_ref, ys_ref, ya_ref, wr_ref, ws_ref, wa_ref, wo_ref, o_ref):
    m = _sigmoid(gr_ref[...].astype(F32)) * jnp.dot(yr_ref[...], wr_ref[...], preferred_element_type=F32)
    m += _sigmoid(gs_ref[...].astype(F32)) * jnp.dot(ys_ref[...], ws_ref[...], preferred_element_type=F32)
    m += _sigmoid(ga_ref[...].astype(F32)) * jnp.dot(ya_ref[...], wa_ref[...], preferred_element_type=F32)
    o_ref[...] = h_ref[...] + jnp.dot(m.astype(BF16), wo_ref[...], preferred_element_type=F32)


def _merge(h, z, y_r, y_s, y_a, w_r, w_s, w_a, w_o, tm):
    t, d = h.shape
    row = lambda w: pl.BlockSpec((tm, w), lambda i: (i, 0))
    gate = lambda col0: pl.BlockSpec((tm, d), lambda i: (i, col0 // d))
    full = lambda a: pl.BlockSpec(a.shape, lambda i: (0, 0))
    return pl.pallas_call(
        _merge_body,
        grid=(t // tm,),
        in_specs=[row(d), gate(Z_GATE_R), gate(Z_GATE_S), gate(Z_GATE_A), row(512), row(512), row(512),
                  full(w_r), full(w_s), full(w_a), full(w_o)],
        out_specs=row(d),
        out_shape=jax.ShapeDtypeStruct((t, d), F32),
        compiler_params=_params("parallel"),
        name="branch_merge",
    )(h, z, z, z, y_r, y_s, y_a, w_r, w_s, w_a, w_o)


ID_NONE = 1 << 20


def _top16(vals, ids):
    out_v, out_i = [], []
    for _ in range(PEER_TOPK):
        m = jnp.max(vals, axis=0, keepdims=True)
        idx = jnp.min(jnp.where(vals == m, ids, ID_NONE), axis=0, keepdims=True)
        out_v.append(m)
        out_i.append(idx)
        vals = jnp.where(ids == idx, -jnp.inf, vals)
    return out_v, out_i


def _pair_blocks(first, second):
    rows1, stack1 = first
    rows2, stack2 = second
    blocks = [(stack1, rows2[0])]
    blocks += [(stack1[0:8], rows2[b]) for b in range(1, 8)]
    blocks += [(rows1[0], stack2[8:16])]
    return blocks


def _pair_ids(tm):
    a8 = lax.broadcasted_iota(jnp.int32, (8, tm), 0)
    a16 = lax.broadcasted_iota(jnp.int32, (PEER_TOPK, tm), 0)
    blocks = [a16 * PEER_TOPK]
    blocks += [jnp.where(a8 < PEER_TOPK // (b + 1), a8 * PEER_TOPK + b, ID_NONE) for b in range(1, 8)]
    blocks += [a8 + 8]
    return jnp.concatenate(blocks, axis=0)


def _route_body(h_ref, g_ref, wq_ref, keys_ref, hn_ref, eid_ref, gate_ref):
    hn = _rms(h_ref[...], g_ref[...])
    hn_ref[...] = _pack_halves(hn)
    q = jnp.dot(hn.astype(BF16), wq_ref[...], preferred_element_type=F32).astype(BF16)
    tm = q.shape[0]
    rows = lax.broadcasted_iota(jnp.int32, (PEER_KEYS, tm), 0)
    pair_ids = _pair_ids(tm)
    pair_ok = pair_ids != ID_NONE
    half = PEER_QDIM // 2
    eids, gates = [], []
    for hd in range(PEER_HEADS):
        sub = []
        for p in range(2):
            qs = q[:, (2 * hd + p) * half:(2 * hd + p + 1) * half]
            s = lax.dot_general(keys_ref[hd, p], qs, (((1,), (1,)), ((), ())), preferred_element_type=F32)
            sub.append(_top16(s, rows))
        (s1, i1), (s2, i2) = sub
        scores = _pair_blocks((s1, jnp.concatenate(s1, axis=0)), (s2, jnp.concatenate(s2, axis=0)))
        experts = _pair_blocks((i1, jnp.concatenate(i1, axis=0)), (i2, jnp.concatenate(i2, axis=0)))
        cand_s = jnp.where(pair_ok, jnp.concatenate([a + b for a, b in scores], axis=0), -jnp.inf)
        cand_e = jnp.concatenate([a * PEER_KEYS + b for a, b in experts], axis=0)
        top_s, sel = _top16(cand_s, pair_ids)
        top_e = [jnp.sum(jnp.where(pair_ids == i, cand_e, 0), axis=0, keepdims=True) for i in sel]
        ts_ = jnp.concatenate(top_s, axis=0)
        e = jnp.exp(ts_ - top_s[0])
        gates.append(e / jnp.sum(e, axis=0, keepdims=True))
        eids.append(jnp.concatenate(top_e, axis=0))
    eid_ref[...] = jnp.concatenate(eids, axis=0).T
    gate_ref[...] = jnp.concatenate(gates, axis=0).T


def _route(h, gain, wq_bf, keys_bf, tm):
    t, d = h.shape
    return pl.pallas_call(
        _route_body,
        grid=(t // tm,),
        in_specs=[pl.BlockSpec((tm, d), lambda i: (i, 0)),
                  pl.BlockSpec((1, d), lambda i: (0, 0)),
                  pl.BlockSpec(wq_bf.shape, lambda i: (0, 0)),
                  pl.BlockSpec(keys_bf.shape, lambda i: (0, 0, 0, 0))],
        out_specs=[pl.BlockSpec((tm, d // 2), lambda i: (i, 0)),
                   pl.BlockSpec((tm, PEER_PICKS), lambda i: (i, 0)),
                   pl.BlockSpec((tm, PEER_PICKS), lambda i: (i, 0))],
        out_shape=[jax.ShapeDtypeStruct((t, d // 2), jnp.int32),
                   jax.ShapeDtypeStruct((t, PEER_PICKS), jnp.int32),
                   jax.ShapeDtypeStruct((t, PEER_PICKS), F32)],
        compiler_params=_params("parallel"),
        name="peer_route",
    )(h, gain.reshape(1, d), wq_bf, keys_bf)


def _sc_mesh():
    return plsc.VectorSubcoreMesh(core_axis_name="core", subcore_axis_name="subcore")


def _sc_worker():
    return lax.axis_index("subcore") * SC_CORES + lax.axis_index("core")


def _sc_row_pipeline(tab_hbm, idx_v, buf, sems, compute):
    nslot, nrow = buf.shape[0], buf.shape[1]
    nsub = PEER_PICKS // nrow
    ng = SC_TOKENS * nsub

    def gather(g, slot):
        rows = idx_v.at[g // nsub, pl.ds((g % nsub) * nrow, nrow)]
        return pltpu.make_async_copy(tab_hbm.at[rows], buf.at[slot], sems.at[slot])

    for b in range(nslot - 1):
        gather(b, b).start()

    @pl.loop(0, ng, step=nslot)
    def _(g):
        for b in range(nslot):
            ahead = g + b + nslot - 1

            @pl.when(ahead < ng)
            def _():
                gather(ahead, (b + nslot - 1) % nslot).start()

            gather(g + b, b).wait()
            compute(g + b, b)


def _sc_split(words):
    return (lax.bitcast_convert_type(words & HI_MASK, F32), lax.bitcast_convert_type(words << 16, F32))


def _expert_scores(table, eid, hn):
    t, words = hn.shape
    per_worker = t // SC_WORKERS
    nslot, nrow = SC_SCORE_RING
    nsub = PEER_PICKS // nrow
    ln = SC_LANES

    @functools.partial(
        pl.kernel, mesh=_sc_mesh(),
        out_type=jax.ShapeDtypeStruct((t, PEER_PICKS), F32),
        scratch_types=[pltpu.VMEM((SC_TOKENS, PEER_PICKS), jnp.int32),
                       pltpu.VMEM((SC_TOKENS, words), jnp.int32),
                       pltpu.VMEM((SC_TOKENS, PEER_PICKS), F32),
                       pltpu.VMEM((nslot, nrow, words), jnp.int32),
                       pltpu.SemaphoreType.DMA((nslot,))],
        compiler_params=pltpu.CompilerParams(needs_layout_passes=False),
        name="peer_expert_scores")
    def run(tab_hbm, eid_hbm, hn_hbm, out_hbm, idx_v, x_v, a_v, buf, sems):
        lane = lax.iota(jnp.int32, ln)

        def compute(g, slot):
            tok = g // nsub
            sub = g % nsub
            for grp in range(nrow // ln):
                def body(c, accs):
                    c0 = pl.ds(2 * c * ln, ln)
                    c1 = pl.ds((2 * c + 1) * ln, ln)
                    x0 = plsc.bitcast(x_v[tok, c0], BF16)
                    x1 = plsc.bitcast(x_v[tok, c1], BF16)
                    out = []
                    for r in range(ln):
                        u0 = plsc.bitcast(buf[slot, grp * ln + r, c0], BF16)
                        u1 = plsc.bitcast(buf[slot, grp * ln + r, c1], BF16)
                        hi, lo = _sc_split(plsc.bitcast(u0 * x0 + u1 * x1, jnp.int32))
                        out.append(accs[r] + hi + lo)
                    return tuple(out)

                accs = lax.fori_loop(0, words // (2 * ln), body, tuple(jnp.zeros((ln,), F32) for _ in range(ln)))
                res = jnp.zeros((ln,), F32)
                for r in range(ln):
                    res = jnp.where(lane == r, jnp.sum(accs[r]), res)
                a_v[tok, pl.ds(sub * nrow + grp * ln, ln)] = res

        @pl.loop(0, per_worker // SC_TOKENS)
        def _(blk):
            tok0 = _sc_worker() * per_worker + blk * SC_TOKENS
            pltpu.sync_copy(eid_hbm.at[pl.ds(tok0, SC_TOKENS)], idx_v)
            pltpu.sync_copy(hn_hbm.at[pl.ds(tok0, SC_TOKENS)], x_v)
            _sc_row_pipeline(tab_hbm, idx_v, buf, sems, compute)
            pltpu.sync_copy(a_v, out_hbm.at[pl.ds(tok0, SC_TOKENS)])

    return run(table, eid, hn)


def _expert_mix(table, eid, w):
    t = eid.shape[0]
    words = table.shape[1]
    d = 2 * words
    per_worker = t // SC_WORKERS
    nslot, nrow = SC_MIX_RING
    nsub = PEER_PICKS // nrow
    ln = SC_LANES
    nvec = SC_WORDS // ln

    @functools.partial(
        pl.kernel, mesh=_sc_mesh(),
        out_type=jax.ShapeDtypeStruct((t, d), F32),
        scratch_types=[pltpu.VMEM((SC_TOKENS, PEER_PICKS), jnp.int32),
                       pltpu.VMEM((SC_TOKENS, PEER_PICKS), jnp.int32),
                       pltpu.VMEM((SC_TOKENS, d), F32),
                       pltpu.VMEM((nslot, nrow, words), jnp.int32),
                       pltpu.SemaphoreType.DMA((nslot,))],
        compiler_params=pltpu.CompilerParams(needs_layout_passes=False),
        name="peer_expert_mix")
    def run(tab_hbm, eid_hbm, w_hbm, out_hbm, idx_v, w_v, y_v, buf, sems):
        zero = jnp.zeros((ln,), F32)

        def compute(g, slot):
            tok = g // nsub
            sub = g % nsub
            tokv = jnp.full((ln,), tok, jnp.int32)

            def weight(r):
                pick = jnp.full((ln,), sub * nrow + r, jnp.int32)
                return plsc.bitcast(plsc.load_gather(w_v, [tokv, pick]), BF16)

            for cc in range(words // SC_WORDS):
                def body(rg, accs):
                    r0 = SC_MIX_GROUP * rg
                    ws = [weight(r0 + j) for j in range(SC_MIX_GROUP)]
                    out = list(accs)
                    for k in range(nvec):
                        cols = pl.ds(cc * SC_WORDS + k * ln, ln)
                        prod = ws[0] * plsc.bitcast(buf[slot, r0, cols], BF16)
                        for j in range(1, SC_MIX_GROUP):
                            prod = prod + ws[j] * plsc.bitcast(buf[slot, r0 + j, cols], BF16)
                        hi, lo = _sc_split(plsc.bitcast(prod, jnp.int32))
                        out[k] = accs[k] + hi
                        out[nvec + k] = accs[nvec + k] + lo
                    return tuple(out)

                lo_cols = [pl.ds(cc * SC_WORDS + k * ln, ln) for k in range(nvec)]
                hi_cols = [pl.ds(words + cc * SC_WORDS + k * ln, ln) for k in range(nvec)]
                init = tuple(y_v[tok, c] for c in lo_cols + hi_cols)
                accs = lax.fori_loop(0, nrow // SC_MIX_GROUP, body, init)
                for c, acc in zip(lo_cols + hi_cols, accs):
                    y_v[tok, c] = acc

        @pl.loop(0, per_worker // SC_TOKENS)
        def _(blk):
            tok0 = _sc_worker() * per_worker + blk * SC_TOKENS
            pltpu.sync_copy(eid_hbm.at[pl.ds(tok0, SC_TOKENS)], idx_v)
            pltpu.sync_copy(w_hbm.at[pl.ds(tok0, SC_TOKENS)], w_v)

            @pl.loop(0, SC_TOKENS)
            def _(tok):
                @pl.loop(0, d // ln)
                def _(c):
                    y_v[tok, pl.ds(c * ln, ln)] = zero

            _sc_row_pipeline(tab_hbm, idx_v, buf, sems, compute)
            pltpu.sync_copy(y_v, out_hbm.at[pl.ds(tok0, SC_TOKENS)])

    return run(table, eid, w)


def _pick_weights_body(a_ref, g_ref, o_ref):
    o_ref[...] = _pack_twice(g_ref[...] * _gelu(a_ref[...]))


def _pick_weights(a, gate, tm):
    t = a.shape[0]
    spec = pl.BlockSpec((tm, PEER_PICKS), lambda i: (i, 0))
    return pl.pallas_call(
        _pick_weights_body,
        grid=(t // tm,),
        in_specs=[spec, spec],
        out_specs=spec,
        out_shape=jax.ShapeDtypeStruct(a.shape, jnp.int32),
        compiler_params=_params("parallel"),
        name="peer_pick_weights",
    )(a, gate)


def _ple_body(h_ref, y_ref, g_ref, wg_ref, p_ref, wp_ref, o_ref):
    h = h_ref[...] + y_ref[...]
    hn = _rms(h, g_ref[...]).astype(BF16)
    gate = _sigmoid(jnp.dot(hn, wg_ref[...], preferred_element_type=F32))
    emb = jnp.dot(p_ref[...].astype(BF16), wp_ref[...], preferred_element_type=F32)
    o_ref[...] = h + gate * emb


def _ple(h, y, gain, wg_bf, p, wp_bf, tm):
    t, d = h.shape
    return pl.pallas_call(
        _ple_body,
        grid=(t // tm,),
        in_specs=[pl.BlockSpec((tm, d), lambda i: (i, 0)),
                  pl.BlockSpec((tm, d), lambda i: (i, 0)),
                  pl.BlockSpec((1, d), lambda i: (0, 0)),
                  pl.BlockSpec(wg_bf.shape, lambda i: (0, 0)),
                  pl.BlockSpec((tm, p.shape[1]), lambda i: (i, 0)),
                  pl.BlockSpec(wp_bf.shape, lambda i: (0, 0))],
        out_specs=pl.BlockSpec((tm, d), lambda i: (i, 0)),
        out_shape=jax.ShapeDtypeStruct((t, d), F32),
        compiler_params=_params("parallel"),
        name="layer_embedding",
    )(h, y, gain.reshape(1, d), wg_bf, p, wp_bf)


def _rope_tables(seq, dim):
    inv = 1.0 / (ROPE_THETA ** (jnp.arange(0, dim, 2, dtype=F32) / dim))
    ang = jnp.arange(seq, dtype=F32)[:, None] * inv[None, :]
    return jnp.cos(ang), jnp.sin(ang)


def _pack_table(tab):
    n = tab.shape[1] // 2
    b = lax.bitcast_convert_type(tab.astype(BF16), jnp.uint16).astype(jnp.uint32)
    return lax.bitcast_convert_type((b[:, :n] << 16) | b[:, n:], jnp.int32)


def _permute_in_columns(w_in):
    return jnp.concatenate([w_in[:, 3840:], w_in[:, :3840]], axis=1)


def kernel(x, p, norm_mix, w_in, ret_decay, ret_norm, sgu_ln_g, sgu_ln_b, sgu_w, sgu_b, att_q_norm, att_k_norm, att_sink, w_proj_ret, w_proj_sgu, w_proj_att, w_out, norm_ffn, peer_wq, peer_keys, peer_u, peer_v, norm_ple, ple_gate, ple_proj):
    batch, seq, d = x.shape
    depth = w_in.shape[0]
    groups = PIPELINE_GROUPS if batch % PIPELINE_GROUPS == 0 else 1
    gb = batch // groups
    t = gb * seq
    ts = min(512, seq)
    tm = min(512, t)
    cos_r, sin_r = _rope_tables(seq, RET_DK)
    cos_r2 = jnp.concatenate([cos_r, cos_r], axis=1)
    sin_r2 = jnp.concatenate([-sin_r, sin_r], axis=1)
    cos_a, sin_a = _rope_tables(seq, ATT_DH)
    lw = []
    for i in range(depth):
        lw.append((_permute_in_columns(w_in[i]).astype(BF16), w_proj_ret[i].astype(BF16),
                   w_proj_sgu[i].astype(BF16), w_proj_att[i].astype(BF16), w_out[i].astype(BF16),
                   peer_wq[i].astype(BF16), peer_keys[i].astype(BF16), ple_gate[i].astype(BF16),
                   ple_proj[i].astype(BF16), _pack_table(peer_u[i]), _pack_table(peer_v[i])))
    out = []
    for g in range(groups):
        h = x[g * gb:(g + 1) * gb].reshape(t, d)
        for i in range(depth):
            w_in_bf, w_r, w_s, w_a, w_o, w_q, keys, w_g, w_p, tab_u, tab_v = lw[i]
            z = _in_projection(h, norm_mix[i], w_in_bf, tm)
            y_r = _retention(z, ret_decay[i], ret_norm[i], cos_r2, sin_r2, gb, seq, ts)
            y_s = _sgu(z, sgu_ln_g[i], sgu_ln_b[i], sgu_w[i], sgu_b[i], ts)
            y_a = _attention(z, att_q_norm[i], att_k_norm[i], att_sink[i], cos_a, sin_a, gb, seq, ts)
            h = _merge(h, z, y_r, y_s, y_a, w_r, w_s, w_a, w_o, tm)
            hn, eid, gate = _route(h, norm_ffn[i], w_q, keys, min(256, t))
            a = _expert_scores(tab_u, eid, hn)
            w = _pick_weights(a, gate, tm)
            y = _expert_mix(tab_v, eid, w)
            p_g = p[i, g * gb:(g + 1) * gb].reshape(t, -1)
            h = _ple(h, y, norm_ple[i], w_g, p_g, w_p, tm)
        out.append(h)
    return jnp.concatenate(out, axis=0).reshape(batch, seq, d)
```

```python
import functools
import math

import jax
import jax.numpy as jnp
from jax import lax
from jax.experimental import pallas as pl
from jax.experimental.pallas import tpu as pltpu
from jax.experimental.pallas import tpu_sc as plsc

F32 = jnp.float32
BF16 = jnp.bfloat16

D_MODEL = 1024
PLE_DIM = 256
CHUNK = 128
EPS = 1e-6
ROPE_THETA = 10000.0
RET_HEADS = 4
RET_DK = 128
SGU_GROUPS = 4
SGU_WIDTH = 512
ATT_HEADS = 8
ATT_KV_HEADS = 2
ATT_DH = 64
ATT_GROUP = ATT_HEADS // ATT_KV_HEADS
NEG_INF = -1e30
PEER_HEADS = 8
PEER_KEYS = 128
PEER_QDIM = 256
PEER_TOPK = 16
PEER_PICKS = PEER_HEADS * PEER_TOPK

Z_GATE_R, Z_GATE_S, Z_GATE_A = 0, 1024, 2048
Z_QR, Z_KR, Z_VR, Z_GR = 3072, 3584, 4096, 4608
Z_US, Z_VS = 5120, 5632
Z_QA, Z_KA, Z_VA = 6144, 6656, 6784
D_IN = 6912

LANE = 128
SC_CORES = 2
SC_SUBCORES = 16
SC_LANES = 16
SC_WORKERS = SC_CORES * SC_SUBCORES
SC_TOKENS = 32
SC_SCORE_RING = (4, 32)
SC_MIX_RING = (2, 64)
SC_MIX_GROUP = 4
SC_WORDS = 128
HI_MASK = -65536
VMEM_LIMIT = 56 * 1024 * 1024
PIPELINE_LAG = 2
PIPELINE_GROUPS = 8


def _params(*sem):
    return pltpu.CompilerParams(dimension_semantics=sem, vmem_limit_bytes=VMEM_LIMIT)


def _gelu(x):
    return 0.5 * x * (1.0 + lax.erf(x * (1.0 / math.sqrt(2.0))))


def _sigmoid(x):
    return 1.0 / (1.0 + jnp.exp(-x))


def _rms(x, g):
    return x * lax.rsqrt(jnp.mean(x * x, axis=-1, keepdims=True) + EPS) * g


def _bf16_hi_bits(x):
    b = lax.bitcast_convert_type(x, jnp.int32)
    return (b + 0x7FFF + ((b >> 16) & 1)) & HI_MASK


def _pack_halves(x):
    n = x.shape[1] // 2
    return _bf16_hi_bits(x[:, :n]) | lax.shift_right_logical(_bf16_hi_bits(x[:, n:]), 16)


def _pack_twice(x):
    b = _bf16_hi_bits(x)
    return b | lax.shift_right_logical(b, 16)


INPROJ_COLS = 1152


def _inproj_body(x_ref, g_ref, w_ref, o_ref):
    xn = _rms(x_ref[...], g_ref[...]).astype(BF16)
    for j in range(w_ref.shape[1] // INPROJ_COLS):
        cols = slice(j * INPROJ_COLS, (j + 1) * INPROJ_COLS)
        o_ref[:, cols] = jnp.dot(xn, w_ref[:, cols], preferred_element_type=F32).astype(o_ref.dtype)


def _in_projection(h, gain, w_bf, tm):
    t, d = h.shape
    n = w_bf.shape[1]
    return pl.pallas_call(
        _inproj_body,
        grid=(t // tm,),
        in_specs=[pl.BlockSpec((tm, d), lambda i: (i, 0)),
                  pl.BlockSpec((1, d), lambda i: (0, 0)),
                  pl.BlockSpec((d, n), lambda i: (0, 0))],
        out_specs=pl.BlockSpec((tm, n), lambda i: (i, 0)),
        out_shape=jax.ShapeDtypeStruct((t, n), BF16),
        compiler_params=_params("parallel"),
        name="in_projection",
    )(h, gain.reshape(1, d), w_bf)


def _rope128(x, cos, sin_signed):
    return x * cos + pltpu.roll(x, 64, 1) * sin_signed


def _ret_bwd_body(q_ref, k_ref, v_ref, cos_ref, sin_ref, qw_ref, kw_ref, cd_ref, o_ref, st_ref, *, nchunk):
    @pl.when(pl.program_id(2) == 0)
    def _():
        st_ref[...] = jnp.zeros_like(st_ref)

    qw = qw_ref[0]
    kw = kw_ref[0]
    cd = cd_ref[0, 0:1, :]
    for c in reversed(range(nchunk)):
        rows = pl.ds(c * CHUNK, CHUNK)
        cos = cos_ref[rows, :]
        sin = sin_ref[rows, :]
        q = _rope128(q_ref[rows, :].astype(F32), cos, sin)
        k = _rope128(k_ref[rows, :].astype(F32), cos, sin) * (RET_DK ** -0.5)
        v = v_ref[rows, :]
        st = st_ref[...]
        o_ref[rows, :] = jnp.dot((q * qw).astype(BF16), st.astype(BF16), preferred_element_type=F32)
        kv = jnp.dot((k * kw).T.astype(BF16), v, preferred_element_type=F32)
        st_ref[...] = st * cd + kv


def _ret_fwd_body(q_ref, k_ref, v_ref, g_ref, yb_ref, cos_ref, sin_ref, dm_ref, qw_ref, kw_ref, cd_ref,
                  gn_ref, o_ref, st_ref, *, nchunk):
    @pl.when(pl.program_id(2) == 0)
    def _():
        st_ref[...] = jnp.zeros_like(st_ref)

    qw = qw_ref[0]
    kw = kw_ref[0]
    cd = cd_ref[0, 0:1, :]
    dm = dm_ref[0]
    gn = gn_ref[0, 0:1, :]
    for c in range(nchunk):
        rows = pl.ds(c * CHUNK, CHUNK)
        cos = cos_ref[rows, :]
        sin = sin_ref[rows, :]
        q = _rope128(q_ref[rows, :].astype(F32), cos, sin)
        k = _rope128(k_ref[rows, :].astype(F32), cos, sin) * (RET_DK ** -0.5)
        v = v_ref[rows, :]
        st = st_ref[...]
        s = lax.dot_general(q.astype(BF16), k.astype(BF16), (((1,), (1,)), ((), ())),
                            preferred_element_type=F32) * dm
        y = jnp.dot(s.astype(BF16), v, preferred_element_type=F32)
        y += jnp.dot((q * qw).astype(BF16), st.astype(BF16), preferred_element_type=F32)
        y += yb_ref[rows, :]
        kv = jnp.dot((k * kw).T.astype(BF16), v, preferred_element_type=F32)
        st_ref[...] = st * cd + kv
        y = y * lax.rsqrt(jnp.mean(y * y, axis=-1, keepdims=True) + EPS) * gn
        g = g_ref[rows, :].astype(F32)
        o_ref[rows, :] = (g * _sigmoid(g) * y).astype(o_ref.dtype)


def _retention(z, ret_decay, ret_norm, cos, sin, batch, seq, ts):
    t = z.shape[0]
    nchunk = ts // CHUNK
    nstep = seq // ts
    hd = RET_HEADS
    log_g = jax.nn.log_sigmoid(ret_decay.astype(F32))
    idx = jnp.arange(CHUNK, dtype=F32)
    diff = idx[:, None] - idx[None, :]
    lf = log_g[0][:, None, None]
    lb = log_g[1][:, None, None]
    dmat = jnp.where(diff[None] >= 0, jnp.exp(lf * jnp.maximum(diff, 0.0)[None]),
                     jnp.exp(lb * jnp.maximum(-diff, 0.0)[None]))
    bc = lambda a: jnp.broadcast_to(a[:, :, None], (hd, a.shape[1], LANE))
    qw_f = bc(jnp.exp(log_g[0][:, None] * (idx + 1.0)[None, :]))
    kw_f = bc(jnp.exp(log_g[0][:, None] * (CHUNK - 1 - idx)[None, :]))
    qw_b = bc(jnp.exp(log_g[1][:, None] * (CHUNK - idx)[None, :]))
    kw_b = bc(jnp.exp(log_g[1][:, None] * idx[None, :]))
    cd_f = jnp.broadcast_to(jnp.exp(log_g[0] * CHUNK)[:, None, None], (hd, 8, LANE))
    cd_b = jnp.broadcast_to(jnp.exp(log_g[1] * CHUNK)[:, None, None], (hd, 8, LANE))
    gn = jnp.broadcast_to(ret_norm.astype(F32).reshape(hd, 1, LANE), (hd, 8, LANE))

    def zspec(col0, rev):
        cb = col0 // LANE
        if rev:
            return pl.BlockSpec((ts, LANE), lambda b, h, s: (b * nstep + nstep - 1 - s, cb + h))
        return pl.BlockSpec((ts, LANE), lambda b, h, s: (b * nstep + s, cb + h))

    def tspec(rev):
        if rev:
            return pl.BlockSpec((ts, LANE), lambda b, h, s: (nstep - 1 - s, 0))
        return pl.BlockSpec((ts, LANE), lambda b, h, s: (s, 0))

    hspec = lambda r: pl.BlockSpec((1, r, LANE), lambda b, h, s: (h, 0, 0))

    yb = pl.pallas_call(
        functools.partial(_ret_bwd_body, nchunk=nchunk),
        grid=(batch, hd, nstep),
        in_specs=[zspec(Z_QR, True), zspec(Z_KR, True), zspec(Z_VR, True), tspec(True), tspec(True),
                  hspec(CHUNK), hspec(CHUNK), hspec(8)],
        out_specs=pl.BlockSpec((ts, LANE), lambda b, h, s: (b * nstep + nstep - 1 - s, h)),
        out_shape=jax.ShapeDtypeStruct((t, hd * LANE), F32),
        scratch_shapes=[pltpu.VMEM((RET_DK, LANE), F32)],
        compiler_params=_params("parallel", "parallel", "arbitrary"),
        name="retention_bwd",
    )(z, z, z, cos, sin, qw_b, kw_b, cd_b)

    return pl.pallas_call(
        functools.partial(_ret_fwd_body, nchunk=nchunk),
        grid=(batch, hd, nstep),
        in_specs=[zspec(Z_QR, False), zspec(Z_KR, False), zspec(Z_VR, False), zspec(Z_GR, False),
                  pl.BlockSpec((ts, LANE), lambda b, h, s: (b * nstep + s, h)),
                  tspec(False), tspec(False), hspec(CHUNK), hspec(CHUNK), hspec(CHUNK), hspec(8), hspec(8)],
        out_specs=pl.BlockSpec((ts, LANE), lambda b, h, s: (b * nstep + s, h)),
        out_shape=jax.ShapeDtypeStruct((t, hd * LANE), BF16),
        scratch_shapes=[pltpu.VMEM((RET_DK, LANE), F32)],
        compiler_params=_params("parallel", "parallel", "arbitrary"),
        name="retention_fwd",
    )(z, z, z, z, yb, cos, sin, dmat, qw_f, kw_f, cd_f, gn)


def _sgu_body(u_ref, v_ref, lg_ref, lb_ref, w_ref, b_ref, o_ref, *, nchunk):
    lg = lg_ref[...]
    lb = lb_ref[...]
    for c in range(nchunk):
        rows = pl.ds(c * CHUNK, CHUNK)
        vf = _gelu(v_ref[rows, :].astype(F32))
        mu = jnp.mean(vf, axis=-1, keepdims=True)
        vc = vf - mu
        var = jnp.mean(vc * vc, axis=-1, keepdims=True)
        vn = (vc * lax.rsqrt(var + EPS) * lg + lb).astype(BF16)
        for g in range(SGU_GROUPS):
            cols = slice(g * LANE, (g + 1) * LANE)
            mixed = jnp.dot(w_ref[g], vn[:, cols], preferred_element_type=F32) + b_ref[g]
            uf = _gelu(u_ref[rows, cols].astype(F32))
            o_ref[rows, cols] = (uf * mixed).astype(o_ref.dtype)


def _sgu(z, ln_g, ln_b, w_s, b_s, ts):
    t = z.shape[0]
    bias = jnp.broadcast_to(b_s.astype(F32)[:, :, None], (SGU_GROUPS, CHUNK, LANE))
    return pl.pallas_call(
        functools.partial(_sgu_body, nchunk=ts // CHUNK),
        grid=(t // ts,),
        in_specs=[pl.BlockSpec((ts, SGU_WIDTH), lambda i: (i, Z_US // SGU_WIDTH)),
                  pl.BlockSpec((ts, SGU_WIDTH), lambda i: (i, Z_VS // SGU_WIDTH)),
                  pl.BlockSpec((1, SGU_WIDTH), lambda i: (0, 0)),
                  pl.BlockSpec((1, SGU_WIDTH), lambda i: (0, 0)),
                  pl.BlockSpec((SGU_GROUPS, CHUNK, CHUNK), lambda i: (0, 0, 0)),
                  pl.BlockSpec((SGU_GROUPS, CHUNK, LANE), lambda i: (0, 0, 0))],
        out_specs=pl.BlockSpec((ts, SGU_WIDTH), lambda i: (i, 0)),
        out_shape=jax.ShapeDtypeStruct((t, SGU_WIDTH), BF16),
        compiler_params=_params("parallel"),
        name="spatial_gating",
    )(z, z, ln_g.reshape(1, -1).astype(F32), ln_b.reshape(1, -1).astype(F32), w_s.astype(BF16), bias)


def _pair_norm_rope(x, gain, cos, sin_up, sin_dn, low):
    sq = x * x
    lo = jnp.sum(jnp.where(low, sq, 0.0), axis=-1, keepdims=True)
    hi = jnp.sum(sq, axis=-1, keepdims=True) - lo
    ms = jnp.where(low, lo, hi) * (1.0 / ATT_DH)
    xn = x * lax.rsqrt(ms + EPS) * gain
    return xn * cos + pltpu.roll(xn, LANE - 32, 1) * sin_up + pltpu.roll(xn, 32, 1) * sin_dn


def _attn_body(sink_ref, q_ref, kp_ref, k_ref, kn_ref, vp_ref, v_ref, vn_ref,
               cq_ref, suq_ref, sdq_ref, ckp_ref, sukp_ref, sdkp_ref, ckn_ref, sukn_ref, sdkn_ref,
               qg_ref, kg_ref, o_ref, *, nchunk, nstep):
    s_id = pl.program_id(1)
    ts = nchunk * CHUNK
    lane = lax.broadcasted_iota(jnp.int32, (1, LANE), 1)
    low = lane < ATT_DH
    kg = kg_ref[...]
    qg = qg_ref[...]
    k_ext = jnp.concatenate([
        _pair_norm_rope(kp_ref[...].astype(F32), kg, ckp_ref[...], sukp_ref[...], sdkp_ref[...], low),
        _pair_norm_rope(k_ref[...].astype(F32), kg, cq_ref[...], suq_ref[...], sdq_ref[...], low),
        _pair_norm_rope(kn_ref[...].astype(F32), kg, ckn_ref[...], sukn_ref[...], sdkn_ref[...], low)], axis=0)
    v_ext = jnp.concatenate([vp_ref[...], v_ref[...], vn_ref[...]], axis=0)
    k_lo = jnp.where(low, k_ext, 0.0).astype(BF16)
    k_hi = jnp.where(low, 0.0, k_ext).astype(BF16)
    qi = lax.broadcasted_iota(jnp.int32, (CHUNK, 3 * CHUNK), 0)
    kj = lax.broadcasted_iota(jnp.int32, (CHUNK, 3 * CHUNK), 1)
    band = jnp.abs(qi + CHUNK - kj) <= CHUNK
    for c in range(nchunk):
        rows = pl.ds(c * CHUNK, CHUNK)
        first = jnp.logical_and(s_id == 0, c == 0)
        last = jnp.logical_and(s_id == nstep - 1, c == nchunk - 1)
        valid = band
        if c == 0:
            valid = jnp.logical_and(valid, jnp.logical_or(kj >= CHUNK, jnp.logical_not(first)))
        if c == nchunk - 1:
            valid = jnp.logical_and(valid, jnp.logical_or(kj < 2 * CHUNK, jnp.logical_not(last)))
        kc_lo = k_lo[c * CHUNK:(c + 3) * CHUNK]
        kc_hi = k_hi[c * CHUNK:(c + 3) * CHUNK]
        vc = v_ext[c * CHUNK:(c + 3) * CHUNK]
        cos = cq_ref[rows, :]
        su = suq_ref[rows, :]
        sd = sdq_ref[rows, :]
        for pair in range(ATT_HEADS // 2):
            cols = slice(pair * LANE, (pair + 1) * LANE)
            qp = _pair_norm_rope(q_ref[rows, cols].astype(F32), qg, cos, su, sd, low) * (ATT_DH ** -0.5)
            kv_head = (2 * pair) // ATT_GROUP
            outs = []
            for half in range(2):
                head = 2 * pair + half
                qh = qp if half == kv_head else pltpu.roll(qp, ATT_DH, 1)
                if kv_head == 0:
                    qh = jnp.where(low, qh, 0.0)
                    kc = kc_lo
                else:
                    qh = jnp.where(low, 0.0, qh)
                    kc = kc_hi
                s = lax.dot_general(qh.astype(BF16), kc, (((1,), (1,)), ((), ())), preferred_element_type=F32)
                s = jnp.where(valid, s, NEG_INF)
                sk = sink_ref[head]
                m = jnp.maximum(jnp.max(s, axis=-1, keepdims=True), sk)
                e = jnp.exp(s - m)
                den = jnp.sum(e, axis=-1, keepdims=True) + jnp.exp(sk - m)
                o = jnp.dot(e.astype(BF16), vc, preferred_element_type=F32) / den
                outs.append(o if half == kv_head else pltpu.roll(o, ATT_DH, 1))
            o_ref[rows, cols] = jnp.where(low, outs[0], outs[1]).astype(o_ref.dtype)


def _attention(z, q_gain, k_gain, sink, cos, sin, batch, seq, ts):
    t = z.shape[0]
    nchunk = ts // CHUNK
    nstep = seq // ts
    nblk = seq // CHUNK
    cos2 = jnp.tile(jnp.concatenate([cos, cos], axis=1), (1, 2))
    zero = jnp.zeros_like(sin)
    sin_up = jnp.tile(jnp.concatenate([-sin, zero], axis=1), (1, 2))
    sin_dn = jnp.tile(jnp.concatenate([zero, sin], axis=1), (1, 2))
    qg = jnp.tile(q_gain.astype(F32), 2).reshape(1, LANE)
    kg = jnp.tile(k_gain.astype(F32), 2).reshape(1, LANE)

    kcb, vcb = Z_KA // LANE, Z_VA // LANE
    prev_blk = lambda s: jnp.maximum(s * nchunk - 1, 0)
    next_blk = lambda s: jnp.minimum((s + 1) * nchunk, nblk - 1)
    main = lambda cb: pl.BlockSpec((ts, LANE), lambda b, s: (b * nstep + s, cb))
    prev = lambda cb: pl.BlockSpec((CHUNK, LANE), lambda b, s: (b * nblk + prev_blk(s), cb))
    nxt = lambda cb: pl.BlockSpec((CHUNK, LANE), lambda b, s: (b * nblk + next_blk(s), cb))
    tmain = pl.BlockSpec((ts, LANE), lambda b, s: (s, 0))
    tprev = pl.BlockSpec((CHUNK, LANE), lambda b, s: (prev_blk(s), 0))
    tnext = pl.BlockSpec((CHUNK, LANE), lambda b, s: (next_blk(s), 0))
    one = pl.BlockSpec((1, LANE), lambda b, s: (0, 0))
    return pl.pallas_call(
        functools.partial(_attn_body, nchunk=nchunk, nstep=nstep),
        grid=(batch, nstep),
        in_specs=[pl.BlockSpec(memory_space=pltpu.SMEM),
                  pl.BlockSpec((ts, ATT_HEADS * ATT_DH), lambda b, s: (b * nstep + s, Z_QA // 512)),
                  prev(kcb), main(kcb), nxt(kcb), prev(vcb), main(vcb), nxt(vcb),
                  tmain, tmain, tmain, tprev, tprev, tprev, tnext, tnext, tnext, one, one],
        out_specs=pl.BlockSpec((ts, ATT_HEADS * ATT_DH), lambda b, s: (b * nstep + s, 0)),
        out_shape=jax.ShapeDtypeStruct((t, ATT_HEADS * ATT_DH), BF16),
        compiler_params=_params("parallel", "arbitrary"),
        name="window_attention",
    )(sink.astype(F32), z, z, z, z, z, z, z,
      cos2, sin_up, sin_dn, cos2, sin_up, sin_dn, cos2, sin_up, sin_dn, qg, kg)


def _merge_body(h_ref, gr_ref, gs_ref, ga_ref, yr_ref, ys_ref, ya_ref, wr_ref, ws_ref, wa_ref, wo_ref, o_ref):
    m = _sigmoid(gr_ref[...].astype(F32)) * jnp.dot(yr_ref[...], wr_ref[...], preferred_element_type=F32)
    m += _sigmoid(gs_ref[...].astype(F32)) * jnp.dot(ys_ref[...], ws_ref[...], preferred_element_type=F32)
    m += _sigmoid(ga_ref[...].astype(F32)) * jnp.dot(ya_ref[...], wa_ref[...], preferred_element_type=F32)
    o_ref[...] = h_ref[...] + jnp.dot(m.astype(BF16), wo_ref[...], preferred_element_type=F32)


def _merge(h, z, y_r, y_s, y_a, w_r, w_s, w_a, w_o, tm):
    t, d = h.shape
    row = lambda w: pl.BlockSpec((tm, w), lambda i: (i, 0))
    gate = lambda col0: pl.BlockSpec((tm, d), lambda i: (i, col0 // d))
    full = lambda a: pl.BlockSpec(a.shape, lambda i: (0, 0))
    return pl.pallas_call(
        _merge_body,
        grid=(t // tm,),
        in_specs=[row(d), gate(Z_GATE_R), gate(Z_GATE_S), gate(Z_GATE_A), row(512), row(512), row(512),
                  full(w_r), full(w_s), full(w_a), full(w_o)],
        out_specs=row(d),
        out_shape=jax.ShapeDtypeStruct((t, d), F32),
        compiler_params=_params("parallel"),
        name="branch_merge",
    )(h, z, z, z, y_r, y_s, y_a, w_r, w_s, w_a, w_o)


ID_NONE = 1 << 20


def _top16(vals, ids):
    out_v, out_i = [], []
    for _ in range(PEER_TOPK):
        m = jnp.max(vals, axis=0, keepdims=True)
        idx = jnp.min(jnp.where(vals == m, ids, ID_NONE), axis=0, keepdims=True)
        out_v.append(m)
        out_i.append(idx)
        vals = jnp.where(ids == idx, -jnp.inf, vals)
    return out_v, out_i


def _pair_blocks(first, second):
    rows1, stack1 = first
    rows2, stack2 = second
    blocks = [(stack1, rows2[0])]
    blocks += [(stack1[0:8], rows2[b]) for b in range(1, 8)]
    blocks += [(rows1[0], stack2[8:16])]
    return blocks


def _pair_ids(tm):
    a8 = lax.broadcasted_iota(jnp.int32, (8, tm), 0)
    a16 = lax.broadcasted_iota(jnp.int32, (PEER_TOPK, tm), 0)
    blocks = [a16 * PEER_TOPK]
    blocks += [jnp.where(a8 < PEER_TOPK // (b + 1), a8 * PEER_TOPK + b, ID_NONE) for b in range(1, 8)]
    blocks += [a8 + 8]
    return jnp.concatenate(blocks, axis=0)


def _route_body(h_ref, g_ref, wq_ref, keys_ref, hn_ref, eid_ref, gate_ref):
    hn = _rms(h_ref[...], g_ref[...])
    hn_ref[...] = _pack_halves(hn)
    q = jnp.dot(hn.astype(BF16), wq_ref[...], preferred_element_type=F32).astype(BF16)
    tm = q.shape[0]
    rows = lax.broadcasted_iota(jnp.int32, (PEER_KEYS, tm), 0)
    pair_ids = _pair_ids(tm)
    pair_ok = pair_ids != ID_NONE
    half = PEER_QDIM // 2
    eids, gates = [], []
    for hd in range(PEER_HEADS):
        sub = []
        for p in range(2):
            qs = q[:, (2 * hd + p) * half:(2 * hd + p + 1) * half]
            s = lax.dot_general(keys_ref[hd, p], qs, (((1,), (1,)), ((), ())), preferred_element_type=F32)
            sub.append(_top16(s, rows))
        (s1, i1), (s2, i2) = sub
        scores = _pair_blocks((s1, jnp.concatenate(s1, axis=0)), (s2, jnp.concatenate(s2, axis=0)))
        experts = _pair_blocks((i1, jnp.concatenate(i1, axis=0)), (i2, jnp.concatenate(i2, axis=0)))
        cand_s = jnp.where(pair_ok, jnp.concatenate([a + b for a, b in scores], axis=0), -jnp.inf)
        cand_e = jnp.concatenate([a * PEER_KEYS + b for a, b in experts], axis=0)
        top_s, sel = _top16(cand_s, pair_ids)
        top_e = [jnp.sum(jnp.where(pair_ids == i, cand_e, 0), axis=0, keepdims=True) for i in sel]
        ts_ = jnp.concatenate(top_s, axis=0)
        e = jnp.exp(ts_ - top_s[0])
        gates.append(e / jnp.sum(e, axis=0, keepdims=True))
        eids.append(jnp.concatenate(top_e, axis=0))
    eid_ref[...] = jnp.concatenate(eids, axis=0).T
    gate_ref[...] = jnp.concatenate(gates, axis=0).T


def _route(h, gain, wq_bf, keys_bf, tm):
    t, d = h.shape
    return pl.pallas_call(
        _route_body,
        grid=(t // tm,),
        in_specs=[pl.BlockSpec((tm, d), lambda i: (i, 0)),
                  pl.BlockSpec((1, d), lambda i: (0, 0)),
                  pl.BlockSpec(wq_bf.shape, lambda i: (0, 0)),
                  pl.BlockSpec(keys_bf.shape, lambda i: (0, 0, 0, 0))],
        out_specs=[pl.BlockSpec((tm, d // 2), lambda i: (i, 0)),
                   pl.BlockSpec((tm, PEER_PICKS), lambda i: (i, 0)),
                   pl.BlockSpec((tm, PEER_PICKS), lambda i: (i, 0))],
        out_shape=[jax.ShapeDtypeStruct((t, d // 2), jnp.int32),
                   jax.ShapeDtypeStruct((t, PEER_PICKS), jnp.int32),
                   jax.ShapeDtypeStruct((t, PEER_PICKS), F32)],
        compiler_params=_params("parallel"),
        name="peer_route",
    )(h, gain.reshape(1, d), wq_bf, keys_bf)


def _sc_mesh():
    return plsc.VectorSubcoreMesh(core_axis_name="core", subcore_axis_name="subcore")


def _sc_worker():
    return lax.axis_index("subcore") * SC_CORES + lax.axis_index("core")


def _sc_row_pipeline(tab_hbm, idx_v, buf, sems, compute):
    nslot, nrow = buf.shape[0], buf.shape[1]
    nsub = PEER_PICKS // nrow
    ng = SC_TOKENS * nsub

    def gather(g, slot):
        rows = idx_v.at[g // nsub, pl.ds((g % nsub) * nrow, nrow)]
        return pltpu.make_async_copy(tab_hbm.at[rows], buf.at[slot], sems.at[slot])

    for b in range(nslot - 1):
        gather(b, b).start()

    @pl.loop(0, ng, step=nslot)
    def _(g):
        for b in range(nslot):
            ahead = g + b + nslot - 1

            @pl.when(ahead < ng)
            def _():
                gather(ahead, (b + nslot - 1) % nslot).start()

            gather(g + b, b).wait()
            compute(g + b, b)


def _sc_split(words):
    return (lax.bitcast_convert_type(words & HI_MASK, F32), lax.bitcast_convert_type(words << 16, F32))


def _expert_scores(table, eid, hn):
    t, words = hn.shape
    per_worker = t // SC_WORKERS
    nslot, nrow = SC_SCORE_RING
    nsub = PEER_PICKS // nrow
    ln = SC_LANES

    @functools.partial(
        pl.kernel, mesh=_sc_mesh(),
        out_type=jax.ShapeDtypeStruct((t, PEER_PICKS), F32),
        scratch_types=[pltpu.VMEM((SC_TOKENS, PEER_PICKS), jnp.int32),
                       pltpu.VMEM((SC_TOKENS, words), jnp.int32),
                       pltpu.VMEM((SC_TOKENS, PEER_PICKS), F32),
                       pltpu.VMEM((nslot, nrow, words), jnp.int32),
                       pltpu.SemaphoreType.DMA((nslot,))],
        compiler_params=pltpu.CompilerParams(needs_layout_passes=False),
        name="peer_expert_scores")
    def run(tab_hbm, eid_hbm, hn_hbm, out_hbm, idx_v, x_v, a_v, buf, sems):
        lane = lax.iota(jnp.int32, ln)

        def compute(g, slot):
            tok = g // nsub
            sub = g % nsub
            for grp in range(nrow // ln):
                def body(c, accs):
                    c0 = pl.ds(2 * c * ln, ln)
                    c1 = pl.ds((2 * c + 1) * ln, ln)
                    x0 = plsc.bitcast(x_v[tok, c0], BF16)
                    x1 = plsc.bitcast(x_v[tok, c1], BF16)
                    out = []
                    for r in range(ln):
                        u0 = plsc.bitcast(buf[slot, grp * ln + r, c0], BF16)
                        u1 = plsc.bitcast(buf[slot, grp * ln + r, c1], BF16)
                        hi, lo = _sc_split(plsc.bitcast(u0 * x0 + u1 * x1, jnp.int32))
                        out.append(accs[r] + hi + lo)
                    return tuple(out)

                accs = lax.fori_loop(0, words // (2 * ln), body, tuple(jnp.zeros((ln,), F32) for _ in range(ln)))
                res = jnp.zeros((ln,), F32)
                for r in range(ln):
                    res = jnp.where(lane == r, jnp.sum(accs[r]), res)
                a_v[tok, pl.ds(sub * nrow + grp * ln, ln)] = res

        @pl.loop(0, per_worker // SC_TOKENS)
        def _(blk):
            tok0 = _sc_worker() * per_worker + blk * SC_TOKENS
            pltpu.sync_copy(eid_hbm.at[pl.ds(tok0, SC_TOKENS)], idx_v)
            pltpu.sync_copy(hn_hbm.at[pl.ds(tok0, SC_TOKENS)], x_v)
            _sc_row_pipeline(tab_hbm, idx_v, buf, sems, compute)
            pltpu.sync_copy(a_v, out_hbm.at[pl.ds(tok0, SC_TOKENS)])

    return run(table, eid, hn)


def _expert_mix(table, eid, w):
    t = eid.shape[0]
    words = table.shape[1]
    d = 2 * words
    per_worker = t // SC_WORKERS
    nslot, nrow = SC_MIX_RING
    nsub = PEER_PICKS // nrow
    ln = SC_LANES
    nvec = SC_WORDS // ln

    @functools.partial(
        pl.kernel, mesh=_sc_mesh(),
        out_type=jax.ShapeDtypeStruct((t, d), F32),
        scratch_types=[pltpu.VMEM((SC_TOKENS, PEER_PICKS), jnp.int32),
                       pltpu.VMEM((SC_TOKENS, PEER_PICKS), jnp.int32),
                       pltpu.VMEM((SC_TOKENS, d), F32),
                       pltpu.VMEM((nslot, nrow, words), jnp.int32),
                       pltpu.SemaphoreType.DMA((nslot,))],
        compiler_params=pltpu.CompilerParams(needs_layout_passes=False),
        name="peer_expert_mix")
    def run(tab_hbm, eid_hbm, w_hbm, out_hbm, idx_v, w_v, y_v, buf, sems):
        zero = jnp.zeros((ln,), F32)

        def compute(g, slot):
            tok = g // nsub
            sub = g % nsub
            tokv = jnp.full((ln,), tok, jnp.int32)

            def weight(r):
                pick = jnp.full((ln,), sub * nrow + r, jnp.int32)
                return plsc.bitcast(plsc.load_gather(w_v, [tokv, pick]), BF16)

            for cc in range(words // SC_WORDS):
                def body(rg, accs):
                    r0 = SC_MIX_GROUP * rg
                    ws = [weight(r0 + j) for j in range(SC_MIX_GROUP)]
                    out = list(accs)
                    for k in range(nvec):
                        cols = pl.ds(cc * SC_WORDS + k * ln, ln)
                        prod = ws[0] * plsc.bitcast(buf[slot, r0, cols], BF16)
                        for j in range(1, SC_MIX_GROUP):
                            prod = prod + ws[j] * plsc.bitcast(buf[slot, r0 + j, cols], BF16)
                        hi, lo = _sc_split(plsc.bitcast(prod, jnp.int32))
                        out[k] = accs[k] + hi
                        out[nvec + k] = accs[nvec + k] + lo
                    return tuple(out)

                lo_cols = [pl.ds(cc * SC_WORDS + k * ln, ln) for k in range(nvec)]
                hi_cols = [pl.ds(words + cc * SC_WORDS + k * ln, ln) for k in range(nvec)]
                init = tuple(y_v[tok, c] for c in lo_cols + hi_cols)
                accs = lax.fori_loop(0, nrow // SC_MIX_GROUP, body, init)
                for c, acc in zip(lo_cols + hi_cols, accs):
                    y_v[tok, c] = acc

        @pl.loop(0, per_worker // SC_TOKENS)
        def _(blk):
            tok0 = _sc_worker() * per_worker + blk * SC_TOKENS
            pltpu.sync_copy(eid_hbm.at[pl.ds(tok0, SC_TOKENS)], idx_v)
            pltpu.sync_copy(w_hbm.at[pl.ds(tok0, SC_TOKENS)], w_v)

            @pl.loop(0, SC_TOKENS)
            def _(tok):
                @pl.loop(0, d // ln)
                def _(c):
                    y_v[tok, pl.ds(c * ln, ln)] = zero

            _sc_row_pipeline(tab_hbm, idx_v, buf, sems, compute)
            pltpu.sync_copy(y_v, out_hbm.at[pl.ds(tok0, SC_TOKENS)])

    return run(table, eid, w)


def _pick_weights_body(a_ref, g_ref, o_ref):
    o_ref[...] = _pack_twice(g_ref[...] * _gelu(a_ref[...]))


def _pick_weights(a, gate, tm):
    t = a.shape[0]
    spec = pl.BlockSpec((tm, PEER_PICKS), lambda i: (i, 0))
    return pl.pallas_call(
        _pick_weights_body,
        grid=(t // tm,),
        in_specs=[spec, spec],
        out_specs=spec,
        out_shape=jax.ShapeDtypeStruct(a.shape, jnp.int32),
        compiler_params=_params("parallel"),
        name="peer_pick_weights",
    )(a, gate)


def _ple_body(h_ref, y_ref, g_ref, wg_ref, p_ref, wp_ref, o_ref):
    h = h_ref[...] + y_ref[...]
    hn = _rms(h, g_ref[...]).astype(BF16)
    gate = _sigmoid(jnp.dot(hn, wg_ref[...], preferred_element_type=F32))
    emb = jnp.dot(p_ref[...].astype(BF16), wp_ref[...], preferred_element_type=F32)
    o_ref[...] = h + gate * emb


def _ple(h, y, gain, wg_bf, p, wp_bf, tm):
    t, d = h.shape
    return pl.pallas_call(
        _ple_body,
        grid=(t // tm,),
        in_specs=[pl.BlockSpec((tm, d), lambda i: (i, 0)),
                  pl.BlockSpec((tm, d), lambda i: (i, 0)),
                  pl.BlockSpec((1, d), lambda i: (0, 0)),
                  pl.BlockSpec(wg_bf.shape, lambda i: (0, 0)),
                  pl.BlockSpec((tm, p.shape[1]), lambda i: (i, 0)),
                  pl.BlockSpec(wp_bf.shape, lambda i: (0, 0))],
        out_specs=pl.BlockSpec((tm, d), lambda i: (i, 0)),
        out_shape=jax.ShapeDtypeStruct((t, d), F32),
        compiler_params=_params("parallel"),
        name="layer_embedding",
    )(h, y, gain.reshape(1, d), wg_bf, p, wp_bf)


def _rope_tables(seq, dim):
    inv = 1.0 / (ROPE_THETA ** (jnp.arange(0, dim, 2, dtype=F32) / dim))
    ang = jnp.arange(seq, dtype=F32)[:, None] * inv[None, :]
    return jnp.cos(ang), jnp.sin(ang)


def _pack_table(tab):
    n = tab.shape[1] // 2
    b = lax.bitcast_convert_type(tab.astype(BF16), jnp.uint16).astype(jnp.uint32)
    return lax.bitcast_convert_type((b[:, :n] << 16) | b[:, n:], jnp.int32)


def _permute_in_columns(w_in):
    return jnp.concatenate([w_in[:, 3840:], w_in[:, :3840]], axis=1)


def kernel(x, p, norm_mix, w_in, ret_decay, ret_norm, sgu_ln_g, sgu_ln_b, sgu_w, sgu_b, att_q_norm, att_k_norm, att_sink, w_proj_ret, w_proj_sgu, w_proj_att, w_out, norm_ffn, peer_wq, peer_keys, peer_u, peer_v, norm_ple, ple_gate, ple_proj):
    batch, seq, d = x.shape
    depth = w_in.shape[0]
    groups = PIPELINE_GROUPS if batch % PIPELINE_GROUPS == 0 else 1
    gb = batch // groups
    t = gb * seq
    ts = min(512, seq)
    tm = min(512, t)
    cos_r, sin_r = _rope_tables(seq, RET_DK)
    cos_r2 = jnp.concatenate([cos_r, cos_r], axis=1)
    sin_r2 = jnp.concatenate([-sin_r, sin_r], axis=1)
    cos_a, sin_a = _rope_tables(seq, ATT_DH)
    lw = []
    for i in range(depth):
        lw.append((_permute_in_columns(w_in[i]).astype(BF16), w_proj_ret[i].astype(BF16),
                   w_proj_sgu[i].astype(BF16), w_proj_att[i].astype(BF16), w_out[i].astype(BF16),
                   peer_wq[i].astype(BF16), peer_keys[i].astype(BF16), ple_gate[i].astype(BF16),
                   ple_proj[i].astype(BF16), _pack_table(peer_u[i]), _pack_table(peer_v[i])))
    def mixers(i, h):
        w_in_bf, w_r, w_s, w_a, w_o, w_q, keys, _, _, tab_u, _ = lw[i]
        z = _in_projection(h, norm_mix[i], w_in_bf, tm)
        y_r = _retention(z, ret_decay[i], ret_norm[i], cos_r2, sin_r2, gb, seq, ts)
        y_s = _sgu(z, sgu_ln_g[i], sgu_ln_b[i], sgu_w[i], sgu_b[i], ts)
        y_a = _attention(z, att_q_norm[i], att_k_norm[i], att_sink[i], cos_a, sin_a, gb, seq, ts)
        h = _merge(h, z, y_r, y_s, y_a, w_r, w_s, w_a, w_o, tm)
        hn, eid, gate = _route(h, norm_ffn[i], w_q, keys, min(256, t))
        return dict(h=h, eid=eid, gate=gate, a=_expert_scores(tab_u, eid, hn))

    def finish(i, g):
        p_g = p[i, g * gb:(g + 1) * gb].reshape(t, -1)
        return _ple(state[(i, g)]["h"], mixed[(i, g)], norm_ple[i], lw[i][7], p_g, lw[i][8], tm)

    jobs = [(i, g) for i in range(depth) for g in range(groups)]
    hs = [x[g * gb:(g + 1) * gb].reshape(t, d) for g in range(groups)]
    state, mixed = {}, {}

    def issue_mix(job, tied):
        if job in mixed:
            return tied
        w = _pick_weights(state[job]["a"], state[job]["gate"], tm)
        if tied is not None:
            tied, w = lax.optimization_barrier((tied, w))
        mixed[job] = _expert_mix(lw[job[0]][10], state[job]["eid"], w)
        return tied

    for k, (i, g) in enumerate(jobs):
        if i > 0:
            issue_mix((i - 1, g), None)
            hs[g] = finish(i - 1, g)
        h_in = hs[g]
        if k >= PIPELINE_LAG:
            h_in = issue_mix(jobs[k - PIPELINE_LAG], h_in)
        state[(i, g)] = mixers(i, h_in)
    for job in jobs:
        issue_mix(job, None)
    out = [finish(depth - 1, g) for g in range(groups)]
    return jnp.concatenate(out, axis=0).reshape(batch, seq, d)
```

```python
import functools
import math

import jax
import jax.numpy as jnp
from jax import lax
from jax.experimental import pallas as pl
from jax.experimental.pallas import tpu as pltpu
from jax.experimental.pallas import tpu_sc as plsc

F32 = jnp.float32
BF16 = jnp.bfloat16

D_MODEL = 1024
PLE_DIM = 256
CHUNK = 128
EPS = 1e-6
ROPE_THETA = 10000.0
RET_HEADS = 4
RET_DK = 128
SGU_GROUPS = 4
SGU_WIDTH = 512
ATT_HEADS = 8
ATT_KV_HEADS = 2
ATT_DH = 64
ATT_GROUP = ATT_HEADS // ATT_KV_HEADS
NEG_INF = -1e30
PEER_HEADS = 8
PEER_KEYS = 128
PEER_QDIM = 256
PEER_TOPK = 16
PEER_PICKS = PEER_HEADS * PEER_TOPK

Z_GATE_R, Z_GATE_S, Z_GATE_A = 0, 1024, 2048
Z_QR, Z_KR, Z_VR, Z_GR = 3072, 3584, 4096, 4608
Z_US, Z_VS = 5120, 5632
Z_QA, Z_KA, Z_VA = 6144, 6656, 6784
D_IN = 6912

LANE = 128
SC_CORES = 2
SC_SUBCORES = 16
SC_LANES = 16
SC_WORKERS = SC_CORES * SC_SUBCORES
SC_SCORE_BLOCK = 64
SC_MIX_BLOCK = 32
SC_SCORE_RING = (4, 32)
SC_MIX_RING = (2, 64)
SC_MIX_GROUP = 4
SC_WORDS = 128
HI_MASK = -65536
VMEM_LIMIT = 56 * 1024 * 1024
PIPELINE_GROUPS = 8


def _params(*sem):
    return pltpu.CompilerParams(dimension_semantics=sem, vmem_limit_bytes=VMEM_LIMIT)


def _gelu(x):
    return 0.5 * x * (1.0 + lax.erf(x * (1.0 / math.sqrt(2.0))))


def _sigmoid(x):
    return 1.0 / (1.0 + jnp.exp(-x))


def _rms(x, g):
    return x * lax.rsqrt(jnp.mean(x * x, axis=-1, keepdims=True) + EPS) * g


def _bf16_hi_bits(x):
    b = lax.bitcast_convert_type(x, jnp.int32)
    return (b + 0x7FFF + ((b >> 16) & 1)) & HI_MASK


def _pack_halves(x):
    n = x.shape[1] // 2
    return _bf16_hi_bits(x[:, :n]) | lax.shift_right_logical(_bf16_hi_bits(x[:, n:]), 16)


def _pack_twice(x):
    b = _bf16_hi_bits(x)
    return b | lax.shift_right_logical(b, 16)


INPROJ_COLS = 1152


def _inproj_body(x_ref, g_ref, w_ref, o_ref):
    xn = _rms(x_ref[...], g_ref[...]).astype(BF16)
    for j in range(w_ref.shape[1] // INPROJ_COLS):
        cols = slice(j * INPROJ_COLS, (j + 1) * INPROJ_COLS)
        o_ref[:, cols] = jnp.dot(xn, w_ref[:, cols], preferred_element_type=F32).astype(o_ref.dtype)


def _in_projection(h, gain, w_bf, tm):
    t, d = h.shape
    n = w_bf.shape[1]
    return pl.pallas_call(
        _inproj_body,
        grid=(t // tm,),
        in_specs=[pl.BlockSpec((tm, d), lambda i: (i, 0)),
                  pl.BlockSpec((1, d), lambda i: (0, 0)),
                  pl.BlockSpec((d, n), lambda i: (0, 0))],
        out_specs=pl.BlockSpec((tm, n), lambda i: (i, 0)),
        out_shape=jax.ShapeDtypeStruct((t, n), BF16),
        compiler_params=_params("parallel"),
        name="in_projection",
    )(h, gain.reshape(1, d), w_bf)


def _rope128(x, cos, sin_signed):
    return x * cos + pltpu.roll(x, 64, 1) * sin_signed


def _ret_bwd_body(q_ref, k_ref, v_ref, cos_ref, sin_ref, qw_ref, kw_ref, cd_ref, o_ref, st_ref, *, nchunk):
    @pl.when(pl.program_id(2) == 0)
    def _():
        st_ref[...] = jnp.zeros_like(st_ref)

    qw = qw_ref[0]
    kw = kw_ref[0]
    cd = cd_ref[0, 0:1, :]
    for c in reversed(range(nchunk)):
        rows = pl.ds(c * CHUNK, CHUNK)
        cos = cos_ref[rows, :]
        sin = sin_ref[rows, :]
        q = _rope128(q_ref[rows, :].astype(F32), cos, sin)
        k = _rope128(k_ref[rows, :].astype(F32), cos, sin) * (RET_DK ** -0.5)
        v = v_ref[rows, :]
        st = st_ref[...]
        o_ref[rows, :] = jnp.dot((q * qw).astype(BF16), st.astype(BF16), preferred_element_type=F32)
        kv = jnp.dot((k * kw).T.astype(BF16), v, preferred_element_type=F32)
        st_ref[...] = st * cd + kv


def _ret_fwd_body(q_ref, k_ref, v_ref, g_ref, yb_ref, cos_ref, sin_ref, dm_ref, qw_ref, kw_ref, cd_ref,
                  gn_ref, o_ref, st_ref, *, nchunk):
    @pl.when(pl.program_id(2) == 0)
    def _():
        st_ref[...] = jnp.zeros_like(st_ref)

    qw = qw_ref[0]
    kw = kw_ref[0]
    cd = cd_ref[0, 0:1, :]
    dm = dm_ref[0]
    gn = gn_ref[0, 0:1, :]
    for c in range(nchunk):
        rows = pl.ds(c * CHUNK, CHUNK)
        cos = cos_ref[rows, :]
        sin = sin_ref[rows, :]
        q = _rope128(q_ref[rows, :].astype(F32), cos, sin)
        k = _rope128(k_ref[rows, :].astype(F32), cos, sin) * (RET_DK ** -0.5)
        v = v_ref[rows, :]
        st = st_ref[...]
        s = lax.dot_general(q.astype(BF16), k.astype(BF16), (((1,), (1,)), ((), ())),
                            preferred_element_type=F32) * dm
        y = jnp.dot(s.astype(BF16), v, preferred_element_type=F32)
        y += jnp.dot((q * qw).astype(BF16), st.astype(BF16), preferred_element_type=F32)
        y += yb_ref[rows, :]
        kv = jnp.dot((k * kw).T.astype(BF16), v, preferred_element_type=F32)
        st_ref[...] = st * cd + kv
        y = y * lax.rsqrt(jnp.mean(y * y, axis=-1, keepdims=True) + EPS) * gn
        g = g_ref[rows, :].astype(F32)
        o_ref[rows, :] = (g * _sigmoid(g) * y).astype(o_ref.dtype)


def _retention(z, ret_decay, ret_norm, cos, sin, batch, seq, ts):
    t = z.shape[0]
    nchunk = ts // CHUNK
    nstep = seq // ts
    hd = RET_HEADS
    log_g = jax.nn.log_sigmoid(ret_decay.astype(F32))
    idx = jnp.arange(CHUNK, dtype=F32)
    diff = idx[:, None] - idx[None, :]
    lf = log_g[0][:, None, None]
    lb = log_g[1][:, None, None]
    dmat = jnp.where(diff[None] >= 0, jnp.exp(lf * jnp.maximum(diff, 0.0)[None]),
                     jnp.exp(lb * jnp.maximum(-diff, 0.0)[None]))
    bc = lambda a: jnp.broadcast_to(a[:, :, None], (hd, a.shape[1], LANE))
    qw_f = bc(jnp.exp(log_g[0][:, None] * (idx + 1.0)[None, :]))
    kw_f = bc(jnp.exp(log_g[0][:, None] * (CHUNK - 1 - idx)[None, :]))
    qw_b = bc(jnp.exp(log_g[1][:, None] * (CHUNK - idx)[None, :]))
    kw_b = bc(jnp.exp(log_g[1][:, None] * idx[None, :]))
    cd_f = jnp.broadcast_to(jnp.exp(log_g[0] * CHUNK)[:, None, None], (hd, 8, LANE))
    cd_b = jnp.broadcast_to(jnp.exp(log_g[1] * CHUNK)[:, None, None], (hd, 8, LANE))
    gn = jnp.broadcast_to(ret_norm.astype(F32).reshape(hd, 1, LANE), (hd, 8, LANE))

    def zspec(col0, rev):
        cb = col0 // LANE
        if rev:
            return pl.BlockSpec((ts, LANE), lambda b, h, s: (b * nstep + nstep - 1 - s, cb + h))
        return pl.BlockSpec((ts, LANE), lambda b, h, s: (b * nstep + s, cb + h))

    def tspec(rev):
        if rev:
            return pl.BlockSpec((ts, LANE), lambda b, h, s: (nstep - 1 - s, 0))
        return pl.BlockSpec((ts, LANE), lambda b, h, s: (s, 0))

    hspec = lambda r: pl.BlockSpec((1, r, LANE), lambda b, h, s: (h, 0, 0))

    yb = pl.pallas_call(
        functools.partial(_ret_bwd_body, nchunk=nchunk),
        grid=(batch, hd, nstep),
        in_specs=[zspec(Z_QR, True), zspec(Z_KR, True), zspec(Z_VR, True), tspec(True), tspec(True),
                  hspec(CHUNK), hspec(CHUNK), hspec(8)],
        out_specs=pl.BlockSpec((ts, LANE), lambda b, h, s: (b * nstep + nstep - 1 - s, h)),
        out_shape=jax.ShapeDtypeStruct((t, hd * LANE), F32),
        scratch_shapes=[pltpu.VMEM((RET_DK, LANE), F32)],
        compiler_params=_params("parallel", "parallel", "arbitrary"),
        name="retention_bwd",
    )(z, z, z, cos, sin, qw_b, kw_b, cd_b)

    return pl.pallas_call(
        functools.partial(_ret_fwd_body, nchunk=nchunk),
        grid=(batch, hd, nstep),
        in_specs=[zspec(Z_QR, False), zspec(Z_KR, False), zspec(Z_VR, False), zspec(Z_GR, False),
                  pl.BlockSpec((ts, LANE), lambda b, h, s: (b * nstep + s, h)),
                  tspec(False), tspec(False), hspec(CHUNK), hspec(CHUNK), hspec(CHUNK), hspec(8), hspec(8)],
        out_specs=pl.BlockSpec((ts, LANE), lambda b, h, s: (b * nstep + s, h)),
        out_shape=jax.ShapeDtypeStruct((t, hd * LANE), BF16),
        scratch_shapes=[pltpu.VMEM((RET_DK, LANE), F32)],
        compiler_params=_params("parallel", "parallel", "arbitrary"),
        name="retention_fwd",
    )(z, z, z, z, yb, cos, sin, dmat, qw_f, kw_f, cd_f, gn)


def _sgu_body(u_ref, v_ref, lg_ref, lb_ref, w_ref, b_ref, o_ref, *, nchunk):
    lg = lg_ref[...]
    lb = lb_ref[...]
    for c in range(nchunk):
        rows = pl.ds(c * CHUNK, CHUNK)
        vf = _gelu(v_ref[rows, :].astype(F32))
        mu = jnp.mean(vf, axis=-1, keepdims=True)
        vc = vf - mu
        var = jnp.mean(vc * vc, axis=-1, keepdims=True)
        vn = (vc * lax.rsqrt(var + EPS) * lg + lb).astype(BF16)
        for g in range(SGU_GROUPS):
            cols = slice(g * LANE, (g + 1) * LANE)
            mixed = jnp.dot(w_ref[g], vn[:, cols], preferred_element_type=F32) + b_ref[g]
            uf = _gelu(u_ref[rows, cols].astype(F32))
            o_ref[rows, cols] = (uf * mixed).astype(o_ref.dtype)


def _sgu(z, ln_g, ln_b, w_s, b_s, ts):
    t = z.shape[0]
    bias = jnp.broadcast_to(b_s.astype(F32)[:, :, None], (SGU_GROUPS, CHUNK, LANE))
    return pl.pallas_call(
        functools.partial(_sgu_body, nchunk=ts // CHUNK),
        grid=(t // ts,),
        in_specs=[pl.BlockSpec((ts, SGU_WIDTH), lambda i: (i, Z_US // SGU_WIDTH)),
                  pl.BlockSpec((ts, SGU_WIDTH), lambda i: (i, Z_VS // SGU_WIDTH)),
                  pl.BlockSpec((1, SGU_WIDTH), lambda i: (0, 0)),
                  pl.BlockSpec((1, SGU_WIDTH), lambda i: (0, 0)),
                  pl.BlockSpec((SGU_GROUPS, CHUNK, CHUNK), lambda i: (0, 0, 0)),
                  pl.BlockSpec((SGU_GROUPS, CHUNK, LANE), lambda i: (0, 0, 0))],
        out_specs=pl.BlockSpec((ts, SGU_WIDTH), lambda i: (i, 0)),
        out_shape=jax.ShapeDtypeStruct((t, SGU_WIDTH), BF16),
        compiler_params=_params("parallel"),
        name="spatial_gating",
    )(z, z, ln_g.reshape(1, -1).astype(F32), ln_b.reshape(1, -1).astype(F32), w_s.astype(BF16), bias)


def _pair_norm_rope(x, gain, cos, sin_up, sin_dn, low):
    sq = x * x
    lo = jnp.sum(jnp.where(low, sq, 0.0), axis=-1, keepdims=True)
    hi = jnp.sum(sq, axis=-1, keepdims=True) - lo
    ms = jnp.where(low, lo, hi) * (1.0 / ATT_DH)
    xn = x * lax.rsqrt(ms + EPS) * gain
    return xn * cos + pltpu.roll(xn, LANE - 32, 1) * sin_up + pltpu.roll(xn, 32, 1) * sin_dn


def _attn_body(sink_ref, q_ref, kp_ref, k_ref, kn_ref, vp_ref, v_ref, vn_ref,
               cq_ref, suq_ref, sdq_ref, ckp_ref, sukp_ref, sdkp_ref, ckn_ref, sukn_ref, sdkn_ref,
               qg_ref, kg_ref, o_ref, *, nchunk, nstep):
    s_id = pl.program_id(1)
    ts = nchunk * CHUNK
    lane = lax.broadcasted_iota(jnp.int32, (1, LANE), 1)
    low = lane < ATT_DH
    kg = kg_ref[...]
    qg = qg_ref[...]
    k_ext = jnp.concatenate([
        _pair_norm_rope(kp_ref[...].astype(F32), kg, ckp_ref[...], sukp_ref[...], sdkp_ref[...], low),
        _pair_norm_rope(k_ref[...].astype(F32), kg, cq_ref[...], suq_ref[...], sdq_ref[...], low),
        _pair_norm_rope(kn_ref[...].astype(F32), kg, ckn_ref[...], sukn_ref[...], sdkn_ref[...], low)], axis=0)
    v_ext = jnp.concatenate([vp_ref[...], v_ref[...], vn_ref[...]], axis=0)
    k_lo = jnp.where(low, k_ext, 0.0).astype(BF16)
    k_hi = jnp.where(low, 0.0, k_ext).astype(BF16)
    qi = lax.broadcasted_iota(jnp.int32, (CHUNK, 3 * CHUNK), 0)
    kj = lax.broadcasted_iota(jnp.int32, (CHUNK, 3 * CHUNK), 1)
    band = jnp.abs(qi + CHUNK - kj) <= CHUNK
    for c in range(nchunk):
        rows = pl.ds(c * CHUNK, CHUNK)
        first = jnp.logical_and(s_id == 0, c == 0)
        last = jnp.logical_and(s_id == nstep - 1, c == nchunk - 1)
        valid = band
        if c == 0:
            valid = jnp.logical_and(valid, jnp.logical_or(kj >= CHUNK, jnp.logical_not(first)))
        if c == nchunk - 1:
            valid = jnp.logical_and(valid, jnp.logical_or(kj < 2 * CHUNK, jnp.logical_not(last)))
        kc_lo = k_lo[c * CHUNK:(c + 3) * CHUNK]
        kc_hi = k_hi[c * CHUNK:(c + 3) * CHUNK]
        vc = v_ext[c * CHUNK:(c + 3) * CHUNK]
        cos = cq_ref[rows, :]
        su = suq_ref[rows, :]
        sd = sdq_ref[rows, :]
        for pair in range(ATT_HEADS // 2):
            cols = slice(pair * LANE, (pair + 1) * LANE)
            qp = _pair_norm_rope(q_ref[rows, cols].astype(F32), qg, cos, su, sd, low) * (ATT_DH ** -0.5)
            kv_head = (2 * pair) // ATT_GROUP
            outs = []
            for half in range(2):
                head = 2 * pair + half
                qh = qp if half == kv_head else pltpu.roll(qp, ATT_DH, 1)
                if kv_head == 0:
                    qh = jnp.where(low, qh, 0.0)
                    kc = kc_lo
                else:
                    qh = jnp.where(low, 0.0, qh)
                    kc = kc_hi
                s = lax.dot_general(qh.astype(BF16), kc, (((1,), (1,)), ((), ())), preferred_element_type=F32)
                s = jnp.where(valid, s, NEG_INF)
                sk = sink_ref[head]
                m = jnp.maximum(jnp.max(s, axis=-1, keepdims=True), sk)
                e = jnp.exp(s - m)
                den = jnp.sum(e, axis=-1, keepdims=True) + jnp.exp(sk - m)
                o = jnp.dot(e.astype(BF16), vc, preferred_element_type=F32) / den
                outs.append(o if half == kv_head else pltpu.roll(o, ATT_DH, 1))
            o_ref[rows, cols] = jnp.where(low, outs[0], outs[1]).astype(o_ref.dtype)


def _attention(z, q_gain, k_gain, sink, cos, sin, batch, seq, ts):
    t = z.shape[0]
    nchunk = ts // CHUNK
    nstep = seq // ts
    nblk = seq // CHUNK
    cos2 = jnp.tile(jnp.concatenate([cos, cos], axis=1), (1, 2))
    zero = jnp.zeros_like(sin)
    sin_up = jnp.tile(jnp.concatenate([-sin, zero], axis=1), (1, 2))
    sin_dn = jnp.tile(jnp.concatenate([zero, sin], axis=1), (1, 2))
    qg = jnp.tile(q_gain.astype(F32), 2).reshape(1, LANE)
    kg = jnp.tile(k_gain.astype(F32), 2).reshape(1, LANE)

    kcb, vcb = Z_KA // LANE, Z_VA // LANE
    prev_blk = lambda s: jnp.maximum(s * nchunk - 1, 0)
    next_blk = lambda s: jnp.minimum((s + 1) * nchunk, nblk - 1)
    main = lambda cb: pl.BlockSpec((ts, LANE), lambda b, s: (b * nstep + s, cb))
    prev = lambda cb: pl.BlockSpec((CHUNK, LANE), lambda b, s: (b * nblk + prev_blk(s), cb))
    nxt = lambda cb: pl.BlockSpec((CHUNK, LANE), lambda b, s: (b * nblk + next_blk(s), cb))
    tmain = pl.BlockSpec((ts, LANE), lambda b, s: (s, 0))
    tprev = pl.BlockSpec((CHUNK, LANE), lambda b, s: (prev_blk(s), 0))
    tnext = pl.BlockSpec((CHUNK, LANE), lambda b, s: (next_blk(s), 0))
    one = pl.BlockSpec((1, LANE), lambda b, s: (0, 0))
    return pl.pallas_call(
        functools.partial(_attn_body, nchunk=nchunk, nstep=nstep),
        grid=(batch, nstep),
        in_specs=[pl.BlockSpec(memory_space=pltpu.SMEM),
                  pl.BlockSpec((ts, ATT_HEADS * ATT_DH), lambda b, s: (b * nstep + s, Z_QA // 512)),
                  prev(kcb), main(kcb), nxt(kcb), prev(vcb), main(vcb), nxt(vcb),
                  tmain, tmain, tmain, tprev, tprev, tprev, tnext, tnext, tnext, one, one],
        out_specs=pl.BlockSpec((ts, ATT_HEADS * ATT_DH), lambda b, s: (b * nstep + s, 0)),
        out_shape=jax.ShapeDtypeStruct((t, ATT_HEADS * ATT_DH), BF16),
        compiler_params=_params("parallel", "arbitrary"),
        name="window_attention",
    )(sink.astype(F32), z, z, z, z, z, z, z,
      cos2, sin_up, sin_dn, cos2, sin_up, sin_dn, cos2, sin_up, sin_dn, qg, kg)


def _merge_body(h_ref, gr_ref, gs_ref, ga_ref, yr_ref, ys_ref, ya_ref, wr_ref, ws_ref, wa_ref, wo_ref, o_ref):
    m = _sigmoid(gr_ref[...].astype(F32)) * jnp.dot(yr_ref[...], wr_ref[...], preferred_element_type=F32)
    m += _sigmoid(gs_ref[...].astype(F32)) * jnp.dot(ys_ref[...], ws_ref[...], preferred_element_type=F32)
    m += _sigmoid(ga_ref[...].astype(F32)) * jnp.dot(ya_ref[...], wa_ref[...], preferred_element_type=F32)
    o_ref[...] = h_ref[...] + jnp.dot(m.astype(BF16), wo_ref[...], preferred_element_type=F32)


def _merge(h, z, y_r, y_s, y_a, w_r, w_s, w_a, w_o, tm):
    t, d = h.shape
    row = lambda w: pl.BlockSpec((tm, w), lambda i: (i, 0))
    gate = lambda col0: pl.BlockSpec((tm, d), lambda i: (i, col0 // d))
    full = lambda a: pl.BlockSpec(a.shape, lambda i: (0, 0))
    return pl.pallas_call(
        _merge_body,
        grid=(t // tm,),
        in_specs=[row(d), gate(Z_GATE_R), gate(Z_GATE_S), gate(Z_GATE_A), row(512), row(512), row(512),
                  full(w_r), full(w_s), full(w_a), full(w_o)],
        out_specs=row(d),
        out_shape=jax.ShapeDtypeStruct((t, d), F32),
        compiler_params=_params("parallel"),
        name="branch_merge",
    )(h, z, z, z, y_r, y_s, y_a, w_r, w_s, w_a, w_o)


ID_NONE = 1 << 20


def _top16(vals, ids):
    out_v, out_i = [], []
    for _ in range(PEER_TOPK):
        m = jnp.max(vals, axis=0, keepdims=True)
        idx = jnp.min(jnp.where(vals == m, ids, ID_NONE), axis=0, keepdims=True)
        out_v.append(m)
        out_i.append(idx)
        vals = jnp.where(ids == idx, -jnp.inf, vals)
    return out_v, out_i


def _pair_blocks(first, second):
    rows1, stack1 = first
    rows2, stack2 = second
    blocks = [(stack1, rows2[0])]
    blocks += [(stack1[0:8], rows2[b]) for b in range(1, 8)]
    blocks += [(rows1[0], stack2[8:16])]
    return blocks


def _pair_ids(tm):
    a8 = lax.broadcasted_iota(jnp.int32, (8, tm), 0)
    a16 = lax.broadcasted_iota(jnp.int32, (PEER_TOPK, tm), 0)
    blocks = [a16 * PEER_TOPK]
    blocks += [jnp.where(a8 < PEER_TOPK // (b + 1), a8 * PEER_TOPK + b, ID_NONE) for b in range(1, 8)]
    blocks += [a8 + 8]
    return jnp.concatenate(blocks, axis=0)


def _route_body(h_ref, g_ref, wq_ref, keys_ref, hn_ref, eid_ref, gate_ref):
    hn = _rms(h_ref[...], g_ref[...])
    hn_ref[...] = _pack_halves(hn)
    q = jnp.dot(hn.astype(BF16), wq_ref[...], preferred_element_type=F32).astype(BF16)
    tm = q.shape[0]
    rows = lax.broadcasted_iota(jnp.int32, (PEER_KEYS, tm), 0)
    pair_ids = _pair_ids(tm)
    pair_ok = pair_ids != ID_NONE
    half = PEER_QDIM // 2
    eids, gates = [], []
    for hd in range(PEER_HEADS):
        sub = []
        for p in range(2):
            qs = q[:, (2 * hd + p) * half:(2 * hd + p + 1) * half]
            s = lax.dot_general(keys_ref[hd, p], qs, (((1,), (1,)), ((), ())), preferred_element_type=F32)
            sub.append(_top16(s, rows))
        (s1, i1), (s2, i2) = sub
        scores = _pair_blocks((s1, jnp.concatenate(s1, axis=0)), (s2, jnp.concatenate(s2, axis=0)))
        experts = _pair_blocks((i1, jnp.concatenate(i1, axis=0)), (i2, jnp.concatenate(i2, axis=0)))
        cand_s = jnp.where(pair_ok, jnp.concatenate([a + b for a, b in scores], axis=0), -jnp.inf)
        cand_e = jnp.concatenate([a * PEER_KEYS + b for a, b in experts], axis=0)
        top_s, sel = _top16(cand_s, pair_ids)
        top_e = [jnp.sum(jnp.where(pair_ids == i, cand_e, 0), axis=0, keepdims=True) for i in sel]
        ts_ = jnp.concatenate(top_s, axis=0)
        e = jnp.exp(ts_ - top_s[0])
        gates.append(e / jnp.sum(e, axis=0, keepdims=True))
        eids.append(jnp.concatenate(top_e, axis=0))
    eid_ref[...] = jnp.concatenate(eids, axis=0).T
    gate_ref[...] = jnp.concatenate(gates, axis=0).T


def _route(h, gain, wq_bf, keys_bf, tm):
    t, d = h.shape
    return pl.pallas_call(
        _route_body,
        grid=(t // tm,),
        in_specs=[pl.BlockSpec((tm, d), lambda i: (i, 0)),
                  pl.BlockSpec((1, d), lambda i: (0, 0)),
                  pl.BlockSpec(wq_bf.shape, lambda i: (0, 0)),
                  pl.BlockSpec(keys_bf.shape, lambda i: (0, 0, 0, 0))],
        out_specs=[pl.BlockSpec((tm, d // 2), lambda i: (i, 0)),
                   pl.BlockSpec((tm, PEER_PICKS), lambda i: (i, 0)),
                   pl.BlockSpec((tm, PEER_PICKS), lambda i: (i, 0))],
        out_shape=[jax.ShapeDtypeStruct((t, d // 2), jnp.int32),
                   jax.ShapeDtypeStruct((t, PEER_PICKS), jnp.int32),
                   jax.ShapeDtypeStruct((t, PEER_PICKS), F32)],
        compiler_params=_params("parallel"),
        name="peer_route",
    )(h, gain.reshape(1, d), wq_bf, keys_bf)


def _sc_mesh():
    return plsc.VectorSubcoreMesh(core_axis_name="core", subcore_axis_name="subcore")


def _sc_worker():
    return lax.axis_index("subcore") * SC_CORES + lax.axis_index("core")


def _sc_row_pipeline(tab_hbm, idx_v, buf, sems, compute):
    nslot, nrow = buf.shape[0], buf.shape[1]
    nsub = PEER_PICKS // nrow
    ng = idx_v.shape[0] * nsub

    def gather(g, slot):
        rows = idx_v.at[g // nsub, pl.ds((g % nsub) * nrow, nrow)]
        return pltpu.make_async_copy(tab_hbm.at[rows], buf.at[slot], sems.at[slot])

    for b in range(nslot - 1):
        gather(b, b).start()

    @pl.loop(0, ng, step=nslot)
    def _(g):
        for b in range(nslot):
            ahead = g + b + nslot - 1

            @pl.when(ahead < ng)
            def _():
                gather(ahead, (b + nslot - 1) % nslot).start()

            gather(g + b, b).wait()
            compute(g + b, b)


def _sc_split(words):
    return (lax.bitcast_convert_type(words & HI_MASK, F32), lax.bitcast_convert_type(words << 16, F32))


def _expert_scores(table, eid, hn):
    t, words = hn.shape
    per_worker = t // SC_WORKERS
    nslot, nrow = SC_SCORE_RING
    ntok = SC_SCORE_BLOCK
    nsub = PEER_PICKS // nrow
    ln = SC_LANES

    @functools.partial(
        pl.kernel, mesh=_sc_mesh(),
        out_type=jax.ShapeDtypeStruct((t, PEER_PICKS), F32),
        scratch_types=[pltpu.VMEM((ntok, PEER_PICKS), jnp.int32),
                       pltpu.VMEM((ntok, words), jnp.int32),
                       pltpu.VMEM((ntok, PEER_PICKS), F32),
                       pltpu.VMEM((nslot, nrow, words), jnp.int32),
                       pltpu.SemaphoreType.DMA((nslot,))],
        compiler_params=pltpu.CompilerParams(needs_layout_passes=False),
        name="peer_expert_scores")
    def run(tab_hbm, eid_hbm, hn_hbm, out_hbm, idx_v, x_v, a_v, buf, sems):
        lane = lax.iota(jnp.int32, ln)

        def compute(g, slot):
            tok = g // nsub
            sub = g % nsub
            for grp in range(nrow // ln):
                def body(c, accs):
                    c0 = pl.ds(2 * c * ln, ln)
                    c1 = pl.ds((2 * c + 1) * ln, ln)
                    x0 = plsc.bitcast(x_v[tok, c0], BF16)
                    x1 = plsc.bitcast(x_v[tok, c1], BF16)
                    out = []
                    for r in range(ln):
                        u0 = plsc.bitcast(buf[slot, grp * ln + r, c0], BF16)
                        u1 = plsc.bitcast(buf[slot, grp * ln + r, c1], BF16)
                        hi, lo = _sc_split(plsc.bitcast(u0 * x0 + u1 * x1, jnp.int32))
                        out.append(accs[r] + hi + lo)
                    return tuple(out)

                accs = lax.fori_loop(0, words // (2 * ln), body, tuple(jnp.zeros((ln,), F32) for _ in range(ln)))
                res = jnp.zeros((ln,), F32)
                for r in range(ln):
                    res = jnp.where(lane == r, jnp.sum(accs[r]), res)
                a_v[tok, pl.ds(sub * nrow + grp * ln, ln)] = res

        @pl.loop(0, per_worker // ntok)
        def _(blk):
            tok0 = _sc_worker() * per_worker + blk * ntok
            pltpu.sync_copy(eid_hbm.at[pl.ds(tok0, ntok)], idx_v)
            pltpu.sync_copy(hn_hbm.at[pl.ds(tok0, ntok)], x_v)
            _sc_row_pipeline(tab_hbm, idx_v, buf, sems, compute)
            pltpu.sync_copy(a_v, out_hbm.at[pl.ds(tok0, ntok)])

    return run(table, eid, hn)


def _expert_mix(table, eid, w):
    t = eid.shape[0]
    words = table.shape[1]
    d = 2 * words
    per_worker = t // SC_WORKERS
    nslot, nrow = SC_MIX_RING
    ntok = SC_MIX_BLOCK
    nsub = PEER_PICKS // nrow
    ln = SC_LANES
    nvec = SC_WORDS // ln

    @functools.partial(
        pl.kernel, mesh=_sc_mesh(),
        out_type=jax.ShapeDtypeStruct((t, d), F32),
        scratch_types=[pltpu.VMEM((ntok, PEER_PICKS), jnp.int32),
                       pltpu.VMEM((ntok, PEER_PICKS), jnp.int32),
                       pltpu.VMEM((ntok, d), F32),
                       pltpu.VMEM((nslot, nrow, words), jnp.int32),
                       pltpu.SemaphoreType.DMA((nslot,))],
        compiler_params=pltpu.CompilerParams(needs_layout_passes=False),
        name="peer_expert_mix")
    def run(tab_hbm, eid_hbm, w_hbm, out_hbm, idx_v, w_v, y_v, buf, sems):
        zero = jnp.zeros((ln,), F32)

        def compute(g, slot):
            tok = g // nsub
            sub = g % nsub
            tokv = jnp.full((ln,), tok, jnp.int32)

            def weight(r):
                pick = jnp.full((ln,), sub * nrow + r, jnp.int32)
                return plsc.bitcast(plsc.load_gather(w_v, [tokv, pick]), BF16)

            for cc in range(words // SC_WORDS):
                def body(rg, accs):
                    r0 = SC_MIX_GROUP * rg
                    ws = [weight(r0 + j) for j in range(SC_MIX_GROUP)]
                    out = list(accs)
                    for k in range(nvec):
                        cols = pl.ds(cc * SC_WORDS + k * ln, ln)
                        prod = ws[0] * plsc.bitcast(buf[slot, r0, cols], BF16)
                        for j in range(1, SC_MIX_GROUP):
                            prod = prod + ws[j] * plsc.bitcast(buf[slot, r0 + j, cols], BF16)
                        hi, lo = _sc_split(plsc.bitcast(prod, jnp.int32))
                        out[k] = accs[k] + hi
                        out[nvec + k] = accs[nvec + k] + lo
                    return tuple(out)

                lo_cols = [pl.ds(cc * SC_WORDS + k * ln, ln) for k in range(nvec)]
                hi_cols = [pl.ds(words + cc * SC_WORDS + k * ln, ln) for k in range(nvec)]
                init = tuple(y_v[tok, c] for c in lo_cols + hi_cols)
                accs = lax.fori_loop(0, nrow // SC_MIX_GROUP, body, init)
                for c, acc in zip(lo_cols + hi_cols, accs):
                    y_v[tok, c] = acc

        @pl.loop(0, per_worker // ntok)
        def _(blk):
            tok0 = _sc_worker() * per_worker + blk * ntok
            pltpu.sync_copy(eid_hbm.at[pl.ds(tok0, ntok)], idx_v)
            pltpu.sync_copy(w_hbm.at[pl.ds(tok0, ntok)], w_v)

            @pl.loop(0, ntok)
            def _(tok):
                @pl.loop(0, d // ln)
                def _(c):
                    y_v[tok, pl.ds(c * ln, ln)] = zero

            _sc_row_pipeline(tab_hbm, idx_v, buf, sems, compute)
            pltpu.sync_copy(y_v, out_hbm.at[pl.ds(tok0, ntok)])

    return run(table, eid, w)


def _pick_weights_body(a_ref, g_ref, o_ref):
    o_ref[...] = _pack_twice(g_ref[...] * _gelu(a_ref[...]))


def _pick_weights(a, gate, tm):
    t = a.shape[0]
    spec = pl.BlockSpec((tm, PEER_PICKS), lambda i: (i, 0))
    return pl.pallas_call(
        _pick_weights_body,
        grid=(t // tm,),
        in_specs=[spec, spec],
        out_specs=spec,
        out_shape=jax.ShapeDtypeStruct(a.shape, jnp.int32),
        compiler_params=_params("parallel"),
        name="peer_pick_weights",
    )(a, gate)


def _ple_body(h_ref, y_ref, g_ref, wg_ref, p_ref, wp_ref, o_ref):
    h = h_ref[...] + y_ref[...]
    hn = _rms(h, g_ref[...]).astype(BF16)
    gate = _sigmoid(jnp.dot(hn, wg_ref[...], preferred_element_type=F32))
    emb = jnp.dot(p_ref[...].astype(BF16), wp_ref[...], preferred_element_type=F32)
    o_ref[...] = h + gate * emb


def _ple(h, y, gain, wg_bf, p, wp_bf, tm):
    t, d = h.shape
    return pl.pallas_call(
        _ple_body,
        grid=(t // tm,),
        in_specs=[pl.BlockSpec((tm, d), lambda i: (i, 0)),
                  pl.BlockSpec((tm, d), lambda i: (i, 0)),
                  pl.BlockSpec((1, d), lambda i: (0, 0)),
                  pl.BlockSpec(wg_bf.shape, lambda i: (0, 0)),
                  pl.BlockSpec((tm, p.shape[1]), lambda i: (i, 0)),
                  pl.BlockSpec(wp_bf.shape, lambda i: (0, 0))],
        out_specs=pl.BlockSpec((tm, d), lambda i: (i, 0)),
        out_shape=jax.ShapeDtypeStruct((t, d), F32),
        compiler_params=_params("parallel"),
        name="layer_embedding",
    )(h, y, gain.reshape(1, d), wg_bf, p, wp_bf)


def _rope_tables(seq, dim):
    inv = 1.0 / (ROPE_THETA ** (jnp.arange(0, dim, 2, dtype=F32) / dim))
    ang = jnp.arange(seq, dtype=F32)[:, None] * inv[None, :]
    return jnp.cos(ang), jnp.sin(ang)


def _pack_table(tab):
    n = tab.shape[1] // 2
    b = lax.bitcast_convert_type(tab.astype(BF16), jnp.uint16).astype(jnp.uint32)
    return lax.bitcast_convert_type((b[:, :n] << 16) | b[:, n:], jnp.int32)


def _permute_in_columns(w_in):
    return jnp.concatenate([w_in[:, 3840:], w_in[:, :3840]], axis=1)


def kernel(x, p, norm_mix, w_in, ret_decay, ret_norm, sgu_ln_g, sgu_ln_b, sgu_w, sgu_b, att_q_norm, att_k_norm, att_sink, w_proj_ret, w_proj_sgu, w_proj_att, w_out, norm_ffn, peer_wq, peer_keys, peer_u, peer_v, norm_ple, ple_gate, ple_proj):
    batch, seq, d = x.shape
    depth = w_in.shape[0]
    groups = PIPELINE_GROUPS if batch % PIPELINE_GROUPS == 0 else 1
    gb = batch // groups
    t = gb * seq
    ts = min(512, seq)
    tm = min(512, t)
    cos_r, sin_r = _rope_tables(seq, RET_DK)
    cos_r2 = jnp.concatenate([cos_r, cos_r], axis=1)
    sin_r2 = jnp.concatenate([-sin_r, sin_r], axis=1)
    cos_a, sin_a = _rope_tables(seq, ATT_DH)
    hs = [x[g * gb:(g + 1) * gb].reshape(t, d) for g in range(groups)]
    for i in range(depth):
        w_in_bf = _permute_in_columns(w_in[i]).astype(BF16)
        w_r, w_s, w_a = w_proj_ret[i].astype(BF16), w_proj_sgu[i].astype(BF16), w_proj_att[i].astype(BF16)
        w_o, w_q, keys = w_out[i].astype(BF16), peer_wq[i].astype(BF16), peer_keys[i].astype(BF16)
        w_g, w_p = ple_gate[i].astype(BF16), ple_proj[i].astype(BF16)
        tab_u, tab_v = _pack_table(peer_u[i]), _pack_table(peer_v[i])
        for g in range(groups):
            h = hs[g]
            z = _in_projection(h, norm_mix[i], w_in_bf, tm)
            y_r = _retention(z, ret_decay[i], ret_norm[i], cos_r2, sin_r2, gb, seq, ts)
            y_s = _sgu(z, sgu_ln_g[i], sgu_ln_b[i], sgu_w[i], sgu_b[i], ts)
            y_a = _attention(z, att_q_norm[i], att_k_norm[i], att_sink[i], cos_a, sin_a, gb, seq, ts)
            h = _merge(h, z, y_r, y_s, y_a, w_r, w_s, w_a, w_o, tm)
            hn, eid, gate = _route(h, norm_ffn[i], w_q, keys, min(256, t))
            a = _expert_scores(tab_u, eid, hn)
            w = _pick_weights(a, gate, tm)
            y = _expert_mix(tab_v, eid, w)
            p_g = p[i, g * gb:(g + 1) * gb].reshape(t, -1)
            hs[g] = _ple(h, y, norm_ple[i], w_g, p_g, w_p, tm)
    return jnp.concatenate(hs, axis=0).reshape(batch, seq, d)
```

```python
import functools
import math

import jax
import jax.numpy as jnp
from jax import lax
from jax.experimental import pallas as pl
from jax.experimental.pallas import tpu as pltpu
from jax.experimental.pallas import tpu_sc as plsc

F32 = jnp.float32
BF16 = jnp.bfloat16

D_MODEL = 1024
PLE_DIM = 256
CHUNK = 128
EPS = 1e-6
ROPE_THETA = 10000.0
RET_HEADS = 4
RET_DK = 128
SGU_GROUPS = 4
SGU_WIDTH = 512
ATT_HEADS = 8
ATT_KV_HEADS = 2
ATT_DH = 64
ATT_GROUP = ATT_HEADS // ATT_KV_HEADS
NEG_INF = -1e30
PEER_HEADS = 8
PEER_KEYS = 128
PEER_QDIM = 256
PEER_TOPK = 16
PEER_PICKS = PEER_HEADS * PEER_TOPK

Z_GATE_R, Z_GATE_S, Z_GATE_A = 0, 1024, 2048
Z_QR, Z_KR, Z_VR, Z_GR = 3072, 3584, 4096, 4608
Z_US, Z_VS = 5120, 5632
Z_QA, Z_KA, Z_VA = 6144, 6656, 6784
D_IN = 6912

LANE = 128
SC_CORES = 2
SC_SUBCORES = 16
SC_LANES = 16
SC_WORKERS = SC_CORES * SC_SUBCORES
SC_SCORE_BLOCK = 64
SC_MIX_BLOCK = 32
SC_SCORE_RING = (8, 16)
SC_MIX_RING = (2, 64)
SC_MIX_GROUP = 4
SC_WORDS = 128
HI_MASK = -65536
VMEM_LIMIT = 56 * 1024 * 1024
PIPELINE_GROUPS = 8


def _params(*sem):
    return pltpu.CompilerParams(dimension_semantics=sem, vmem_limit_bytes=VMEM_LIMIT)


def _gelu(x):
    return 0.5 * x * (1.0 + lax.erf(x * (1.0 / math.sqrt(2.0))))


def _sigmoid(x):
    return 1.0 / (1.0 + jnp.exp(-x))


def _rms(x, g):
    return x * lax.rsqrt(jnp.mean(x * x, axis=-1, keepdims=True) + EPS) * g


def _bf16_hi_bits(x):
    b = lax.bitcast_convert_type(x, jnp.int32)
    return (b + 0x7FFF + ((b >> 16) & 1)) & HI_MASK


def _pack_halves(x):
    n = x.shape[1] // 2
    return _bf16_hi_bits(x[:, :n]) | lax.shift_right_logical(_bf16_hi_bits(x[:, n:]), 16)


def _pack_twice(x):
    b = _bf16_hi_bits(x)
    return b | lax.shift_right_logical(b, 16)


INPROJ_COLS = 1152


def _inproj_body(x_ref, g_ref, w_ref, o_ref):
    xn = _rms(x_ref[...], g_ref[...]).astype(BF16)
    for j in range(w_ref.shape[1] // INPROJ_COLS):
        cols = slice(j * INPROJ_COLS, (j + 1) * INPROJ_COLS)
        o_ref[:, cols] = jnp.dot(xn, w_ref[:, cols], preferred_element_type=F32).astype(o_ref.dtype)


def _in_projection(h, gain, w_bf, tm):
    t, d = h.shape
    n = w_bf.shape[1]
    return pl.pallas_call(
        _inproj_body,
        grid=(t // tm,),
        in_specs=[pl.BlockSpec((tm, d), lambda i: (i, 0)),
                  pl.BlockSpec((1, d), lambda i: (0, 0)),
                  pl.BlockSpec((d, n), lambda i: (0, 0))],
        out_specs=pl.BlockSpec((tm, n), lambda i: (i, 0)),
        out_shape=jax.ShapeDtypeStruct((t, n), BF16),
        compiler_params=_params("parallel"),
        name="in_projection",
    )(h, gain.reshape(1, d), w_bf)


def _rope128(x, cos, sin_signed):
    return x * cos + pltpu.roll(x, 64, 1) * sin_signed


def _ret_bwd_body(q_ref, k_ref, v_ref, cos_ref, sin_ref, qw_ref, kw_ref, cd_ref, o_ref, st_ref, *, nchunk):
    @pl.when(pl.program_id(2) == 0)
    def _():
        st_ref[...] = jnp.zeros_like(st_ref)

    qw = qw_ref[0]
    kw = kw_ref[0]
    cd = cd_ref[0, 0:1, :]
    for c in reversed(range(nchunk)):
        rows = pl.ds(c * CHUNK, CHUNK)
        cos = cos_ref[rows, :]
        sin = sin_ref[rows, :]
        q = _rope128(q_ref[rows, :].astype(F32), cos, sin)
        k = _rope128(k_ref[rows, :].astype(F32), cos, sin) * (RET_DK ** -0.5)
        v = v_ref[rows, :]
        st = st_ref[...]
        o_ref[rows, :] = jnp.dot((q * qw).astype(BF16), st.astype(BF16), preferred_element_type=F32)
        kv = jnp.dot((k * kw).T.astype(BF16), v, preferred_element_type=F32)
        st_ref[...] = st * cd + kv


def _ret_fwd_body(q_ref, k_ref, v_ref, g_ref, yb_ref, cos_ref, sin_ref, dm_ref, qw_ref, kw_ref, cd_ref,
                  gn_ref, o_ref, st_ref, *, nchunk):
    @pl.when(pl.program_id(2) == 0)
    def _():
        st_ref[...] = jnp.zeros_like(st_ref)

    qw = qw_ref[0]
    kw = kw_ref[0]
    cd = cd_ref[0, 0:1, :]
    dm = dm_ref[0]
    gn = gn_ref[0, 0:1, :]
    for c in range(nchunk):
        rows = pl.ds(c * CHUNK, CHUNK)
        cos = cos_ref[rows, :]
        sin = sin_ref[rows, :]
        q = _rope128(q_ref[rows, :].astype(F32), cos, sin)
        k = _rope128(k_ref[rows, :].astype(F32), cos, sin) * (RET_DK ** -0.5)
        v = v_ref[rows, :]
        st = st_ref[...]
        s = lax.dot_general(q.astype(BF16), k.astype(BF16), (((1,), (1,)), ((), ())),
                            preferred_element_type=F32) * dm
        y = jnp.dot(s.astype(BF16), v, preferred_element_type=F32)
        y += jnp.dot((q * qw).astype(BF16), st.astype(BF16), preferred_element_type=F32)
        y += yb_ref[rows, :]
        kv = jnp.dot((k * kw).T.astype(BF16), v, preferred_element_type=F32)
        st_ref[...] = st * cd + kv
        y = y * lax.rsqrt(jnp.mean(y * y, axis=-1, keepdims=True) + EPS) * gn
        g = g_ref[rows, :].astype(F32)
        o_ref[rows, :] = (g * _sigmoid(g) * y).astype(o_ref.dtype)


def _retention(z, ret_decay, ret_norm, cos, sin, batch, seq, ts):
    t = z.shape[0]
    nchunk = ts // CHUNK
    nstep = seq // ts
    hd = RET_HEADS
    log_g = jax.nn.log_sigmoid(ret_decay.astype(F32))
    idx = jnp.arange(CHUNK, dtype=F32)
    diff = idx[:, None] - idx[None, :]
    lf = log_g[0][:, None, None]
    lb = log_g[1][:, None, None]
    dmat = jnp.where(diff[None] >= 0, jnp.exp(lf * jnp.maximum(diff, 0.0)[None]),
                     jnp.exp(lb * jnp.maximum(-diff, 0.0)[None]))
    bc = lambda a: jnp.broadcast_to(a[:, :, None], (hd, a.shape[1], LANE))
    qw_f = bc(jnp.exp(log_g[0][:, None] * (idx + 1.0)[None, :]))
    kw_f = bc(jnp.exp(log_g[0][:, None] * (CHUNK - 1 - idx)[None, :]))
    qw_b = bc(jnp.exp(log_g[1][:, None] * (CHUNK - idx)[None, :]))
    kw_b = bc(jnp.exp(log_g[1][:, None] * idx[None, :]))
    cd_f = jnp.broadcast_to(jnp.exp(log_g[0] * CHUNK)[:, None, None], (hd, 8, LANE))
    cd_b = jnp.broadcast_to(jnp.exp(log_g[1] * CHUNK)[:, None, None], (hd, 8, LANE))
    gn = jnp.broadcast_to(ret_norm.astype(F32).reshape(hd, 1, LANE), (hd, 8, LANE))

    def zspec(col0, rev):
        cb = col0 // LANE
        if rev:
            return pl.BlockSpec((ts, LANE), lambda b, h, s: (b * nstep + nstep - 1 - s, cb + h))
        return pl.BlockSpec((ts, LANE), lambda b, h, s: (b * nstep + s, cb + h))

    def tspec(rev):
        if rev:
            return pl.BlockSpec((ts, LANE), lambda b, h, s: (nstep - 1 - s, 0))
        return pl.BlockSpec((ts, LANE), lambda b, h, s: (s, 0))

    hspec = lambda r: pl.BlockSpec((1, r, LANE), lambda b, h, s: (h, 0, 0))

    yb = pl.pallas_call(
        functools.partial(_ret_bwd_body, nchunk=nchunk),
        grid=(batch, hd, nstep),
        in_specs=[zspec(Z_QR, True), zspec(Z_KR, True), zspec(Z_VR, True), tspec(True), tspec(True),
                  hspec(CHUNK), hspec(CHUNK), hspec(8)],
        out_specs=pl.BlockSpec((ts, LANE), lambda b, h, s: (b * nstep + nstep - 1 - s, h)),
        out_shape=jax.ShapeDtypeStruct((t, hd * LANE), F32),
        scratch_shapes=[pltpu.VMEM((RET_DK, LANE), F32)],
        compiler_params=_params("parallel", "parallel", "arbitrary"),
        name="retention_bwd",
    )(z, z, z, cos, sin, qw_b, kw_b, cd_b)

    return pl.pallas_call(
        functools.partial(_ret_fwd_body, nchunk=nchunk),
        grid=(batch, hd, nstep),
        in_specs=[zspec(Z_QR, False), zspec(Z_KR, False), zspec(Z_VR, False), zspec(Z_GR, False),
                  pl.BlockSpec((ts, LANE), lambda b, h, s: (b * nstep + s, h)),
                  tspec(False), tspec(False), hspec(CHUNK), hspec(CHUNK), hspec(CHUNK), hspec(8), hspec(8)],
        out_specs=pl.BlockSpec((ts, LANE), lambda b, h, s: (b * nstep + s, h)),
        out_shape=jax.ShapeDtypeStruct((t, hd * LANE), BF16),
        scratch_shapes=[pltpu.VMEM((RET_DK, LANE), F32)],
        compiler_params=_params("parallel", "parallel", "arbitrary"),
        name="retention_fwd",
    )(z, z, z, z, yb, cos, sin, dmat, qw_f, kw_f, cd_f, gn)


def _sgu_body(u_ref, v_ref, lg_ref, lb_ref, w_ref, b_ref, o_ref, *, nchunk):
    lg = lg_ref[...]
    lb = lb_ref[...]
    for c in range(nchunk):
        rows = pl.ds(c * CHUNK, CHUNK)
        vf = _gelu(v_ref[rows, :].astype(F32))
        mu = jnp.mean(vf, axis=-1, keepdims=True)
        vc = vf - mu
        var = jnp.mean(vc * vc, axis=-1, keepdims=True)
        vn = (vc * lax.rsqrt(var + EPS) * lg + lb).astype(BF16)
        for g in range(SGU_GROUPS):
            cols = slice(g * LANE, (g + 1) * LANE)
            mixed = jnp.dot(w_ref[g], vn[:, cols], preferred_element_type=F32) + b_ref[g]
            uf = _gelu(u_ref[rows, cols].astype(F32))
            o_ref[rows, cols] = (uf * mixed).astype(o_ref.dtype)


def _sgu(z, ln_g, ln_b, w_s, b_s, ts):
    t = z.shape[0]
    bias = jnp.broadcast_to(b_s.astype(F32)[:, :, None], (SGU_GROUPS, CHUNK, LANE))
    return pl.pallas_call(
        functools.partial(_sgu_body, nchunk=ts // CHUNK),
        grid=(t // ts,),
        in_specs=[pl.BlockSpec((ts, SGU_WIDTH), lambda i: (i, Z_US // SGU_WIDTH)),
                  pl.BlockSpec((ts, SGU_WIDTH), lambda i: (i, Z_VS // SGU_WIDTH)),
                  pl.BlockSpec((1, SGU_WIDTH), lambda i: (0, 0)),
                  pl.BlockSpec((1, SGU_WIDTH), lambda i: (0, 0)),
                  pl.BlockSpec((SGU_GROUPS, CHUNK, CHUNK), lambda i: (0, 0, 0)),
                  pl.BlockSpec((SGU_GROUPS, CHUNK, LANE), lambda i: (0, 0, 0))],
        out_specs=pl.BlockSpec((ts, SGU_WIDTH), lambda i: (i, 0)),
        out_shape=jax.ShapeDtypeStruct((t, SGU_WIDTH), BF16),
        compiler_params=_params("parallel"),
        name="spatial_gating",
    )(z, z, ln_g.reshape(1, -1).astype(F32), ln_b.reshape(1, -1).astype(F32), w_s.astype(BF16), bias)


def _pair_norm_rope(x, gain, cos, sin_up, sin_dn, low):
    sq = x * x
    lo = jnp.sum(jnp.where(low, sq, 0.0), axis=-1, keepdims=True)
    hi = jnp.sum(sq, axis=-1, keepdims=True) - lo
    ms = jnp.where(low, lo, hi) * (1.0 / ATT_DH)
    xn = x * lax.rsqrt(ms + EPS) * gain
    return xn * cos + pltpu.roll(xn, LANE - 32, 1) * sin_up + pltpu.roll(xn, 32, 1) * sin_dn


def _attn_body(sink_ref, q_ref, kp_ref, k_ref, kn_ref, vp_ref, v_ref, vn_ref,
               cq_ref, suq_ref, sdq_ref, ckp_ref, sukp_ref, sdkp_ref, ckn_ref, sukn_ref, sdkn_ref,
               qg_ref, kg_ref, o_ref, *, nchunk, nstep):
    s_id = pl.program_id(1)
    ts = nchunk * CHUNK
    lane = lax.broadcasted_iota(jnp.int32, (1, LANE), 1)
    low = lane < ATT_DH
    kg = kg_ref[...]
    qg = qg_ref[...]
    k_ext = jnp.concatenate([
        _pair_norm_rope(kp_ref[...].astype(F32), kg, ckp_ref[...], sukp_ref[...], sdkp_ref[...], low),
        _pair_norm_rope(k_ref[...].astype(F32), kg, cq_ref[...], suq_ref[...], sdq_ref[...], low),
        _pair_norm_rope(kn_ref[...].astype(F32), kg, ckn_ref[...], sukn_ref[...], sdkn_ref[...], low)], axis=0)
    v_ext = jnp.concatenate([vp_ref[...], v_ref[...], vn_ref[...]], axis=0)
    k_lo = jnp.where(low, k_ext, 0.0).astype(BF16)
    k_hi = jnp.where(low, 0.0, k_ext).astype(BF16)
    qi = lax.broadcasted_iota(jnp.int32, (CHUNK, 3 * CHUNK), 0)
    kj = lax.broadcasted_iota(jnp.int32, (CHUNK, 3 * CHUNK), 1)
    band = jnp.abs(qi + CHUNK - kj) <= CHUNK
    for c in range(nchunk):
        rows = pl.ds(c * CHUNK, CHUNK)
        first = jnp.logical_and(s_id == 0, c == 0)
        last = jnp.logical_and(s_id == nstep - 1, c == nchunk - 1)
        valid = band
        if c == 0:
            valid = jnp.logical_and(valid, jnp.logical_or(kj >= CHUNK, jnp.logical_not(first)))
        if c == nchunk - 1:
            valid = jnp.logical_and(valid, jnp.logical_or(kj < 2 * CHUNK, jnp.logical_not(last)))
        kc_lo = k_lo[c * CHUNK:(c + 3) * CHUNK]
        kc_hi = k_hi[c * CHUNK:(c + 3) * CHUNK]
        vc = v_ext[c * CHUNK:(c + 3) * CHUNK]
        cos = cq_ref[rows, :]
        su = suq_ref[rows, :]
        sd = sdq_ref[rows, :]
        for pair in range(ATT_HEADS // 2):
            cols = slice(pair * LANE, (pair + 1) * LANE)
            qp = _pair_norm_rope(q_ref[rows, cols].astype(F32), qg, cos, su, sd, low) * (ATT_DH ** -0.5)
            kv_head = (2 * pair) // ATT_GROUP
            outs = []
            for half in range(2):
                head = 2 * pair + half
                qh = qp if half == kv_head else pltpu.roll(qp, ATT_DH, 1)
                if kv_head == 0:
                    qh = jnp.where(low, qh, 0.0)
                    kc = kc_lo
                else:
                    qh = jnp.where(low, 0.0, qh)
                    kc = kc_hi
                s = lax.dot_general(qh.astype(BF16), kc, (((1,), (1,)), ((), ())), preferred_element_type=F32)
                s = jnp.where(valid, s, NEG_INF)
                sk = sink_ref[head]
                m = jnp.maximum(jnp.max(s, axis=-1, keepdims=True), sk)
                e = jnp.exp(s - m)
                den = jnp.sum(e, axis=-1, keepdims=True) + jnp.exp(sk - m)
                o = jnp.dot(e.astype(BF16), vc, preferred_element_type=F32) / den
                outs.append(o if half == kv_head else pltpu.roll(o, ATT_DH, 1))
            o_ref[rows, cols] = jnp.where(low, outs[0], outs[1]).astype(o_ref.dtype)


def _attention(z, q_gain, k_gain, sink, cos, sin, batch, seq, ts):
    t = z.shape[0]
    nchunk = ts // CHUNK
    nstep = seq // ts
    nblk = seq // CHUNK
    cos2 = jnp.tile(jnp.concatenate([cos, cos], axis=1), (1, 2))
    zero = jnp.zeros_like(sin)
    sin_up = jnp.tile(jnp.concatenate([-sin, zero], axis=1), (1, 2))
    sin_dn = jnp.tile(jnp.concatenate([zero, sin], axis=1), (1, 2))
    qg = jnp.tile(q_gain.astype(F32), 2).reshape(1, LANE)
    kg = jnp.tile(k_gain.astype(F32), 2).reshape(1, LANE)

    kcb, vcb = Z_KA // LANE, Z_VA // LANE
    prev_blk = lambda s: jnp.maximum(s * nchunk - 1, 0)
    next_blk = lambda s: jnp.minimum((s + 1) * nchunk, nblk - 1)
    main = lambda cb: pl.BlockSpec((ts, LANE), lambda b, s: (b * nstep + s, cb))
    prev = lambda cb: pl.BlockSpec((CHUNK, LANE), lambda b, s: (b * nblk + prev_blk(s), cb))
    nxt = lambda cb: pl.BlockSpec((CHUNK, LANE), lambda b, s: (b * nblk + next_blk(s), cb))
    tmain = pl.BlockSpec((ts, LANE), lambda b, s: (s, 0))
    tprev = pl.BlockSpec((CHUNK, LANE), lambda b, s: (prev_blk(s), 0))
    tnext = pl.BlockSpec((CHUNK, LANE), lambda b, s: (next_blk(s), 0))
    one = pl.BlockSpec((1, LANE), lambda b, s: (0, 0))
    return pl.pallas_call(
        functools.partial(_attn_body, nchunk=nchunk, nstep=nstep),
        grid=(batch, nstep),
        in_specs=[pl.BlockSpec(memory_space=pltpu.SMEM),
                  pl.BlockSpec((ts, ATT_HEADS * ATT_DH), lambda b, s: (b * nstep + s, Z_QA // 512)),
                  prev(kcb), main(kcb), nxt(kcb), prev(vcb), main(vcb), nxt(vcb),
                  tmain, tmain, tmain, tprev, tprev, tprev, tnext, tnext, tnext, one, one],
        out_specs=pl.BlockSpec((ts, ATT_HEADS * ATT_DH), lambda b, s: (b * nstep + s, 0)),
        out_shape=jax.ShapeDtypeStruct((t, ATT_HEADS * ATT_DH), BF16),
        compiler_params=_params("parallel", "arbitrary"),
        name="window_attention",
    )(sink.astype(F32), z, z, z, z, z, z, z,
      cos2, sin_up, sin_dn, cos2, sin_up, sin_dn, cos2, sin_up, sin_dn, qg, kg)


def _merge_body(h_ref, gr_ref, gs_ref, ga_ref, yr_ref, ys_ref, ya_ref, wr_ref, ws_ref, wa_ref, wo_ref, o_ref):
    m = _sigmoid(gr_ref[...].astype(F32)) * jnp.dot(yr_ref[...], wr_ref[...], preferred_element_type=F32)
    m += _sigmoid(gs_ref[...].astype(F32)) * jnp.dot(ys_ref[...], ws_ref[...], preferred_element_type=F32)
    m += _sigmoid(ga_ref[...].astype(F32)) * jnp.dot(ya_ref[...], wa_ref[...], preferred_element_type=F32)
    o_ref[...] = h_ref[...] + jnp.dot(m.astype(BF16), wo_ref[...], preferred_element_type=F32)


def _merge(h, z, y_r, y_s, y_a, w_r, w_s, w_a, w_o, tm):
    t, d = h.shape
    row = lambda w: pl.BlockSpec((tm, w), lambda i: (i, 0))
    gate = lambda col0: pl.BlockSpec((tm, d), lambda i: (i, col0 // d))
    full = lambda a: pl.BlockSpec(a.shape, lambda i: (0, 0))
    return pl.pallas_call(
        _merge_body,
        grid=(t // tm,),
        in_specs=[row(d), gate(Z_GATE_R), gate(Z_GATE_S), gate(Z_GATE_A), row(512), row(512), row(512),
                  full(w_r), full(w_s), full(w_a), full(w_o)],
        out_specs=row(d),
        out_shape=jax.ShapeDtypeStruct((t, d), F32),
        compiler_params=_params("parallel"),
        name="branch_merge",
    )(h, z, z, z, y_r, y_s, y_a, w_r, w_s, w_a, w_o)


ID_NONE = 1 << 20


def _top16(vals, ids):
    out_v, out_i = [], []
    for _ in range(PEER_TOPK):
        m = jnp.max(vals, axis=0, keepdims=True)
        idx = jnp.min(jnp.where(vals == m, ids, ID_NONE), axis=0, keepdims=True)
        out_v.append(m)
        out_i.append(idx)
        vals = jnp.where(ids == idx, -jnp.inf, vals)
    return out_v, out_i


def _pair_blocks(first, second):
    rows1, stack1 = first
    rows2, stack2 = second
    blocks = [(stack1, rows2[0])]
    blocks += [(stack1[0:8], rows2[b]) for b in range(1, 8)]
    blocks += [(rows1[0], stack2[8:16])]
    return blocks


def _pair_ids(tm):
    a8 = lax.broadcasted_iota(jnp.int32, (8, tm), 0)
    a16 = lax.broadcasted_iota(jnp.int32, (PEER_TOPK, tm), 0)
    blocks = [a16 * PEER_TOPK]
    blocks += [jnp.where(a8 < PEER_TOPK // (b + 1), a8 * PEER_TOPK + b, ID_NONE) for b in range(1, 8)]
    blocks += [a8 + 8]
    return jnp.concatenate(blocks, axis=0)


def _route_body(h_ref, g_ref, wq_ref, keys_ref, hn_ref, eid_ref, gate_ref):
    hn = _rms(h_ref[...], g_ref[...])
    hn_ref[...] = _pack_halves(hn)
    q = jnp.dot(hn.astype(BF16), wq_ref[...], preferred_element_type=F32).astype(BF16)
    tm = q.shape[0]
    rows = lax.broadcasted_iota(jnp.int32, (PEER_KEYS, tm), 0)
    pair_ids = _pair_ids(tm)
    pair_ok = pair_ids != ID_NONE
    half = PEER_QDIM // 2
    eids, gates = [], []
    for hd in range(PEER_HEADS):
        sub = []
        for p in range(2):
            qs = q[:, (2 * hd + p) * half:(2 * hd + p + 1) * half]
            s = lax.dot_general(keys_ref[hd, p], qs, (((1,), (1,)), ((), ())), preferred_element_type=F32)
            sub.append(_top16(s, rows))
        (s1, i1), (s2, i2) = sub
        scores = _pair_blocks((s1, jnp.concatenate(s1, axis=0)), (s2, jnp.concatenate(s2, axis=0)))
        experts = _pair_blocks((i1, jnp.concatenate(i1, axis=0)), (i2, jnp.concatenate(i2, axis=0)))
        cand_s = jnp.where(pair_ok, jnp.concatenate([a + b for a, b in scores], axis=0), -jnp.inf)
        cand_e = jnp.concatenate([a * PEER_KEYS + b for a, b in experts], axis=0)
        top_s, sel = _top16(cand_s, pair_ids)
        top_e = [jnp.sum(jnp.where(pair_ids == i, cand_e, 0), axis=0, keepdims=True) for i in sel]
        ts_ = jnp.concatenate(top_s, axis=0)
        e = jnp.exp(ts_ - top_s[0])
        gates.append(e / jnp.sum(e, axis=0, keepdims=True))
        eids.append(jnp.concatenate(top_e, axis=0))
    eid_ref[...] = jnp.concatenate(eids, axis=0).T
    gate_ref[...] = jnp.concatenate(gates, axis=0).T


def _route(h, gain, wq_bf, keys_bf, tm):
    t, d = h.shape
    return pl.pallas_call(
        _route_body,
        grid=(t // tm,),
        in_specs=[pl.BlockSpec((tm, d), lambda i: (i, 0)),
                  pl.BlockSpec((1, d), lambda i: (0, 0)),
                  pl.BlockSpec(wq_bf.shape, lambda i: (0, 0)),
                  pl.BlockSpec(keys_bf.shape, lambda i: (0, 0, 0, 0))],
        out_specs=[pl.BlockSpec((tm, d // 2), lambda i: (i, 0)),
                   pl.BlockSpec((tm, PEER_PICKS), lambda i: (i, 0)),
                   pl.BlockSpec((tm, PEER_PICKS), lambda i: (i, 0))],
        out_shape=[jax.ShapeDtypeStruct((t, d // 2), jnp.int32),
                   jax.ShapeDtypeStruct((t, PEER_PICKS), jnp.int32),
                   jax.ShapeDtypeStruct((t, PEER_PICKS), F32)],
        compiler_params=_params("parallel"),
        name="peer_route",
    )(h, gain.reshape(1, d), wq_bf, keys_bf)


def _sc_mesh():
    return plsc.VectorSubcoreMesh(core_axis_name="core", subcore_axis_name="subcore")


def _sc_worker():
    return lax.axis_index("subcore") * SC_CORES + lax.axis_index("core")


def _sc_row_pipeline(tab_hbm, idx_v, buf, sems, compute):
    nslot, nrow = buf.shape[0], buf.shape[1]
    nsub = PEER_PICKS // nrow
    ng = idx_v.shape[0] * nsub

    def gather(g, slot):
        rows = idx_v.at[g // nsub, pl.ds((g % nsub) * nrow, nrow)]
        return pltpu.make_async_copy(tab_hbm.at[rows], buf.at[slot], sems.at[slot])

    for b in range(nslot - 1):
        gather(b, b).start()

    @pl.loop(0, ng, step=nslot)
    def _(g):
        for b in range(nslot):
            ahead = g + b + nslot - 1

            @pl.when(ahead < ng)
            def _():
                gather(ahead, (b + nslot - 1) % nslot).start()

            gather(g + b, b).wait()
            compute(g + b, b)


def _sc_split(words):
    return (lax.bitcast_convert_type(words & HI_MASK, F32), lax.bitcast_convert_type(words << 16, F32))


def _expert_scores(table, eid, hn):
    t, words = hn.shape
    per_worker = t // SC_WORKERS
    nslot, nrow = SC_SCORE_RING
    ntok = SC_SCORE_BLOCK
    nsub = PEER_PICKS // nrow
    ln = SC_LANES

    @functools.partial(
        pl.kernel, mesh=_sc_mesh(),
        out_type=jax.ShapeDtypeStruct((t, PEER_PICKS), F32),
        scratch_types=[pltpu.VMEM((ntok, PEER_PICKS), jnp.int32),
                       pltpu.VMEM((ntok, words), jnp.int32),
                       pltpu.VMEM((ntok, PEER_PICKS), F32),
                       pltpu.VMEM((nslot, nrow, words), jnp.int32),
                       pltpu.SemaphoreType.DMA((nslot,))],
        compiler_params=pltpu.CompilerParams(needs_layout_passes=False),
        name="peer_expert_scores")
    def run(tab_hbm, eid_hbm, hn_hbm, out_hbm, idx_v, x_v, a_v, buf, sems):
        lane = lax.iota(jnp.int32, ln)

        def compute(g, slot):
            tok = g // nsub
            sub = g % nsub
            for grp in range(nrow // ln):
                def body(c, accs):
                    c0 = pl.ds(2 * c * ln, ln)
                    c1 = pl.ds((2 * c + 1) * ln, ln)
                    x0 = plsc.bitcast(x_v[tok, c0], BF16)
                    x1 = plsc.bitcast(x_v[tok, c1], BF16)
                    out = []
                    for r in range(ln):
                        u0 = plsc.bitcast(buf[slot, grp * ln + r, c0], BF16)
                        u1 = plsc.bitcast(buf[slot, grp * ln + r, c1], BF16)
                        hi, lo = _sc_split(plsc.bitcast(u0 * x0 + u1 * x1, jnp.int32))
                        out.append(accs[r] + hi + lo)
                    return tuple(out)

                accs = lax.fori_loop(0, words // (2 * ln), body, tuple(jnp.zeros((ln,), F32) for _ in range(ln)))
                res = jnp.zeros((ln,), F32)
                for r in range(ln):
                    res = jnp.where(lane == r, jnp.sum(accs[r]), res)
                a_v[tok, pl.ds(sub * nrow + grp * ln, ln)] = res

        @pl.loop(0, per_worker // ntok)
        def _(blk):
            tok0 = _sc_worker() * per_worker + blk * ntok
            pltpu.sync_copy(eid_hbm.at[pl.ds(tok0, ntok)], idx_v)
            pltpu.sync_copy(hn_hbm.at[pl.ds(tok0, ntok)], x_v)
            _sc_row_pipeline(tab_hbm, idx_v, buf, sems, compute)
            pltpu.sync_copy(a_v, out_hbm.at[pl.ds(tok0, ntok)])

    return run(table, eid, hn)


def _expert_mix(table, eid, w):
    t = eid.shape[0]
    words = table.shape[1]
    d = 2 * words
    per_worker = t // SC_WORKERS
    nslot, nrow = SC_MIX_RING
    ntok = SC_MIX_BLOCK
    nsub = PEER_PICKS // nrow
    ln = SC_LANES
    nvec = SC_WORDS // ln

    @functools.partial(
        pl.kernel, mesh=_sc_mesh(),
        out_type=jax.ShapeDtypeStruct((t, d), F32),
        scratch_types=[pltpu.VMEM((ntok, PEER_PICKS), jnp.int32),
                       pltpu.VMEM((ntok, PEER_PICKS), jnp.int32),
                       pltpu.VMEM((ntok, d), F32),
                       pltpu.VMEM((nslot, nrow, words), jnp.int32),
                       pltpu.SemaphoreType.DMA((nslot,))],
        compiler_params=pltpu.CompilerParams(needs_layout_passes=False),
        name="peer_expert_mix")
    def run(tab_hbm, eid_hbm, w_hbm, out_hbm, idx_v, w_v, y_v, buf, sems):
        zero = jnp.zeros((ln,), F32)

        def compute(g, slot):
            tok = g // nsub
            sub = g % nsub
            tokv = jnp.full((ln,), tok, jnp.int32)

            def weight(r):
                pick = jnp.full((ln,), sub * nrow + r, jnp.int32)
                return plsc.bitcast(plsc.load_gather(w_v, [tokv, pick]), BF16)

            for cc in range(words // SC_WORDS):
                def body(rg, accs):
                    r0 = SC_MIX_GROUP * rg
                    ws = [weight(r0 + j) for j in range(SC_MIX_GROUP)]
                    out = list(accs)
                    for k in range(nvec):
                        cols = pl.ds(cc * SC_WORDS + k * ln, ln)
                        prod = ws[0] * plsc.bitcast(buf[slot, r0, cols], BF16)
                        for j in range(1, SC_MIX_GROUP):
                            prod = prod + ws[j] * plsc.bitcast(buf[slot, r0 + j, cols], BF16)
                        hi, lo = _sc_split(plsc.bitcast(prod, jnp.int32))
                        out[k] = accs[k] + hi
                        out[nvec + k] = accs[nvec + k] + lo
                    return tuple(out)

                lo_cols = [pl.ds(cc * SC_WORDS + k * ln, ln) for k in range(nvec)]
                hi_cols = [pl.ds(words + cc * SC_WORDS + k * ln, ln) for k in range(nvec)]
                init = tuple(y_v[tok, c] for c in lo_cols + hi_cols)
                accs = lax.fori_loop(0, nrow // SC_MIX_GROUP, body, init)
                for c, acc in zip(lo_cols + hi_cols, accs):
                    y_v[tok, c] = acc

        @pl.loop(0, per_worker // ntok)
        def _(blk):
            tok0 = _sc_worker() * per_worker + blk * ntok
            pltpu.sync_copy(eid_hbm.at[pl.ds(tok0, ntok)], idx_v)
            pltpu.sync_copy(w_hbm.at[pl.ds(tok0, ntok)], w_v)

            @pl.loop(0, ntok)
            def _(tok):
                @pl.loop(0, d // ln)
                def _(c):
                    y_v[tok, pl.ds(c * ln, ln)] = zero

            _sc_row_pipeline(tab_hbm, idx_v, buf, sems, compute)
            pltpu.sync_copy(y_v, out_hbm.at[pl.ds(tok0, ntok)])

    return run(table, eid, w)


def _pick_weights_body(a_ref, g_ref, o_ref):
    o_ref[...] = _pack_twice(g_ref[...] * _gelu(a_ref[...]))


def _pick_weights(a, gate, tm):
    t = a.shape[0]
    spec = pl.BlockSpec((tm, PEER_PICKS), lambda i: (i, 0))
    return pl.pallas_call(
        _pick_weights_body,
        grid=(t // tm,),
        in_specs=[spec, spec],
        out_specs=spec,
        out_shape=jax.ShapeDtypeStruct(a.shape, jnp.int32),
        compiler_params=_params("parallel"),
        name="peer_pick_weights",
    )(a, gate)


def _ple_body(h_ref, y_ref, g_ref, wg_ref, p_ref, wp_ref, o_ref):
    h = h_ref[...] + y_ref[...]
    hn = _rms(h, g_ref[...]).astype(BF16)
    gate = _sigmoid(jnp.dot(hn, wg_ref[...], preferred_element_type=F32))
    emb = jnp.dot(p_ref[...].astype(BF16), wp_ref[...], preferred_element_type=F32)
    o_ref[...] = h + gate * emb


def _ple(h, y, gain, wg_bf, p, wp_bf, tm):
    t, d = h.shape
    return pl.pallas_call(
        _ple_body,
        grid=(t // tm,),
        in_specs=[pl.BlockSpec((tm, d), lambda i: (i, 0)),
                  pl.BlockSpec((tm, d), lambda i: (i, 0)),
                  pl.BlockSpec((1, d), lambda i: (0, 0)),
                  pl.BlockSpec(wg_bf.shape, lambda i: (0, 0)),
                  pl.BlockSpec((tm, p.shape[1]), lambda i: (i, 0)),
                  pl.BlockSpec(wp_bf.shape, lambda i: (0, 0))],
        out_specs=pl.BlockSpec((tm, d), lambda i: (i, 0)),
        out_shape=jax.ShapeDtypeStruct((t, d), F32),
        compiler_params=_params("parallel"),
        name="layer_embedding",
    )(h, y, gain.reshape(1, d), wg_bf, p, wp_bf)


def _rope_tables(seq, dim):
    inv = 1.0 / (ROPE_THETA ** (jnp.arange(0, dim, 2, dtype=F32) / dim))
    ang = jnp.arange(seq, dtype=F32)[:, None] * inv[None, :]
    return jnp.cos(ang), jnp.sin(ang)


def _pack_table(tab):
    n = tab.shape[1] // 2
    b = lax.bitcast_convert_type(tab.astype(BF16), jnp.uint16).astype(jnp.uint32)
    return lax.bitcast_convert_type((b[:, :n] << 16) | b[:, n:], jnp.int32)


def _permute_in_columns(w_in):
    return jnp.concatenate([w_in[:, 3840:], w_in[:, :3840]], axis=1)


def kernel(x, p, norm_mix, w_in, ret_decay, ret_norm, sgu_ln_g, sgu_ln_b, sgu_w, sgu_b, att_q_norm, att_k_norm, att_sink, w_proj_ret, w_proj_sgu, w_proj_att, w_out, norm_ffn, peer_wq, peer_keys, peer_u, peer_v, norm_ple, ple_gate, ple_proj):
    batch, seq, d = x.shape
    depth = w_in.shape[0]
    groups = PIPELINE_GROUPS if batch % PIPELINE_GROUPS == 0 else 1
    gb = batch // groups
    t = gb * seq
    ts = min(512, seq)
    tm = min(512, t)
    cos_r, sin_r = _rope_tables(seq, RET_DK)
    cos_r2 = jnp.concatenate([cos_r, cos_r], axis=1)
    sin_r2 = jnp.concatenate([-sin_r, sin_r], axis=1)
    cos_a, sin_a = _rope_tables(seq, ATT_DH)
    hs = [x[g * gb:(g + 1) * gb].reshape(t, d) for g in range(groups)]
    for i in range(depth):
        w_in_bf = _permute_in_columns(w_in[i]).astype(BF16)
        w_r, w_s, w_a = w_proj_ret[i].astype(BF16), w_proj_sgu[i].astype(BF16), w_proj_att[i].astype(BF16)
        w_o, w_q, keys = w_out[i].astype(BF16), peer_wq[i].astype(BF16), peer_keys[i].astype(BF16)
        w_g, w_p = ple_gate[i].astype(BF16), ple_proj[i].astype(BF16)
        tab_u, tab_v = _pack_table(peer_u[i]), _pack_table(peer_v[i])
        for g in range(groups):
            h = hs[g]
            z = _in_projection(h, norm_mix[i], w_in_bf, tm)
            y_r = _retention(z, ret_decay[i], ret_norm[i], cos_r2, sin_r2, gb, seq, ts)
            y_s = _sgu(z, sgu_ln_g[i], sgu_ln_b[i], sgu_w[i], sgu_b[i], ts)
            y_a = _attention(z, att_q_norm[i], att_k_norm[i], att_sink[i], cos_a, sin_a, gb, seq, ts)
            h = _merge(h, z, y_r, y_s, y_a, w_r, w_s, w_a, w_o, tm)
            hn, eid, gate = _route(h, norm_ffn[i], w_q, keys, min(256, t))
            a = _expert_scores(tab_u, eid, hn)
            w = _pick_weights(a, gate, tm)
            y = _expert_mix(tab_v, eid, w)
            p_g = p[i, g * gb:(g + 1) * gb].reshape(t, -1)
            hs[g] = _ple(h, y, norm_ple[i], w_g, p_g, w_p, tm)
    return jnp.concatenate(hs, axis=0).reshape(batch, seq, d)
```

```python
import functools
import math

import jax
import jax.numpy as jnp
from jax import lax
from jax.experimental import pallas as pl
from jax.experimental.pallas import tpu as pltpu
from jax.experimental.pallas import tpu_sc as plsc

F32 = jnp.float32
BF16 = jnp.bfloat16

D_MODEL = 1024
PLE_DIM = 256
CHUNK = 128
EPS = 1e-6
ROPE_THETA = 10000.0
RET_HEADS = 4
RET_DK = 128
SGU_GROUPS = 4
SGU_WIDTH = 512
ATT_HEADS = 8
ATT_KV_HEADS = 2
ATT_DH = 64
ATT_GROUP = ATT_HEADS // ATT_KV_HEADS
NEG_INF = -1e30
PEER_HEADS = 8
PEER_KEYS = 128
PEER_QDIM = 256
PEER_TOPK = 16
PEER_PICKS = PEER_HEADS * PEER_TOPK

Z_GATE_R, Z_GATE_S, Z_GATE_A = 0, 1024, 2048
Z_QR, Z_KR, Z_VR, Z_GR = 3072, 3584, 4096, 4608
Z_US, Z_VS = 5120, 5632
Z_QA, Z_KA, Z_VA = 6144, 6656, 6784
D_IN = 6912

LANE = 128
SC_CORES = 2
SC_SUBCORES = 16
SC_LANES = 16
SC_WORKERS = SC_CORES * SC_SUBCORES
SC_SCORE_BLOCK = 64
SC_MIX_BLOCK = 32
SC_SCORE_RING = (8, 16)
SC_MIX_RING = (2, 64)
SC_MIX_GROUP = 4
SC_WORDS = 128
HI_MASK = -65536
VMEM_LIMIT = 56 * 1024 * 1024
PIPELINE_GROUPS = 8


def _params(*sem):
    return pltpu.CompilerParams(dimension_semantics=sem, vmem_limit_bytes=VMEM_LIMIT)


def _gelu(x):
    return 0.5 * x * (1.0 + lax.erf(x * (1.0 / math.sqrt(2.0))))


def _sigmoid(x):
    return 1.0 / (1.0 + jnp.exp(-x))


def _rms(x, g):
    return x * lax.rsqrt(jnp.mean(x * x, axis=-1, keepdims=True) + EPS) * g


def _bf16_hi_bits(x):
    b = lax.bitcast_convert_type(x, jnp.int32)
    return (b + 0x7FFF + ((b >> 16) & 1)) & HI_MASK


def _pack_halves(x):
    n = x.shape[1] // 2
    return _bf16_hi_bits(x[:, :n]) | lax.shift_right_logical(_bf16_hi_bits(x[:, n:]), 16)


def _pack_twice(x):
    b = _bf16_hi_bits(x)
    return b | lax.shift_right_logical(b, 16)


INPROJ_COLS = 1152


def _inproj_body(x_ref, g_ref, w_ref, o_ref):
    xn = _rms(x_ref[...], g_ref[...]).astype(BF16)
    for j in range(w_ref.shape[1] // INPROJ_COLS):
        cols = slice(j * INPROJ_COLS, (j + 1) * INPROJ_COLS)
        o_ref[:, cols] = jnp.dot(xn, w_ref[:, cols], preferred_element_type=F32).astype(o_ref.dtype)


def _in_projection(h, gain, w_bf, tm):
    t, d = h.shape
    n = w_bf.shape[1]
    return pl.pallas_call(
        _inproj_body,
        grid=(t // tm,),
        in_specs=[pl.BlockSpec((tm, d), lambda i: (i, 0)),
                  pl.BlockSpec((1, d), lambda i: (0, 0)),
                  pl.BlockSpec((d, n), lambda i: (0, 0))],
        out_specs=pl.BlockSpec((tm, n), lambda i: (i, 0)),
        out_shape=jax.ShapeDtypeStruct((t, n), BF16),
        compiler_params=_params("parallel"),
        name="in_projection",
    )(h, gain.reshape(1, d), w_bf)


def _rope128(x, cos, sin_signed):
    return x * cos + pltpu.roll(x, 64, 1) * sin_signed


def _ret_bwd_body(q_ref, k_ref, v_ref, cos_ref, sin_ref, qw_ref, kw_ref, cd_ref, o_ref, st_ref, *, nchunk):
    @pl.when(pl.program_id(2) == 0)
    def _():
        st_ref[...] = jnp.zeros_like(st_ref)

    qw = qw_ref[0]
    kw = kw_ref[0]
    cd = cd_ref[0, 0:1, :]
    for c in reversed(range(nchunk)):
        rows = pl.ds(c * CHUNK, CHUNK)
        cos = cos_ref[rows, :]
        sin = sin_ref[rows, :]
        q = _rope128(q_ref[rows, :].astype(F32), cos, sin)
        k = _rope128(k_ref[rows, :].astype(F32), cos, sin) * (RET_DK ** -0.5)
        v = v_ref[rows, :]
        st = st_ref[...]
        o_ref[rows, :] = jnp.dot((q * qw).astype(BF16), st.astype(BF16), preferred_element_type=F32)
        kv = jnp.dot((k * kw).T.astype(BF16), v, preferred_element_type=F32)
        st_ref[...] = st * cd + kv


def _ret_fwd_body(q_ref, k_ref, v_ref, g_ref, yb_ref, cos_ref, sin_ref, dm_ref, qw_ref, kw_ref, cd_ref,
                  gn_ref, o_ref, st_ref, *, nchunk):
    @pl.when(pl.program_id(2) == 0)
    def _():
        st_ref[...] = jnp.zeros_like(st_ref)

    qw = qw_ref[0]
    kw = kw_ref[0]
    cd = cd_ref[0, 0:1, :]
    dm = dm_ref[0]
    gn = gn_ref[0, 0:1, :]
    for c in range(nchunk):
        rows = pl.ds(c * CHUNK, CHUNK)
        cos = cos_ref[rows, :]
        sin = sin_ref[rows, :]
        q = _rope128(q_ref[rows, :].astype(F32), cos, sin)
        k = _rope128(k_ref[rows, :].astype(F32), cos, sin) * (RET_DK ** -0.5)
        v = v_ref[rows, :]
        st = st_ref[...]
        s = lax.dot_general(q.astype(BF16), k.astype(BF16), (((1,), (1,)), ((), ())),
                            preferred_element_type=F32) * dm
        y = jnp.dot(s.astype(BF16), v, preferred_element_type=F32)
        y += jnp.dot((q * qw).astype(BF16), st.astype(BF16), preferred_element_type=F32)
        y += yb_ref[rows, :]
        kv = jnp.dot((k * kw).T.astype(BF16), v, preferred_element_type=F32)
        st_ref[...] = st * cd + kv
        y = y * lax.rsqrt(jnp.mean(y * y, axis=-1, keepdims=True) + EPS) * gn
        g = g_ref[rows, :].astype(F32)
        o_ref[rows, :] = (g * _sigmoid(g) * y).astype(o_ref.dtype)


def _retention(z, ret_decay, ret_norm, cos, sin, batch, seq, ts):
    t = z.shape[0]
    nchunk = ts // CHUNK
    nstep = seq // ts
    hd = RET_HEADS
    log_g = jax.nn.log_sigmoid(ret_decay.astype(F32))
    idx = jnp.arange(CHUNK, dtype=F32)
    diff = idx[:, None] - idx[None, :]
    lf = log_g[0][:, None, None]
    lb = log_g[1][:, None, None]
    dmat = jnp.where(diff[None] >= 0, jnp.exp(lf * jnp.maximum(diff, 0.0)[None]),
                     jnp.exp(lb * jnp.maximum(-diff, 0.0)[None]))
    bc = lambda a: jnp.broadcast_to(a[:, :, None], (hd, a.shape[1], LANE))
    qw_f = bc(jnp.exp(log_g[0][:, None] * (idx + 1.0)[None, :]))
    kw_f = bc(jnp.exp(log_g[0][:, None] * (CHUNK - 1 - idx)[None, :]))
    qw_b = bc(jnp.exp(log_g[1][:, None] * (CHUNK - idx)[None, :]))
    kw_b = bc(jnp.exp(log_g[1][:, None] * idx[None, :]))
    cd_f = jnp.broadcast_to(jnp.exp(log_g[0] * CHUNK)[:, None, None], (hd, 8, LANE))
    cd_b = jnp.broadcast_to(jnp.exp(log_g[1] * CHUNK)[:, None, None], (hd, 8, LANE))
    gn = jnp.broadcast_to(ret_norm.astype(F32).reshape(hd, 1, LANE), (hd, 8, LANE))

    def zspec(col0, rev):
        cb = col0 // LANE
        if rev:
            return pl.BlockSpec((ts, LANE), lambda b, h, s: (b * nstep + nstep - 1 - s, cb + h))
        return pl.BlockSpec((ts, LANE), lambda b, h, s: (b * nstep + s, cb + h))

    def tspec(rev):
        if rev:
            return pl.BlockSpec((ts, LANE), lambda b, h, s: (nstep - 1 - s, 0))
        return pl.BlockSpec((ts, LANE), lambda b, h, s: (s, 0))

    hspec = lambda r: pl.BlockSpec((1, r, LANE), lambda b, h, s: (h, 0, 0))

    yb = pl.pallas_call(
        functools.partial(_ret_bwd_body, nchunk=nchunk),
        grid=(batch, hd, nstep),
        in_specs=[zspec(Z_QR, True), zspec(Z_KR, True), zspec(Z_VR, True), tspec(True), tspec(True),
                  hspec(CHUNK), hspec(CHUNK), hspec(8)],
        out_specs=pl.BlockSpec((ts, LANE), lambda b, h, s: (b * nstep + nstep - 1 - s, h)),
        out_shape=jax.ShapeDtypeStruct((t, hd * LANE), F32),
        scratch_shapes=[pltpu.VMEM((RET_DK, LANE), F32)],
        compiler_params=_params("parallel", "parallel", "arbitrary"),
        name="retention_bwd",
    )(z, z, z, cos, sin, qw_b, kw_b, cd_b)

    return pl.pallas_call(
        functools.partial(_ret_fwd_body, nchunk=nchunk),
        grid=(batch, hd, nstep),
        in_specs=[zspec(Z_QR, False), zspec(Z_KR, False), zspec(Z_VR, False), zspec(Z_GR, False),
                  pl.BlockSpec((ts, LANE), lambda b, h, s: (b * nstep + s, h)),
                  tspec(False), tspec(False), hspec(CHUNK), hspec(CHUNK), hspec(CHUNK), hspec(8), hspec(8)],
        out_specs=pl.BlockSpec((ts, LANE), lambda b, h, s: (b * nstep + s, h)),
        out_shape=jax.ShapeDtypeStruct((t, hd * LANE), BF16),
        scratch_shapes=[pltpu.VMEM((RET_DK, LANE), F32)],
        compiler_params=_params("parallel", "parallel", "arbitrary"),
        name="retention_fwd",
    )(z, z, z, z, yb, cos, sin, dmat, qw_f, kw_f, cd_f, gn)


def _sgu_body(u_ref, v_ref, lg_ref, lb_ref, w_ref, b_ref, o_ref, *, nchunk):
    lg = lg_ref[...]
    lb = lb_ref[...]
    for c in range(nchunk):
        rows = pl.ds(c * CHUNK, CHUNK)
        vf = _gelu(v_ref[rows, :].astype(F32))
        mu = jnp.mean(vf, axis=-1, keepdims=True)
        vc = vf - mu
        var = jnp.mean(vc * vc, axis=-1, keepdims=True)
        vn = (vc * lax.rsqrt(var + EPS) * lg + lb).astype(BF16)
        for g in range(SGU_GROUPS):
            cols = slice(g * LANE, (g + 1) * LANE)
            mixed = jnp.dot(w_ref[g], vn[:, cols], preferred_element_type=F32) + b_ref[g]
            uf = _gelu(u_ref[rows, cols].astype(F32))
            o_ref[rows, cols] = (uf * mixed).astype(o_ref.dtype)


def _sgu(z, ln_g, ln_b, w_s, b_s, ts):
    t = z.shape[0]
    bias = jnp.broadcast_to(b_s.astype(F32)[:, :, None], (SGU_GROUPS, CHUNK, LANE))
    return pl.pallas_call(
        functools.partial(_sgu_body, nchunk=ts // CHUNK),
        grid=(t // ts,),
        in_specs=[pl.BlockSpec((ts, SGU_WIDTH), lambda i: (i, Z_US // SGU_WIDTH)),
                  pl.BlockSpec((ts, SGU_WIDTH), lambda i: (i, Z_VS // SGU_WIDTH)),
                  pl.BlockSpec((1, SGU_WIDTH), lambda i: (0, 0)),
                  pl.BlockSpec((1, SGU_WIDTH), lambda i: (0, 0)),
                  pl.BlockSpec((SGU_GROUPS, CHUNK, CHUNK), lambda i: (0, 0, 0)),
                  pl.BlockSpec((SGU_GROUPS, CHUNK, LANE), lambda i: (0, 0, 0))],
        out_specs=pl.BlockSpec((ts, SGU_WIDTH), lambda i: (i, 0)),
        out_shape=jax.ShapeDtypeStruct((t, SGU_WIDTH), BF16),
        compiler_params=_params("parallel"),
        name="spatial_gating",
    )(z, z, ln_g.reshape(1, -1).astype(F32), ln_b.reshape(1, -1).astype(F32), w_s.astype(BF16), bias)


def _pair_norm_rope(x, gain, cos, sin_up, sin_dn, low):
    sq = x * x
    lo = jnp.sum(jnp.where(low, sq, 0.0), axis=-1, keepdims=True)
    hi = jnp.sum(sq, axis=-1, keepdims=True) - lo
    ms = jnp.where(low, lo, hi) * (1.0 / ATT_DH)
    xn = x * lax.rsqrt(ms + EPS) * gain
    return xn * cos + pltpu.roll(xn, LANE - 32, 1) * sin_up + pltpu.roll(xn, 32, 1) * sin_dn


def _attn_body(sink_ref, q_ref, kp_ref, k_ref, kn_ref, vp_ref, v_ref, vn_ref,
               cq_ref, suq_ref, sdq_ref, ckp_ref, sukp_ref, sdkp_ref, ckn_ref, sukn_ref, sdkn_ref,
               qg_ref, kg_ref, o_ref, *, nchunk, nstep):
    s_id = pl.program_id(1)
    ts = nchunk * CHUNK
    lane = lax.broadcasted_iota(jnp.int32, (1, LANE), 1)
    low = lane < ATT_DH
    kg = kg_ref[...]
    qg = qg_ref[...]
    k_ext = jnp.concatenate([
        _pair_norm_rope(kp_ref[...].astype(F32), kg, ckp_ref[...], sukp_ref[...], sdkp_ref[...], low),
        _pair_norm_rope(k_ref[...].astype(F32), kg, cq_ref[...], suq_ref[...], sdq_ref[...], low),
        _pair_norm_rope(kn_ref[...].astype(F32), kg, ckn_ref[...], sukn_ref[...], sdkn_ref[...], low)], axis=0)
    v_ext = jnp.concatenate([vp_ref[...], v_ref[...], vn_ref[...]], axis=0)
    k_lo = jnp.where(low, k_ext, 0.0).astype(BF16)
    k_hi = jnp.where(low, 0.0, k_ext).astype(BF16)
    qi = lax.broadcasted_iota(jnp.int32, (CHUNK, 3 * CHUNK), 0)
    kj = lax.broadcasted_iota(jnp.int32, (CHUNK, 3 * CHUNK), 1)
    band = jnp.abs(qi + CHUNK - kj) <= CHUNK
    for c in range(nchunk):
        rows = pl.ds(c * CHUNK, CHUNK)
        first = jnp.logical_and(s_id == 0, c == 0)
        last = jnp.logical_and(s_id == nstep - 1, c == nchunk - 1)
        valid = band
        if c == 0:
            valid = jnp.logical_and(valid, jnp.logical_or(kj >= CHUNK, jnp.logical_not(first)))
        if c == nchunk - 1:
            valid = jnp.logical_and(valid, jnp.logical_or(kj < 2 * CHUNK, jnp.logical_not(last)))
        kc_lo = k_lo[c * CHUNK:(c + 3) * CHUNK]
        kc_hi = k_hi[c * CHUNK:(c + 3) * CHUNK]
        vc = v_ext[c * CHUNK:(c + 3) * CHUNK]
        cos = cq_ref[rows, :]
        su = suq_ref[rows, :]
        sd = sdq_ref[rows, :]
        for pair in range(ATT_HEADS // 2):
            cols = slice(pair * LANE, (pair + 1) * LANE)
            qp = _pair_norm_rope(q_ref[rows, cols].astype(F32), qg, cos, su, sd, low) * (ATT_DH ** -0.5)
            kv_head = (2 * pair) // ATT_GROUP
            outs = []
            for half in range(2):
                head = 2 * pair + half
                qh = qp if half == kv_head else pltpu.roll(qp, ATT_DH, 1)
                if kv_head == 0:
                    qh = jnp.where(low, qh, 0.0)
                    kc = kc_lo
                else:
                    qh = jnp.where(low, 0.0, qh)
                    kc = kc_hi
                s = lax.dot_general(qh.astype(BF16), kc, (((1,), (1,)), ((), ())), preferred_element_type=F32)
                s = jnp.where(valid, s, NEG_INF)
                sk = sink_ref[head]
                m = jnp.maximum(jnp.max(s, axis=-1, keepdims=True), sk)
                e = jnp.exp(s - m)
                den = jnp.sum(e, axis=-1, keepdims=True) + jnp.exp(sk - m)
                o = jnp.dot(e.astype(BF16), vc, preferred_element_type=F32) / den
                outs.append(o if half == kv_head else pltpu.roll(o, ATT_DH, 1))
            o_ref[rows, cols] = jnp.where(low, outs[0], outs[1]).astype(o_ref.dtype)


def _attention(z, q_gain, k_gain, sink, cos, sin, batch, seq, ts):
    t = z.shape[0]
    nchunk = ts // CHUNK
    nstep = seq // ts
    nblk = seq // CHUNK
    cos2 = jnp.tile(jnp.concatenate([cos, cos], axis=1), (1, 2))
    zero = jnp.zeros_like(sin)
    sin_up = jnp.tile(jnp.concatenate([-sin, zero], axis=1), (1, 2))
    sin_dn = jnp.tile(jnp.concatenate([zero, sin], axis=1), (1, 2))
    qg = jnp.tile(q_gain.astype(F32), 2).reshape(1, LANE)
    kg = jnp.tile(k_gain.astype(F32), 2).reshape(1, LANE)

    kcb, vcb = Z_KA // LANE, Z_VA // LANE
    prev_blk = lambda s: jnp.maximum(s * nchunk - 1, 0)
    next_blk = lambda s: jnp.minimum((s + 1) * nchunk, nblk - 1)
    main = lambda cb: pl.BlockSpec((ts, LANE), lambda b, s: (b * nstep + s, cb))
    prev = lambda cb: pl.BlockSpec((CHUNK, LANE), lambda b, s: (b * nblk + prev_blk(s), cb))
    nxt = lambda cb: pl.BlockSpec((CHUNK, LANE), lambda b, s: (b * nblk + next_blk(s), cb))
    tmain = pl.BlockSpec((ts, LANE), lambda b, s: (s, 0))
    tprev = pl.BlockSpec((CHUNK, LANE), lambda b, s: (prev_blk(s), 0))
    tnext = pl.BlockSpec((CHUNK, LANE), lambda b, s: (next_blk(s), 0))
    one = pl.BlockSpec((1, LANE), lambda b, s: (0, 0))
    return pl.pallas_call(
        functools.partial(_attn_body, nchunk=nchunk, nstep=nstep),
        grid=(batch, nstep),
        in_specs=[pl.BlockSpec(memory_space=pltpu.SMEM),
                  pl.BlockSpec((ts, ATT_HEADS * ATT_DH), lambda b, s: (b * nstep + s, Z_QA // 512)),
                  prev(kcb), main(kcb), nxt(kcb), prev(vcb), main(vcb), nxt(vcb),
                  tmain, tmain, tmain, tprev, tprev, tprev, tnext, tnext, tnext, one, one],
        out_specs=pl.BlockSpec((ts, ATT_HEADS * ATT_DH), lambda b, s: (b * nstep + s, 0)),
        out_shape=jax.ShapeDtypeStruct((t, ATT_HEADS * ATT_DH), BF16),
        compiler_params=_params("parallel", "arbitrary"),
        name="window_attention",
    )(sink.astype(F32), z, z, z, z, z, z, z,
      cos2, sin_up, sin_dn, cos2, sin_up, sin_dn, cos2, sin_up, sin_dn, qg, kg)


def _merge_body(h_ref, gr_ref, gs_ref, ga_ref, yr_ref, ys_ref, ya_ref, wr_ref, ws_ref, wa_ref, wo_ref, o_ref):
    m = _sigmoid(gr_ref[...].astype(F32)) * jnp.dot(yr_ref[...], wr_ref[...], preferred_element_type=F32)
    m += _sigmoid(gs_ref[...].astype(F32)) * jnp.dot(ys_ref[...], ws_ref[...], preferred_element_type=F32)
    m += _sigmoid(ga_ref[...].astype(F32)) * jnp.dot(ya_ref[...], wa_ref[...], preferred_element_type=F32)
    o_ref[...] = h_ref[...] + jnp.dot(m.astype(BF16), wo_ref[...], preferred_element_type=F32)


def _merge(h, z, y_r, y_s, y_a, w_r, w_s, w_a, w_o, tm):
    t, d = h.shape
    row = lambda w: pl.BlockSpec((tm, w), lambda i: (i, 0))
    gate = lambda col0: pl.BlockSpec((tm, d), lambda i: (i, col0 // d))
    full = lambda a: pl.BlockSpec(a.shape, lambda i: (0, 0))
    return pl.pallas_call(
        _merge_body,
        grid=(t // tm,),
        in_specs=[row(d), gate(Z_GATE_R), gate(Z_GATE_S), gate(Z_GATE_A), row(512), row(512), row(512),
                  full(w_r), full(w_s), full(w_a), full(w_o)],
        out_specs=row(d),
        out_shape=jax.ShapeDtypeStruct((t, d), F32),
        compiler_params=_params("parallel"),
        name="branch_merge",
    )(h, z, z, z, y_r, y_s, y_a, w_r, w_s, w_a, w_o)


ID_NONE = 1 << 20


def _top16(vals, ids):
    out_v, out_i = [], []
    for _ in range(PEER_TOPK):
        m = jnp.max(vals, axis=0, keepdims=True)
        idx = jnp.min(jnp.where(vals == m, ids, ID_NONE), axis=0, keepdims=True)
        out_v.append(m)
        out_i.append(idx)
        vals = jnp.where(ids == idx, -jnp.inf, vals)
    return out_v, out_i


def _pair_blocks(first, second):
    rows1, stack1 = first
    rows2, stack2 = second
    blocks = [(stack1, rows2[0])]
    blocks += [(stack1[0:8], rows2[b]) for b in range(1, 8)]
    blocks += [(rows1[0], stack2[8:16])]
    return blocks


def _pair_ids(tm):
    a8 = lax.broadcasted_iota(jnp.int32, (8, tm), 0)
    a16 = lax.broadcasted_iota(jnp.int32, (PEER_TOPK, tm), 0)
    blocks = [a16 * PEER_TOPK]
    blocks += [jnp.where(a8 < PEER_TOPK // (b + 1), a8 * PEER_TOPK + b, ID_NONE) for b in range(1, 8)]
    blocks += [a8 + 8]
    return jnp.concatenate(blocks, axis=0)


def _route_body(h_ref, g_ref, wq_ref, keys_ref, hn_ref, eid_ref, gate_ref):
    hn = _rms(h_ref[...], g_ref[...])
    hn_ref[...] = _pack_halves(hn)
    q = jnp.dot(hn.astype(BF16), wq_ref[...], preferred_element_type=F32).astype(BF16)
    tm = q.shape[0]
    rows = lax.broadcasted_iota(jnp.int32, (PEER_KEYS, tm), 0)
    pair_ids = _pair_ids(tm)
    pair_ok = pair_ids != ID_NONE
    half = PEER_QDIM // 2
    eids, gates = [], []
    for hd in range(PEER_HEADS):
        sub = []
        for p in range(2):
            qs = q[:, (2 * hd + p) * half:(2 * hd + p + 1) * half]
            s = lax.dot_general(keys_ref[hd, p], qs, (((1,), (1,)), ((), ())), preferred_element_type=F32)
            sub.append(_top16(s, rows))
        (s1, i1), (s2, i2) = sub
        scores = _pair_blocks((s1, jnp.concatenate(s1, axis=0)), (s2, jnp.concatenate(s2, axis=0)))
        experts = _pair_blocks((i1, jnp.concatenate(i1, axis=0)), (i2, jnp.concatenate(i2, axis=0)))
        cand_s = jnp.where(pair_ok, jnp.concatenate([a + b for a, b in scores], axis=0), -jnp.inf)
        cand_e = jnp.concatenate([a * PEER_KEYS + b for a, b in experts], axis=0)
        top_s, sel = _top16(cand_s, pair_ids)
        top_e = [jnp.sum(jnp.where(pair_ids == i, cand_e, 0), axis=0, keepdims=True) for i in sel]
        ts_ = jnp.concatenate(top_s, axis=0)
        e = jnp.exp(ts_ - top_s[0])
        gates.append(e / jnp.sum(e, axis=0, keepdims=True))
        eids.append(jnp.concatenate(top_e, axis=0))
    eid_ref[...] = jnp.concatenate(eids, axis=0).T
    gate_ref[...] = jnp.concatenate(gates, axis=0).T


def _route(h, gain, wq_bf, keys_bf, tm):
    t, d = h.shape
    return pl.pallas_call(
        _route_body,
        grid=(t // tm,),
        in_specs=[pl.BlockSpec((tm, d), lambda i: (i, 0)),
                  pl.BlockSpec((1, d), lambda i: (0, 0)),
                  pl.BlockSpec(wq_bf.shape, lambda i: (0, 0)),
                  pl.BlockSpec(keys_bf.shape, lambda i: (0, 0, 0, 0))],
        out_specs=[pl.BlockSpec((tm, d // 2), lambda i: (i, 0)),
                   pl.BlockSpec((tm, PEER_PICKS), lambda i: (i, 0)),
                   pl.BlockSpec((tm, PEER_PICKS), lambda i: (i, 0))],
        out_shape=[jax.ShapeDtypeStruct((t, d // 2), jnp.int32),
                   jax.ShapeDtypeStruct((t, PEER_PICKS), jnp.int32),
                   jax.ShapeDtypeStruct((t, PEER_PICKS), F32)],
        compiler_params=_params("parallel"),
        name="peer_route",
    )(h, gain.reshape(1, d), wq_bf, keys_bf)


def _sc_mesh():
    return plsc.VectorSubcoreMesh(core_axis_name="core", subcore_axis_name="subcore")


def _sc_worker():
    return lax.axis_index("subcore") * SC_CORES + lax.axis_index("core")


def _sc_row_pipeline(tab_hbm, idx_v, buf, sems, compute):
    nslot, nrow = buf.shape[0], buf.shape[1]
    nsub = PEER_PICKS // nrow
    ng = idx_v.shape[0] * nsub

    def gather(g, slot):
        rows = idx_v.at[g // nsub, pl.ds((g % nsub) * nrow, nrow)]
        return pltpu.make_async_copy(tab_hbm.at[rows], buf.at[slot], sems.at[slot])

    for b in range(nslot - 1):
        gather(b, b).start()

    @pl.loop(0, ng, step=nslot)
    def _(g):
        for b in range(nslot):
            ahead = g + b + nslot - 1

            @pl.when(ahead < ng)
            def _():
                gather(ahead, (b + nslot - 1) % nslot).start()

            gather(g + b, b).wait()
            compute(g + b, b)


def _sc_split(words):
    return (lax.bitcast_convert_type(words & HI_MASK, F32), lax.bitcast_convert_type(words << 16, F32))


def _expert_scores(table, eid, hn):
    t, words = hn.shape
    per_worker = t // SC_WORKERS
    nslot, nrow = SC_SCORE_RING
    ntok = SC_SCORE_BLOCK
    nsub = PEER_PICKS // nrow
    ln = SC_LANES

    @functools.partial(
        pl.kernel, mesh=_sc_mesh(),
        out_type=jax.ShapeDtypeStruct((t, PEER_PICKS), F32),
        scratch_types=[pltpu.VMEM((ntok, PEER_PICKS), jnp.int32),
                       pltpu.VMEM((ntok, words), jnp.int32),
                       pltpu.VMEM((ntok, PEER_PICKS), F32),
                       pltpu.VMEM((nslot, nrow, words), jnp.int32),
                       pltpu.SemaphoreType.DMA((nslot,))],
        compiler_params=pltpu.CompilerParams(needs_layout_passes=False),
        name="peer_expert_scores")
    def run(tab_hbm, eid_hbm, hn_hbm, out_hbm, idx_v, x_v, a_v, buf, sems):
        lane = lax.iota(jnp.int32, ln)

        def compute(g, slot):
            tok = g // nsub
            sub = g % nsub
            for grp in range(nrow // ln):
                def body(c, accs):
                    c0 = pl.ds(2 * c * ln, ln)
                    c1 = pl.ds((2 * c + 1) * ln, ln)
                    x0 = plsc.bitcast(x_v[tok, c0], BF16)
                    x1 = plsc.bitcast(x_v[tok, c1], BF16)
                    out = []
                    for r in range(ln):
                        u0 = plsc.bitcast(buf[slot, grp * ln + r, c0], BF16)
                        u1 = plsc.bitcast(buf[slot, grp * ln + r, c1], BF16)
                        hi, lo = _sc_split(plsc.bitcast(u0 * x0 + u1 * x1, jnp.int32))
                        out.append(accs[r] + hi + lo)
                    return tuple(out)

                accs = lax.fori_loop(0, words // (2 * ln), body, tuple(jnp.zeros((ln,), F32) for _ in range(ln)))
                res = jnp.zeros((ln,), F32)
                for r in range(ln):
                    res = jnp.where(lane == r, jnp.sum(accs[r]), res)
                a_v[tok, pl.ds(sub * nrow + grp * ln, ln)] = res

        @pl.loop(0, per_worker // ntok)
        def _(blk):
            tok0 = _sc_worker() * per_worker + blk * ntok
            pltpu.sync_copy(eid_hbm.at[pl.ds(tok0, ntok)], idx_v)
            pltpu.sync_copy(hn_hbm.at[pl.ds(tok0, ntok)], x_v)
            _sc_row_pipeline(tab_hbm, idx_v, buf, sems, compute)
            pltpu.sync_copy(a_v, out_hbm.at[pl.ds(tok0, ntok)])

    return run(table, eid, hn)


def _expert_mix(table, eid, w):
    t = eid.shape[0]
    words = table.shape[1]
    d = 2 * words
    per_worker = t // SC_WORKERS
    nslot, nrow = SC_MIX_RING
    ntok = SC_MIX_BLOCK
    nsub = PEER_PICKS // nrow
    ln = SC_LANES
    nvec = SC_WORDS // ln

    @functools.partial(
        pl.kernel, mesh=_sc_mesh(),
        out_type=jax.ShapeDtypeStruct((t, d), F32),
        scratch_types=[pltpu.VMEM((ntok, PEER_PICKS), jnp.int32),
                       pltpu.VMEM((ntok, PEER_PICKS), jnp.int32),
                       pltpu.VMEM((ntok, d), F32),
                       pltpu.VMEM((nslot, nrow, words), jnp.int32),
                       pltpu.SemaphoreType.DMA((nslot,))],
        compiler_params=pltpu.CompilerParams(needs_layout_passes=False),
        name="peer_expert_mix")
    def run(tab_hbm, eid_hbm, w_hbm, out_hbm, idx_v, w_v, y_v, buf, sems):
        zero = jnp.zeros((ln,), F32)

        def compute(g, slot):
            tok = g // nsub
            sub = g % nsub
            tokv = jnp.full((ln,), tok, jnp.int32)
            first = jnp.full((ln,), sub, jnp.int32) == 0

            def weight(r):
                pick = jnp.full((ln,), sub * nrow + r, jnp.int32)
                return plsc.bitcast(plsc.load_gather(w_v, [tokv, pick]), BF16)

            for cc in range(words // SC_WORDS):
                def body(rg, accs):
                    r0 = SC_MIX_GROUP * rg
                    ws = [weight(r0 + j) for j in range(SC_MIX_GROUP)]
                    out = list(accs)
                    for k in range(nvec):
                        cols = pl.ds(cc * SC_WORDS + k * ln, ln)
                        prod = ws[0] * plsc.bitcast(buf[slot, r0, cols], BF16)
                        for j in range(1, SC_MIX_GROUP):
                            prod = prod + ws[j] * plsc.bitcast(buf[slot, r0 + j, cols], BF16)
                        hi, lo = _sc_split(plsc.bitcast(prod, jnp.int32))
                        out[k] = accs[k] + hi
                        out[nvec + k] = accs[nvec + k] + lo
                    return tuple(out)

                lo_cols = [pl.ds(cc * SC_WORDS + k * ln, ln) for k in range(nvec)]
                hi_cols = [pl.ds(words + cc * SC_WORDS + k * ln, ln) for k in range(nvec)]
                init = tuple(jnp.where(first, zero, y_v[tok, c]) for c in lo_cols + hi_cols)
                accs = lax.fori_loop(0, nrow // SC_MIX_GROUP, body, init)
                for c, acc in zip(lo_cols + hi_cols, accs):
                    y_v[tok, c] = acc

        @pl.loop(0, per_worker // ntok)
        def _(blk):
            tok0 = _sc_worker() * per_worker + blk * ntok
            pltpu.sync_copy(eid_hbm.at[pl.ds(tok0, ntok)], idx_v)
            pltpu.sync_copy(w_hbm.at[pl.ds(tok0, ntok)], w_v)
            _sc_row_pipeline(tab_hbm, idx_v, buf, sems, compute)
            pltpu.sync_copy(y_v, out_hbm.at[pl.ds(tok0, ntok)])

    return run(table, eid, w)


def _pick_weights_body(a_ref, g_ref, o_ref):
    o_ref[...] = _pack_twice(g_ref[...] * _gelu(a_ref[...]))


def _pick_weights(a, gate, tm):
    t = a.shape[0]
    spec = pl.BlockSpec((tm, PEER_PICKS), lambda i: (i, 0))
    return pl.pallas_call(
        _pick_weights_body,
        grid=(t // tm,),
        in_specs=[spec, spec],
        out_specs=spec,
        out_shape=jax.ShapeDtypeStruct(a.shape, jnp.int32),
        compiler_params=_params("parallel"),
        name="peer_pick_weights",
    )(a, gate)


def _ple_body(h_ref, y_ref, g_ref, wg_ref, p_ref, wp_ref, o_ref):
    h = h_ref[...] + y_ref[...]
    hn = _rms(h, g_ref[...]).astype(BF16)
    gate = _sigmoid(jnp.dot(hn, wg_ref[...], preferred_element_type=F32))
    emb = jnp.dot(p_ref[...].astype(BF16), wp_ref[...], preferred_element_type=F32)
    o_ref[...] = h + gate * emb


def _ple(h, y, gain, wg_bf, p, wp_bf, tm):
    t, d = h.shape
    return pl.pallas_call(
        _ple_body,
        grid=(t // tm,),
        in_specs=[pl.BlockSpec((tm, d), lambda i: (i, 0)),
                  pl.BlockSpec((tm, d), lambda i: (i, 0)),
                  pl.BlockSpec((1, d), lambda i: (0, 0)),
                  pl.BlockSpec(wg_bf.shape, lambda i: (0, 0)),
                  pl.BlockSpec((tm, p.shape[1]), lambda i: (i, 0)),
                  pl.BlockSpec(wp_bf.shape, lambda i: (0, 0))],
        out_specs=pl.BlockSpec((tm, d), lambda i: (i, 0)),
        out_shape=jax.ShapeDtypeStruct((t, d), F32),
        compiler_params=_params("parallel"),
        name="layer_embedding",
    )(h, y, gain.reshape(1, d), wg_bf, p, wp_bf)


def _rope_tables(seq, dim):
    inv = 1.0 / (ROPE_THETA ** (jnp.arange(0, dim, 2, dtype=F32) / dim))
    ang = jnp.arange(seq, dtype=F32)[:, None] * inv[None, :]
    return jnp.cos(ang), jnp.sin(ang)


def _pack_table(tab):
    n = tab.shape[1] // 2
    b = lax.bitcast_convert_type(tab.astype(BF16), jnp.uint16).astype(jnp.uint32)
    return lax.bitcast_convert_type((b[:, :n] << 16) | b[:, n:], jnp.int32)


def _permute_in_columns(w_in):
    return jnp.concatenate([w_in[:, 3840:], w_in[:, :3840]], axis=1)


def kernel(x, p, norm_mix, w_in, ret_decay, ret_norm, sgu_ln_g, sgu_ln_b, sgu_w, sgu_b, att_q_norm, att_k_norm, att_sink, w_proj_ret, w_proj_sgu, w_proj_att, w_out, norm_ffn, peer_wq, peer_keys, peer_u, peer_v, norm_ple, ple_gate, ple_proj):
    batch, seq, d = x.shape
    depth = w_in.shape[0]
    groups = PIPELINE_GROUPS if batch % PIPELINE_GROUPS == 0 else 1
    gb = batch // groups
    t = gb * seq
    ts = min(512, seq)
    tm = min(512, t)
    cos_r, sin_r = _rope_tables(seq, RET_DK)
    cos_r2 = jnp.concatenate([cos_r, cos_r], axis=1)
    sin_r2 = jnp.concatenate([-sin_r, sin_r], axis=1)
    cos_a, sin_a = _rope_tables(seq, ATT_DH)
    hs = [x[g * gb:(g + 1) * gb].reshape(t, d) for g in range(groups)]
    for i in range(depth):
        w_in_bf = _permute_in_columns(w_in[i]).astype(BF16)
        w_r, w_s, w_a = w_proj_ret[i].astype(BF16), w_proj_sgu[i].astype(BF16), w_proj_att[i].astype(BF16)
        w_o, w_q, keys = w_out[i].astype(BF16), peer_wq[i].astype(BF16), peer_keys[i].astype(BF16)
        w_g, w_p = ple_gate[i].astype(BF16), ple_proj[i].astype(BF16)
        tab_u, tab_v = _pack_table(peer_u[i]), _pack_table(peer_v[i])
        for g in range(groups):
            h = hs[g]
            z = _in_projection(h, norm_mix[i], w_in_bf, tm)
            y_r = _retention(z, ret_decay[i], ret_norm[i], cos_r2, sin_r2, gb, seq, ts)
            y_s = _sgu(z, sgu_ln_g[i], sgu_ln_b[i], sgu_w[i], sgu_b[i], ts)
            y_a = _attention(z, att_q_norm[i], att_k_norm[i], att_sink[i], cos_a, sin_a, gb, seq, ts)
            h = _merge(h, z, y_r, y_s, y_a, w_r, w_s, w_a, w_o, tm)
            hn, eid, gate = _route(h, norm_ffn[i], w_q, keys, min(256, t))
            a = _expert_scores(tab_u, eid, hn)
            w = _pick_weights(a, gate, tm)
            y = _expert_mix(tab_v, eid, w)
            p_g = p[i, g * gb:(g + 1) * gb].reshape(t, -1)
            hs[g] = _ple(h, y, norm_ple[i], w_g, p_g, w_p, tm)
    return jnp.concatenate(hs, axis=0).reshape(batch, seq, d)
```
